```python
import math
import jax
import jax.numpy as jnp
from jax import lax
import numpy as np

D_MODEL = 1024
BATCH = 8
SEQ = 8192
DEPTH = 2

PLE_DIM = 256
MIX_WIDTH = D_MODEL
N_EVEN = (DEPTH + 1) // 2
N_ODD = DEPTH // 2
DEEPNORM_ALPHA = (2.0 * DEPTH) ** 0.25
DEEPNORM_BETA = (8.0 * DEPTH) ** -0.25
LN_EPS = 1e-5
RMS_EPS = 1e-6

DN_HEADS = 4
DN_DK = 128
DN_DV = 128
DN_WIDTH = DN_HEADS * DN_DV
DN_QKV = 2 * DN_HEADS * DN_DK + DN_WIDTH
DN_AB = 4 * DN_HEADS
DN_CONV = 5
DN_CHUNK = 64

RK_HEADS = 8
RK_HEAD = 64
RK_WIDTH = RK_HEADS * RK_HEAD
RK_DECAY_LORA = 64
RK_ICLR_LORA = 64
RK_SHIFT = 3 * RK_WIDTH + 2 * RK_DECAY_LORA + RK_ICLR_LORA
RK_GN_EPS = 64e-5

EVEN_SPLITS = [DN_QKV, DN_QKV + DN_AB, DN_QKV + DN_AB + DN_WIDTH, DN_QKV + DN_AB + DN_WIDTH + RK_SHIFT]
EVEN_IN = DN_QKV + DN_AB + DN_WIDTH + RK_SHIFT + RK_WIDTH
RK_SPLITS = [RK_WIDTH, 2 * RK_WIDTH, 3 * RK_WIDTH, 3 * RK_WIDTH + RK_DECAY_LORA, 3 * RK_WIDTH + 2 * RK_DECAY_LORA]

HY_WIDTH = MIX_WIDTH
HY_ORDER = 2
HY_SHORT = 3
HY_EMB = 33
HY_FILTER_WIDTH = 64
HY_N_FILTERS = HY_ORDER * 2 * HY_WIDTH
HY_TARGET = 1e-2
HY_FAST_PCT = 0.3
HY_SLOW_PCT = 1.5
ODD_IN = (HY_ORDER + 1) * HY_WIDTH + HY_WIDTH

kernel_name = 'bidir_deltanet_rwkv7_hyena_deepnorm_trunk'


def layer_norm(u, g, b):
    uf = u.astype(jnp.float32)
    mu = jnp.mean(uf, -1, keepdims=True)
    var = jnp.mean(jnp.square(uf - mu), -1, keepdims=True)
    return ((uf - mu) * lax.rsqrt(var + LN_EPS) * g + b).astype(u.dtype)


def rms_norm(u, g):
    uf = u.astype(jnp.float32)
    return (uf * lax.rsqrt(jnp.mean(jnp.square(uf), -1, keepdims=True) + RMS_EPS) * g).astype(u.dtype)


def l2_normalize(u):
    return u * lax.rsqrt(jnp.sum(jnp.square(u), -1, keepdims=True) + RMS_EPS)


def dwconv_centred(u, w, b=None):
    width = w.shape[0]
    pad = width // 2
    y = lax.conv_general_dilated(u, w.astype(u.dtype)[:, None, :], window_strides=(1,), padding=[(pad, pad)],
                                 dimension_numbers=('NWC', 'WIO', 'NWC'), feature_group_count=u.shape[-1])
    return y if b is None else y + b.astype(u.dtype)


def bidir_token_shift(s, mu):
    prev = jnp.pad(s, ((0, 0), (1, 0), (0, 0)))[:, :-1]
    nxt = jnp.pad(s, ((0, 0), (0, 1), (0, 0)))[:, 1:]
    return s + mu[0] * (prev - s) + mu[1] * (nxt - s)


def gated_delta_chunked(q, k, v, g, beta):
    bsz, heads, seq, dk = q.shape
    dv = v.shape[-1]
    c = DN_CHUNK
    n = seq // c
    q = q * (dk ** -0.5)
    chunks = lambda t: t.reshape(bsz, heads, n, c, *t.shape[3:])
    q, k, v, g, beta = chunks(q), chunks(k), chunks(v), chunks(g), chunks(beta)
    g = jnp.cumsum(g, axis=-1)
    incl = jnp.tril(jnp.ones((c, c), dtype=bool))
    strict = jnp.tril(jnp.ones((c, c), dtype=bool), -1)
    decay = jnp.exp(jnp.where(incl, g[..., :, None] - g[..., None, :], -jnp.inf))
    k_beta = k * beta[..., None]
    kk = jnp.einsum('bhnid,bhnjd->bhnij', k_beta, k) * decay
    t_mat = jnp.where(strict, kk, 0.0) + jnp.eye(c, dtype=q.dtype)
    u = lax.linalg.triangular_solve(t_mat, v * beta[..., None], left_side=True, lower=True, unit_diagonal=True)
    w = lax.linalg.triangular_solve(t_mat, k_beta * jnp.exp(g)[..., None], left_side=True, lower=True, unit_diagonal=True)
    qk = jnp.einsum('bhnid,bhnjd->bhnij', q, k) * decay
    q_dec = q * jnp.exp(g)[..., None]
    g_last = g[..., -1]
    k_tail = k * jnp.exp(g_last[..., None] - g)[..., None]

    def step(state, xs):
        u_n, w_n, qd_n, qk_n, kt_n, gl_n = xs
        v_new = u_n - jnp.einsum('bhck,bhkv->bhcv', w_n, state)
        o_n = jnp.einsum('bhck,bhkv->bhcv', qd_n, state) + jnp.einsum('bhij,bhjv->bhiv', qk_n, v_new)
        state = state * jnp.exp(gl_n)[..., None, None] + jnp.einsum('bhck,bhcv->bhkv', kt_n, v_new)
        return state, o_n

    xs = tuple(jnp.moveaxis(t, 2, 0) for t in (u, w, q_dec, qk, k_tail, g_last))
    s0 = jnp.zeros((bsz, heads, dk, dv), jnp.float32)
    _, o = lax.scan(step, s0, xs)
    return jnp.moveaxis(o, 0, 2).reshape(bsz, heads, seq, dv)


def rwkv7_scan(r, w, k, v, a, b):
    bsz, seq, heads, hd = r.shape

    def step(state, xs):
        r_t, w_t, k_t, v_t, a_t, b_t = xs
        sa = jnp.einsum('bhvk,bhk->bhv', state, a_t)
        state = state * w_t[:, :, None, :] + sa[..., None] * b_t[:, :, None, :] + v_t[..., None] * k_t[:, :, None, :]
        return state, jnp.einsum('bhvk,bhk->bhv', state, r_t)

    xs = tuple(jnp.moveaxis(t.astype(jnp.float32), 1, 0) for t in (r, w, k, v, a, b))
    _, y = lax.scan(step, jnp.zeros((bsz, heads, hd, hd), jnp.float32), xs)
    return jnp.moveaxis(y, 0, 1)


def rwkv_decay(wd, w0, w2):
    w_log = -jax.nn.softplus(-(w0 + jnp.tanh(wd) @ w2)) - 0.5
    return jnp.exp(-jnp.exp(w_log))


def hyena_position_features(seq):
    bands = (HY_EMB - 1) // 2
    t = jnp.linspace(0.0, 1.0, seq, dtype=jnp.float32)[:, None]
    f = jnp.linspace(1e-4, bands - 1, bands, dtype=jnp.float32)[None, :]
    ang = (2.0 * math.pi / seq) * jnp.arange(seq, dtype=jnp.float32)[:, None] * f
    return jnp.concatenate([t, jnp.cos(ang), -jnp.sin(ang)], axis=-1), t


def hyena_filters(seq, w1, b1, w2, b2, w3, b3, freq, w_out, deltas):
    f32 = jnp.float32
    feats, t = hyena_position_features(seq)
    freq = freq.astype(f32)
    hdn = jnp.sin(freq * (feats @ w1.astype(f32) + b1.astype(f32)))
    hdn = jnp.sin(freq * (hdn @ w2.astype(f32) + b2.astype(f32)))
    hdn = jnp.sin(freq * (hdn @ w3.astype(f32) + b3.astype(f32)))
    h = (hdn @ w_out.astype(f32)) * jnp.exp(-t * jnp.abs(deltas.astype(f32)))
    h = h.reshape(seq, HY_ORDER, 2, HY_WIDTH)
    fwd, bwd = h[:, :, 0], h[:, :, 1]
    circ = jnp.concatenate([fwd, jnp.zeros_like(fwd[:1]), jnp.flip(bwd[1:], axis=0)], axis=0)
    return circ / (jnp.sum(jnp.abs(circ), axis=0, keepdims=True) + RMS_EPS)


def fft_long_conv(u, filt):
    seq = u.shape[1]
    u_f = jnp.fft.rfft(u, n=2 * seq, axis=1)
    h_f = jnp.fft.rfft(filt, axis=0)
    return jnp.fft.irfft(u_f * h_f[None], n=2 * seq, axis=1)[:, :seq]


def even_mixer(x, w_in, dn_conv, dn_a_log, dn_dt_bias, dn_norm, rk_mu, rk_w0, rk_w2, rk_a0, rk_a2,
               rk_k_k, rk_k_a, rk_r_k, rk_ln_w, rk_ln_b):
    bsz, seq, _ = x.shape
    f32 = jnp.float32
    proj = x @ w_in
    dn_qkv, dn_ab, dn_gate, rk_in, rk_gate = jnp.split(proj, EVEN_SPLITS, axis=-1)

    qkv = jax.nn.silu(dwconv_centred(dn_qkv, dn_conv)).astype(f32)
    q, k, v = jnp.split(qkv, [DN_HEADS * DN_DK, 2 * DN_HEADS * DN_DK], axis=-1)
    q = l2_normalize(q.reshape(bsz, seq, DN_HEADS, DN_DK)).transpose(0, 2, 1, 3)
    k = l2_normalize(k.reshape(bsz, seq, DN_HEADS, DN_DK)).transpose(0, 2, 1, 3)
    v = v.reshape(bsz, seq, DN_HEADS, DN_DV).transpose(0, 2, 1, 3)
    ab = dn_ab.astype(f32).reshape(bsz, seq, 4, DN_HEADS).transpose(2, 0, 3, 1)
    g = -jnp.exp(dn_a_log.astype(f32))[:, None, :, None] * jax.nn.softplus(ab[:2] + dn_dt_bias.astype(f32)[:, None, :, None])
    beta = jax.nn.sigmoid(ab[2:])
    flip_l = lambda t: jnp.flip(t, axis=2)
    o = gated_delta_chunked(q, k, v, g[0], beta[0]) + flip_l(
        gated_delta_chunked(flip_l(q), flip_l(k), flip_l(v), flip_l(g[1]), flip_l(beta[1])))
    o = rms_norm(o.transpose(0, 2, 1, 3), dn_norm).reshape(bsz, seq, DN_WIDTH)
    y_dn = o.astype(x.dtype) * jax.nn.silu(dn_gate)

    s = bidir_token_shift(rk_in, rk_mu).astype(f32)
    r, k, v, wd_f, wd_b, a_d = jnp.split(s, RK_SPLITS, axis=-1)
    w_f = rwkv_decay(wd_f, rk_w0[0], rk_w2[0])
    w_b = rwkv_decay(wd_b, rk_w0[1], rk_w2[1])
    a = jax.nn.sigmoid(rk_a0 + a_d @ rk_a2)
    heads = lambda t: t.reshape(bsz, seq, RK_HEADS, RK_HEAD)
    kk = l2_normalize(heads(k * rk_k_k))
    k = k * (1.0 + (a - 1.0) * rk_k_a)
    r, k, v, a, w_f, w_b = heads(r), heads(k), heads(v), heads(a), heads(w_f), heads(w_b)
    a_vec = -kk
    b_vec = kk * a
    flip_t = lambda t: jnp.flip(t, axis=1)
    wkv = rwkv7_scan(r, w_f, k, v, a_vec, b_vec) + flip_t(
        rwkv7_scan(flip_t(r), flip_t(w_b), flip_t(k), flip_t(v), flip_t(a_vec), flip_t(b_vec)))
    mean = jnp.mean(wkv, -1, keepdims=True)
    var = jnp.mean(jnp.square(wkv - mean), -1, keepdims=True)
    wkv = ((wkv - mean) * lax.rsqrt(var + RK_GN_EPS)).reshape(bsz, seq, RK_WIDTH) * rk_ln_w + rk_ln_b
    bonus = (jnp.sum(r * k * rk_r_k, -1, keepdims=True) * v).reshape(bsz, seq, RK_WIDTH)
    y_rk = (wkv + bonus).astype(x.dtype) * jax.nn.silu(rk_gate)
    return jnp.concatenate([y_dn, y_rk], axis=-1)


def odd_mixer(x, w_in, conv_w, conv_b, f_w1, f_b1, f_w2, f_b2, f_w3, f_b3, f_freq, f_out, deltas, skip):
    bsz, seq, _ = x.shape
    f32 = jnp.float32
    proj = x @ w_in
    xv, gate = jnp.split(proj, [(HY_ORDER + 1) * HY_WIDTH], axis=-1)
    xv = dwconv_centred(xv, conv_w, conv_b).astype(f32)
    x1, x2, v = jnp.split(xv, 3, axis=-1)
    filt = hyena_filters(seq, f_w1, f_b1, f_w2, f_b2, f_w3, f_b3, f_freq, f_out, deltas)
    skip = skip.astype(f32)
    z = x1 * (fft_long_conv(v, filt[:, 0]) + skip[0] * v)
    y = x2 * (fft_long_conv(z, filt[:, 1]) + skip[1] * z)
    return y.astype(x.dtype) * jax.nn.silu(gate)


def setup_inputs(seed: int = 0) -> dict:
    key = jax.random.key(seed)
    keys = iter(list(jax.random.split(key, 48)))
    f32 = jnp.float32
    nrm = lambda shape, scale: jax.random.normal(next(keys), shape, f32) * scale
    unif = lambda shape, lo, hi: jax.random.uniform(next(keys), shape, f32, lo, hi)
    x = nrm((BATCH, SEQ, D_MODEL), 1.0)
    p = nrm((DEPTH, BATCH, SEQ, PLE_DIM), 1.0)
    even_w_in = nrm((N_EVEN, D_MODEL, EVEN_IN), D_MODEL ** -0.5)
    dn_conv = nrm((N_EVEN, DN_CONV, DN_QKV), DN_CONV ** -0.5)
    dn_a_log = jnp.log(unif((N_EVEN, 2, DN_HEADS), 1.0, 16.0))
    dt = jnp.exp(unif((N_EVEN, 2, DN_HEADS), math.log(1e-3), math.log(1e-1)))
    dn_dt_bias = dt + jnp.log(-jnp.expm1(-dt))
    dn_norm = 1.0 + nrm((N_EVEN, DN_DV), 0.02)
    rk_mu = unif((N_EVEN, 2, RK_SHIFT), 0.0, 0.5)
    rk_w0 = unif((N_EVEN, 2, RK_WIDTH), -6.0, -1.0)
    rk_w2 = nrm((N_EVEN, 2, RK_DECAY_LORA, RK_WIDTH), 0.1 * RK_DECAY_LORA ** -0.5)
    rk_a0 = nrm((N_EVEN, RK_WIDTH), 0.1)
    rk_a2 = nrm((N_EVEN, RK_ICLR_LORA, RK_WIDTH), 0.5 * RK_ICLR_LORA ** -0.5)
    rk_k_k = 0.85 + nrm((N_EVEN, RK_WIDTH), 0.02)
    rk_k_a = 1.0 + nrm((N_EVEN, RK_WIDTH), 0.02)
    rk_r_k = -0.04 + nrm((N_EVEN, RK_HEADS, RK_HEAD), 0.1)
    rk_ln_w = 1.0 + nrm((N_EVEN, RK_WIDTH), 0.02)
    rk_ln_b = nrm((N_EVEN, RK_WIDTH), 0.02)
    odd_w_in = nrm((N_ODD, D_MODEL, ODD_IN), D_MODEL ** -0.5)
    hy_conv_w = nrm((N_ODD, HY_SHORT, (HY_ORDER + 1) * HY_WIDTH), HY_SHORT ** -0.5)
    hy_conv_b = nrm((N_ODD, (HY_ORDER + 1) * HY_WIDTH), 0.02)
    hy_ffn_w1 = nrm((N_ODD, HY_EMB, HY_FILTER_WIDTH), HY_EMB ** -0.5)
    hy_ffn_b1 = nrm((N_ODD, HY_FILTER_WIDTH), 0.1)
    hy_ffn_w2 = nrm((N_ODD, HY_FILTER_WIDTH, HY_FILTER_WIDTH), HY_FILTER_WIDTH ** -0.5)
    hy_ffn_b2 = nrm((N_ODD, HY_FILTER_WIDTH), 0.1)
    hy_ffn_w3 = nrm((N_ODD, HY_FILTER_WIDTH, HY_FILTER_WIDTH), HY_FILTER_WIDTH ** -0.5)
    hy_ffn_b3 = nrm((N_ODD, HY_FILTER_WIDTH), 0.1)
    hy_ffn_freq = 1.0 + nrm((N_ODD, HY_FILTER_WIDTH), 0.02)
    hy_ffn_out = nrm((N_ODD, HY_FILTER_WIDTH, HY_N_FILTERS), HY_FILTER_WIDTH ** -0.5)
    max_decay = math.log(HY_TARGET) / HY_FAST_PCT
    min_decay = math.log(HY_TARGET) / HY_SLOW_PCT
    base = jnp.abs(jnp.linspace(min_decay, max_decay, HY_WIDTH, dtype=f32))
    hy_deltas = jnp.tile(base, HY_ORDER * 2)[None] + nrm((N_ODD, HY_N_FILTERS), 0.01)
    hy_skip = nrm((N_ODD, HY_ORDER, HY_WIDTH), 0.5)
    w_out = nrm((DEPTH, MIX_WIDTH, D_MODEL), MIX_WIDTH ** -0.5 * DEEPNORM_BETA)
    ln_g = 1.0 + nrm((DEPTH, D_MODEL), 0.02)
    ln_b = nrm((DEPTH, D_MODEL), 0.02)
    ple_w = nrm((DEPTH, PLE_DIM, D_MODEL), PLE_DIM ** -0.5)
    ple_norm = 1.0 + nrm((DEPTH, D_MODEL), 0.02)
    ple_gate = nrm((DEPTH, D_MODEL, D_MODEL), D_MODEL ** -0.5)
    return {'x': x, 'p': p, 'even_w_in': even_w_in, 'dn_conv': dn_conv, 'dn_a_log': dn_a_log,
            'dn_dt_bias': dn_dt_bias, 'dn_norm': dn_norm, 'rk_mu': rk_mu, 'rk_w0': rk_w0, 'rk_w2': rk_w2,
            'rk_a0': rk_a0, 'rk_a2': rk_a2, 'rk_k_k': rk_k_k, 'rk_k_a': rk_k_a, 'rk_r_k': rk_r_k,
            'rk_ln_w': rk_ln_w, 'rk_ln_b': rk_ln_b, 'odd_w_in': odd_w_in, 'hy_conv_w': hy_conv_w,
            'hy_conv_b': hy_conv_b, 'hy_ffn_w1': hy_ffn_w1, 'hy_ffn_b1': hy_ffn_b1, 'hy_ffn_w2': hy_ffn_w2,
            'hy_ffn_b2': hy_ffn_b2, 'hy_ffn_w3': hy_ffn_w3, 'hy_ffn_b3': hy_ffn_b3, 'hy_ffn_freq': hy_ffn_freq,
            'hy_ffn_out': hy_ffn_out, 'hy_deltas': hy_deltas, 'hy_skip': hy_skip, 'w_out': w_out,
            'ln_g': ln_g, 'ln_b': ln_b, 'ple_w': ple_w, 'ple_norm': ple_norm, 'ple_gate': ple_gate}


def reference(x, p, even_w_in, dn_conv, dn_a_log, dn_dt_bias, dn_norm, rk_mu, rk_w0, rk_w2, rk_a0, rk_a2,
              rk_k_k, rk_k_a, rk_r_k, rk_ln_w, rk_ln_b, odd_w_in, hy_conv_w, hy_conv_b, hy_ffn_w1, hy_ffn_b1,
              hy_ffn_w2, hy_ffn_b2, hy_ffn_w3, hy_ffn_b3, hy_ffn_freq, hy_ffn_out, hy_deltas, hy_skip,
              w_out, ln_g, ln_b, ple_w, ple_norm, ple_gate):
    h = x
    for i in range(DEPTH):
        j = i // 2
        if i % 2 == 0:
            mix = even_mixer(h, even_w_in[j], dn_conv[j], dn_a_log[j], dn_dt_bias[j], dn_norm[j], rk_mu[j],
                             rk_w0[j], rk_w2[j], rk_a0[j], rk_a2[j], rk_k_k[j], rk_k_a[j], rk_r_k[j],
                             rk_ln_w[j], rk_ln_b[j])
        else:
            mix = odd_mixer(h, odd_w_in[j], hy_conv_w[j], hy_conv_b[j], hy_ffn_w1[j], hy_ffn_b1[j],
                            hy_ffn_w2[j], hy_ffn_b2[j], hy_ffn_w3[j], hy_ffn_b3[j], hy_ffn_freq[j],
                            hy_ffn_out[j], hy_deltas[j], hy_skip[j])
        h = layer_norm(DEEPNORM_ALPHA * h + mix @ w_out[i], ln_g[i], ln_b[i])
        e = rms_norm(p[i] @ ple_w[i], ple_norm[i])
        h = h + jax.nn.sigmoid(h @ ple_gate[i]) * e
    return h
```

```python
import functools
import math

import numpy as np
import jax
import jax.numpy as jnp
from jax import lax
from jax.experimental import pallas as pl
from jax.experimental.pallas import tpu as pltpu

F32 = jnp.float32
BF16 = jnp.bfloat16

LN_EPS = 1e-5
RMS_EPS = 1e-6

DN_HEADS = 4
DN_DK = 128
DN_DV = 128
DN_WIDTH = DN_HEADS * DN_DV
DN_QKV = 2 * DN_HEADS * DN_DK + DN_WIDTH
DN_AB = 4 * DN_HEADS
DN_CONV = 5
DN_CHUNK = 64

RK_HEADS = 8
RK_HEAD = 64
RK_WIDTH = RK_HEADS * RK_HEAD
RK_LORA = 64
RK_SHIFT = 3 * RK_WIDTH + 3 * RK_LORA
RK_SHIFT_PAD = 1792
RK_GN_EPS = 64e-5
RK_CHUNK = 64

HY_ORDER = 2
HY_SHORT = 3
HY_EMB = 33
HY_FW = 64

LANE = 128
SUBLANE = 8
DFT_Q = 128
VMEM_LIMIT = 56 * 1024 * 1024

ROW_TILE = 256
SCAN_TILE = 512
HY_CT = 128

HI = lax.Precision.HIGHEST


def _cparams(sem):
    return pltpu.CompilerParams(dimension_semantics=sem, vmem_limit_bytes=VMEM_LIMIT)


_DIMS = {
    "nn": (((1,), (0,)), ((), ())),
    "nt": (((1,), (1,)), ((), ())),
    "tn": (((0,), (0,)), ((), ())),
}


def _dot16(a, b, dims="nn"):
    return lax.dot_general(a.astype(BF16), b.astype(BF16), _DIMS[dims], preferred_element_type=F32)


def _dot32(a, b, dims="nn"):
    return lax.dot_general(a.astype(F32), b.astype(F32), _DIMS[dims], precision=HI,
                           preferred_element_type=F32)


def _split2(x):
    hi = x.astype(BF16)
    lo = (x - hi.astype(F32)).astype(BF16)
    return hi, lo


def _dot_exact_rhs(x, m16):
    hi, lo = _split2(x)
    return (jnp.dot(hi, m16, preferred_element_type=F32) + jnp.dot(lo, m16, preferred_element_type=F32))


def _sigmoid(x):
    return 1.0 / (1.0 + jnp.exp(-x))


def _silu(x):
    return x * _sigmoid(x)


def _softplus(x):
    return jnp.maximum(x, 0.0) + jnp.log1p(jnp.exp(-jnp.abs(x)))


def _shifted(prev8, cur, next8, d):
    t = cur.shape[0]
    ext = jnp.concatenate([prev8, cur, next8], axis=0)
    return ext[SUBLANE + d:SUBLANE + d + t]


def _halo_specs(t_rows, width, col, l_total):
    nb = t_rows // SUBLANE
    last = l_total // SUBLANE - 1
    prev = pl.BlockSpec((1, SUBLANE, width), lambda b, t: (b, jnp.maximum(t * nb - 1, 0), col))
    nxt = pl.BlockSpec((1, SUBLANE, width), lambda b, t: (b, jnp.minimum((t + 1) * nb, last), col))
    return prev, nxt


def _full(shape):
    nd = len(shape)
    return pl.BlockSpec(shape, lambda *_: (0,) * nd)


def _proj_kernel(a_ref, w_ref, *o_refs, offs):
    a = a_ref[...].astype(BF16)
    for o_ref, (lo, hi) in zip(o_refs, offs):
        o_ref[...] = jnp.dot(a, w_ref[:, lo:hi], preferred_element_type=F32)


def _project(a, w16, widths):
    m, k = a.shape
    offs, o = [], 0
    for w in widths:
        offs.append((o, o + w))
        o += w
    n = o
    tm = ROW_TILE
    return pl.pallas_call(
        functools.partial(_proj_kernel, offs=tuple(offs)),
        grid=(m // tm,),
        in_specs=[pl.BlockSpec((tm, k), lambda i: (i, 0)), _full((k, n))],
        out_specs=[pl.BlockSpec((tm, w), lambda i: (i, 0)) for w in widths],
        out_shape=[jax.ShapeDtypeStruct((m, w), F32) for w in widths],
        compiler_params=_cparams(("parallel",)),
    )(a, w16)


def _post_kernel(h_ref, mix_ref, p_ref, wo_ref, pw_ref, pg_ref, lng_ref, lnb_ref, pn_ref, o_ref, *, alpha):
    t = alpha * h_ref[...] + jnp.dot(mix_ref[...].astype(BF16), wo_ref[...], preferred_element_type=F32)
    mu = jnp.mean(t, axis=-1, keepdims=True)
    tc = t - mu
    var = jnp.mean(tc * tc, axis=-1, keepdims=True)
    y = tc * lax.rsqrt(var + LN_EPS) * lng_ref[...] + lnb_ref[...]
    e = jnp.dot(p_ref[...].astype(BF16), pw_ref[...], preferred_element_type=F32)
    e = e * lax.rsqrt(jnp.mean(e * e, axis=-1, keepdims=True) + RMS_EPS) * pn_ref[...]
    gate = _sigmoid(jnp.dot(y.astype(BF16), pg_ref[...], preferred_element_type=F32))
    o_ref[...] = y + gate * e


def _post_layer(h, mix, p, w_out, ple_w, ple_gate, ln_g, ln_b, ple_norm, alpha):
    m, d = h.shape
    pd = p.shape[1]
    tm = ROW_TILE
    row = lambda w: pl.BlockSpec((tm, w), lambda i: (i, 0))
    return pl.pallas_call(
        functools.partial(_post_kernel, alpha=alpha),
        grid=(m // tm,),
        in_specs=[row(d), row(mix.shape[1]), row(pd), _full(w_out.shape), _full(ple_w.shape),
                  _full(ple_gate.shape), _full((1, d)), _full((1, d)), _full((1, d))],
        out_specs=row(d),
        out_shape=jax.ShapeDtypeStruct((m, d), F32),
        compiler_params=_cparams(("parallel",)),
    )(h, mix, p, w_out.astype(BF16), ple_w.astype(BF16), ple_gate.astype(BF16),
      ln_g.reshape(1, d), ln_b.reshape(1, d), ple_norm.reshape(1, d))


def _dn_prep_kernel(x_ref, xp_ref, xn_ref, ab_ref, cw_ref, ga_ref, gbias_ref, q_ref, k_ref, v_ref, gb_ref):
    t_idx = pl.program_id(1)
    n_t = pl.num_programs(1)
    cur = x_ref[0]
    prev8 = jnp.where(t_idx > 0, xp_ref[0], 0.0)
    next8 = jnp.where(t_idx < n_t - 1, xn_ref[0], 0.0)
    pad = DN_CONV // 2
    acc = cur * cw_ref[pad:pad + 1, :]
    for j in range(DN_CONV):
        if j != pad:
            acc = acc + _shifted(prev8, cur, next8, j - pad) * cw_ref[j:j + 1, :]
    y = _silu(acc)
    nqk = DN_HEADS * DN_DK
    for h in range(DN_HEADS):
        qh = y[:, h * DN_DK:(h + 1) * DN_DK]
        kh = y[:, nqk + h * DN_DK:nqk + (h + 1) * DN_DK]
        qn = lax.rsqrt(jnp.sum(qh * qh, axis=-1, keepdims=True) + RMS_EPS) * (DN_DK ** -0.5)
        kn = lax.rsqrt(jnp.sum(kh * kh, axis=-1, keepdims=True) + RMS_EPS)
        q_ref[0, :, h * DN_DK:(h + 1) * DN_DK] = qh * qn
        k_ref[0, :, h * DN_DK:(h + 1) * DN_DK] = kh * kn
    v_ref[0] = y[:, 2 * nqk:]
    ab = ab_ref[0]
    lane = lax.broadcasted_iota(jnp.int32, ab.shape, 1)
    g = ga_ref[...] * _softplus(ab + gbias_ref[...])
    gb_ref[0] = jnp.where(lane < 2 * DN_HEADS, g, _sigmoid(ab))


def _dn_prep(qkv, ab, conv_w, a_log, dt_bias):
    b, l, _ = qkv.shape
    t = SCAN_TILE
    ga = jnp.zeros((1, LANE), F32).at[0, :2 * DN_HEADS].set(-jnp.exp(a_log.astype(F32)).reshape(-1))
    gbias = jnp.zeros((1, LANE), F32).at[0, :2 * DN_HEADS].set(dt_bias.astype(F32).reshape(-1))
    prev, nxt = _halo_specs(t, DN_QKV, 0, l)
    blk = lambda w: pl.BlockSpec((1, t, w), lambda bi, ti: (bi, ti, 0))
    return pl.pallas_call(
        _dn_prep_kernel,
        grid=(b, l // t),
        in_specs=[blk(DN_QKV), prev, nxt, blk(LANE), _full((DN_CONV, DN_QKV)), _full((1, LANE)), _full((1, LANE))],
        out_specs=[blk(DN_WIDTH), blk(DN_WIDTH), blk(DN_WIDTH), blk(LANE)],
        out_shape=[jax.ShapeDtypeStruct((b, l, DN_WIDTH), F32)] * 3 + [jax.ShapeDtypeStruct((b, l, LANE), F32)],
        compiler_params=_cparams(("parallel", "parallel")),
    )(qkv, qkv, qkv, ab, conv_w.astype(F32), ga, gbias)


def _tri_masks(c, reverse):
    r = lax.broadcasted_iota(jnp.int32, (c, c), 0)
    s = lax.broadcasted_iota(jnp.int32, (c, c), 1)
    if reverse:
        return s >= r, s > r, r >= s
    return s <= r, s < r, r <= s


def _nilpotent_inverse(x, eye):
    c = x.shape[0]
    r = eye + x
    p = x
    for _ in range(int(math.log2(c)) - 1):
        p = _dot32(p, p)
        r = r + _dot32(r, p)
    return r


def _dn_scan_kernel(q_ref, k_ref, v_ref, gb_ref, o_ref, s_ref, *, reverse):
    c = DN_CHUNK
    n_sub = q_ref.shape[1] // c

    @pl.when(pl.program_id(1) == 0)
    def _():
        s_ref[...] = jnp.zeros_like(s_ref)

    incl, strict, incl_t = _tri_masks(c, reverse)
    eye_b = lax.broadcasted_iota(jnp.int32, (c, c), 0) == lax.broadcasted_iota(jnp.int32, (c, c), 1)
    eye = eye_b.astype(F32)
    neg = jnp.float32(-1e30)

    def chunk(j, carry):
        jj = (n_sub - 1 - j) if reverse else j
        r0 = pl.multiple_of(jj * c, c)
        gb = gb_ref[0, pl.ds(r0, c), :]
        for h in range(DN_HEADS):
            gi = (DN_HEADS + h) if reverse else h
            bi = (3 * DN_HEADS + h) if reverse else (2 * DN_HEADS + h)
            g_col = gb[:, gi:gi + 1]
            beta = gb[:, bi:bi + 1]
            g_row = jnp.sum(jnp.where(eye_b, g_col, 0.0), axis=0, keepdims=True)
            cum_col = jnp.sum(jnp.where(incl, g_row, 0.0), axis=1, keepdims=True)
            cum_row = jnp.sum(jnp.where(incl_t, g_col, 0.0), axis=0, keepdims=True)
            g_tot = jnp.sum(g_col, axis=0, keepdims=True)
            decay = jnp.exp(jnp.where(incl, cum_col - cum_row, neg))
            eg = jnp.exp(cum_col)
            lanes = slice(h * DN_DK, (h + 1) * DN_DK)
            q = q_ref[0, pl.ds(r0, c), lanes]
            k = k_ref[0, pl.ds(r0, c), lanes]
            v = v_ref[0, pl.ds(r0, c), lanes]
            k_beta = k * beta
            kk = _dot16(k_beta, k, "nt") * decay
            t_inv = _nilpotent_inverse(jnp.where(strict, -kk, 0.0), eye)
            u = _dot32(t_inv, v * beta)
            w = _dot32(t_inv, k_beta * eg)
            qk = _dot16(q, k, "nt") * decay
            s = s_ref[h]
            v_new = u - _dot16(w, s)
            o = _dot16(q * eg, s) + _dot16(qk, v_new)
            k_tail = k * jnp.exp(g_tot - cum_col)
            s_ref[h] = s * jnp.exp(g_tot) + _dot16(k_tail, v_new, "tn")
            o_ref[0, pl.ds(r0, c), lanes] = o
        return carry

    lax.fori_loop(0, n_sub, chunk, 0)


def _dn_scan(q, k, v, gb, reverse):
    b, l, _ = q.shape
    t = SCAN_TILE
    n = l // t
    idx = (lambda bi, ti: (bi, n - 1 - ti, 0)) if reverse else (lambda bi, ti: (bi, ti, 0))
    blk = lambda w: pl.BlockSpec((1, t, w), idx)
    return pl.pallas_call(
        functools.partial(_dn_scan_kernel, reverse=reverse),
        grid=(b, n),
        in_specs=[blk(DN_WIDTH), blk(DN_WIDTH), blk(DN_WIDTH), blk(LANE)],
        out_specs=blk(DN_WIDTH),
        out_shape=jax.ShapeDtypeStruct((b, l, DN_WIDTH), F32),
        scratch_shapes=[pltpu.VMEM((DN_HEADS, DN_DK, DN_DV), F32)],
        compiler_params=_cparams(("parallel", "arbitrary")),
    )(q, k, v, gb)


def _rk_prep_kernel(x_ref, xp_ref, xn_ref, mu_ref, w2_ref, w0_ref, a2_ref, a0_ref, kk_w_ref, ka_ref, seg_ref,
                    r_ref, k_ref, v_ref, kk_ref, a_ref, lw_ref):
    t_idx = pl.program_id(1)
    n_t = pl.num_programs(1)
    cur = x_ref[0]
    prev8 = jnp.where(t_idx > 0, xp_ref[0], 0.0)
    next8 = jnp.where(t_idx < n_t - 1, xn_ref[0], 0.0)
    prev = _shifted(prev8, cur, next8, -1)
    nxt = _shifted(prev8, cur, next8, 1)
    s = cur + mu_ref[0:1, :] * (prev - cur) + mu_ref[1:2, :] * (nxt - cur)
    w = RK_WIDTH
    r = s[:, 0:w]
    k = s[:, w:2 * w]
    v = s[:, 2 * w:3 * w]
    wd = s[:, 3 * w:3 * w + 2 * RK_LORA]
    ad = s[:, 3 * w + 2 * RK_LORA:3 * w + 4 * RK_LORA]
    lora_w = _dot16(jnp.tanh(wd), w2_ref[...])
    w_log = -_softplus(-(w0_ref[...] + lora_w)) - 0.5
    lw_ref[0] = -jnp.exp(w_log)
    a = _sigmoid(a0_ref[...] + _dot16(ad, a2_ref[...]))
    kk_raw = k * kk_w_ref[...]
    ssq = _dot_exact_rhs(kk_raw * kk_raw, seg_ref[...])
    kk_ref[0] = kk_raw * lax.rsqrt(ssq + RMS_EPS)
    r_ref[0] = r
    k_ref[0] = k * (1.0 + (a - 1.0) * ka_ref[...])
    v_ref[0] = v
    a_ref[0] = a


def _seg_ones(width, group):
    i = np.arange(width) // group
    return (i[:, None] == i[None, :]).astype(np.float32)


def _rk_prep(rk, mu, w0, w2, a0, a2, k_k, k_a):
    b, l, wp = rk.shape
    t = ROW_TILE
    w = RK_WIDTH
    mu_p = jnp.zeros((2, wp), F32).at[:, :RK_SHIFT].set(mu.astype(F32))
    w2cat = jnp.zeros((2 * RK_LORA, 2 * w), F32)
    w2cat = w2cat.at[:RK_LORA, :w].set(w2[0]).at[RK_LORA:, w:].set(w2[1]).astype(BF16)
    w0cat = w0.astype(F32).reshape(1, 2 * w)
    a2p = jnp.zeros((2 * RK_LORA, w), F32).at[:RK_LORA].set(a2).astype(BF16)
    prev, nxt = _halo_specs(t, wp, 0, l)
    blk = lambda width: pl.BlockSpec((1, t, width), lambda bi, ti: (bi, ti, 0))
    return pl.pallas_call(
        _rk_prep_kernel,
        grid=(b, l // t),
        in_specs=[blk(wp), prev, nxt, _full((2, wp)), _full((2 * RK_LORA, 2 * w)), _full((1, 2 * w)),
                  _full((2 * RK_LORA, w)), _full((1, w)), _full((1, w)), _full((1, w)), _full((w, w))],
        out_specs=[blk(w)] * 5 + [blk(2 * w)],
        out_shape=[jax.ShapeDtypeStruct((b, l, w), F32)] * 5 + [jax.ShapeDtypeStruct((b, l, 2 * w), F32)],
        compiler_params=_cparams(("parallel", "parallel")),
    )(rk, rk, rk, mu_p, w2cat, w0cat, a2p, a0.astype(F32).reshape(1, w), k_k.astype(F32).reshape(1, w),
      k_a.astype(F32).reshape(1, w), jnp.asarray(_seg_ones(w, RK_HEAD), dtype=BF16))


def _rk_scan_kernel(r_ref, k_ref, v_ref, kk_ref, a_ref, lw_ref, y_ref, s_ref, *, reverse):
    c = RK_CHUNK
    n_sub = r_ref.shape[1] // c
    hd = RK_HEAD

    @pl.when(pl.program_id(1) == 0)
    def _():
        s_ref[...] = jnp.zeros_like(s_ref)

    incl, strict, _ = _tri_masks(c, reverse)
    eye = (lax.broadcasted_iota(jnp.int32, (c, c), 0) == lax.broadcasted_iota(jnp.int32, (c, c), 1)).astype(F32)
    tri16 = incl.astype(BF16)

    def chunk(j, carry):
        jj = (n_sub - 1 - j) if reverse else j
        r0 = pl.multiple_of(jj * c, c)
        rows = pl.ds(r0, c)
        lw = lw_ref[0, rows, :]
        l1 = lw.astype(BF16)
        rem = lw - l1.astype(F32)
        l2 = rem.astype(BF16)
        l3 = (rem - l2.astype(F32)).astype(BF16)
        cum = (jnp.dot(tri16, l1, preferred_element_type=F32) + jnp.dot(tri16, l2, preferred_element_type=F32)
               + jnp.dot(tri16, l3, preferred_element_type=F32))
        tot = jnp.sum(lw, axis=0, keepdims=True)
        e_pos = jnp.exp(cum)
        e_neg = jnp.exp(-cum)
        e_prev = jnp.exp(cum - lw)
        e_tail = jnp.exp(tot - cum)
        e_tot = jnp.exp(tot)
        r = r_ref[0, rows, :]
        k = k_ref[0, rows, :]
        v = v_ref[0, rows, :]
        kk = kk_ref[0, rows, :]
        b_vec = kk * a_ref[0, rows, :]
        ra = r * e_pos
        aa = -kk * e_prev
        bb = b_vec * e_neg
        kb = k * e_neg
        bt = b_vec * e_tail
        kt = k * e_tail
        for h in range(RK_HEADS):
            lanes = slice(h * hd, (h + 1) * hd)
            s = s_ref[h]
            aa_h, ra_h, bb_h, kb_h, v_h = aa[:, lanes], ra[:, lanes], bb[:, lanes], kb[:, lanes], v[:, lanes]
            a_ab = jnp.where(strict, _dot32(aa_h, bb_h, "nt"), 0.0)
            a_ak = jnp.where(strict, _dot32(aa_h, kb_h, "nt"), 0.0)
            m_rb = jnp.where(incl, _dot32(ra_h, bb_h, "nt"), 0.0)
            m_rk = jnp.where(incl, _dot32(ra_h, kb_h, "nt"), 0.0)
            t_inv = _nilpotent_inverse(a_ab, eye)
            u = _dot32(t_inv, _dot32(aa_h, s, "nt") + _dot32(a_ak, v_h))
            y = _dot32(ra_h, s, "nt") + _dot32(m_rb, u) + _dot32(m_rk, v_h)
            s_ref[h] = (s * e_tot[:, lanes] + _dot32(u, bt[:, lanes], "tn") + _dot32(v_h, kt[:, lanes], "tn"))
            y_ref[0, rows, lanes] = y
        return carry

    lax.fori_loop(0, n_sub, chunk, 0)


def _rk_scan(r, k, v, kk, a, lw, reverse):
    b, l, w = r.shape
    t = SCAN_TILE
    n = l // t
    col = 1 if reverse else 0
    idx = (lambda bi, ti: (bi, n - 1 - ti, 0)) if reverse else (lambda bi, ti: (bi, ti, 0))
    idx_lw = (lambda bi, ti: (bi, n - 1 - ti, col)) if reverse else (lambda bi, ti: (bi, ti, col))
    blk = pl.BlockSpec((1, t, w), idx)
    return pl.pallas_call(
        functools.partial(_rk_scan_kernel, reverse=reverse),
        grid=(b, n),
        in_specs=[blk] * 5 + [pl.BlockSpec((1, t, w), idx_lw)],
        out_specs=blk,
        out_shape=jax.ShapeDtypeStruct((b, l, w), F32),
        scratch_shapes=[pltpu.VMEM((RK_HEADS, RK_HEAD, RK_HEAD), F32)],
        compiler_params=_cparams(("parallel", "arbitrary")),
    )(r, k, v, kk, a, lw)


def _even_mix_kernel(of_ref, ob_ref, dg_ref, dnw_ref, yf_ref, yb_ref, r_ref, k_ref, v_ref, rg_ref,
                     rk_ref, lnw_ref, lnb_ref, segm_ref, seg1_ref, o_ref):
    o = of_ref[0] + ob_ref[0]
    gate = _silu(dg_ref[0])
    for h in range(DN_HEADS):
        lanes = slice(h * DN_DV, (h + 1) * DN_DV)
        oh = o[:, lanes]
        ms = jnp.mean(oh * oh, axis=-1, keepdims=True)
        o_ref[0, :, lanes] = oh * lax.rsqrt(ms + RMS_EPS) * dnw_ref[...] * gate[:, lanes]
    wkv = yf_ref[0] + yb_ref[0]
    mean = _dot_exact_rhs(wkv, segm_ref[...])
    cen = wkv - mean
    var = _dot_exact_rhs(cen * cen, segm_ref[...])
    wkv = cen * lax.rsqrt(var + RK_GN_EPS) * lnw_ref[...] + lnb_ref[...]
    v = v_ref[0]
    bonus = _dot_exact_rhs(r_ref[0] * k_ref[0] * rk_ref[...], seg1_ref[...]) * v
    o_ref[0, :, DN_WIDTH:] = (wkv + bonus) * _silu(rg_ref[0])


def _even_mix(o_f, o_b, dn_gate, dn_norm, y_f, y_b, r, k, v, rk_gate, r_k, ln_w, ln_b):
    b, l, _ = o_f.shape
    t = ROW_TILE
    w = RK_WIDTH
    blk = lambda width: pl.BlockSpec((1, t, width), lambda bi, ti: (bi, ti, 0))
    seg1 = jnp.asarray(_seg_ones(w, RK_HEAD), dtype=BF16)
    segm = jnp.asarray(_seg_ones(w, RK_HEAD) / RK_HEAD, dtype=BF16)
    return pl.pallas_call(
        _even_mix_kernel,
        grid=(b, l // t),
        in_specs=[blk(DN_WIDTH), blk(DN_WIDTH), blk(DN_WIDTH), _full((1, DN_DV)),
                  blk(w), blk(w), blk(w), blk(w), blk(w), blk(w),
                  _full((1, w)), _full((1, w)), _full((1, w)), _full((w, w)), _full((w, w))],
        out_specs=blk(DN_WIDTH + w),
        out_shape=jax.ShapeDtypeStruct((b, l, DN_WIDTH + w), F32),
        compiler_params=_cparams(("parallel", "parallel")),
    )(o_f, o_b, dn_gate, dn_norm.astype(F32).reshape(1, DN_DV), y_f, y_b, r, k, v, rk_gate,
      r_k.astype(F32).reshape(1, w), ln_w.astype(F32).reshape(1, w), ln_b.astype(F32).reshape(1, w), segm, seg1)


def _even_layer(h, w_in, dn_conv, dn_a_log, dn_dt_bias, dn_norm, rk_mu, rk_w0, rk_w2, rk_a0, rk_a2,
                rk_k_k, rk_k_a, rk_r_k, rk_ln_w, rk_ln_b):
    b, l, d = h.shape
    s0 = DN_QKV
    s1 = s0 + DN_AB
    s2 = s1 + DN_WIDTH
    s3 = s2 + RK_SHIFT
    pad = lambda m, width: jnp.pad(m, ((0, 0), (0, width - m.shape[1])))
    widths = (DN_QKV, LANE, DN_WIDTH, RK_SHIFT_PAD, RK_WIDTH)
    w16 = jnp.concatenate([w_in[:, :s0], pad(w_in[:, s0:s1], LANE), w_in[:, s1:s2],
                           pad(w_in[:, s2:s3], RK_SHIFT_PAD), w_in[:, s3:]], axis=1).astype(BF16)
    qkv, ab, dn_gate, rk, rk_gate = _project(h.reshape(b * l, d), w16, widths)
    r3 = lambda m: m.reshape(b, l, m.shape[-1])
    q, k, v, gb = _dn_prep(r3(qkv), r3(ab), dn_conv, dn_a_log, dn_dt_bias)
    o_f = _dn_scan(q, k, v, gb, False)
    o_b = _dn_scan(q, k, v, gb, True)
    r, kr, vr, kk, a, lw = _rk_prep(r3(rk), rk_mu, rk_w0, rk_w2, rk_a0, rk_a2, rk_k_k, rk_k_a)
    y_f = _rk_scan(r, kr, vr, kk, a, lw, False)
    y_b = _rk_scan(r, kr, vr, kk, a, lw, True)
    return _even_mix(o_f, o_b, r3(dn_gate), dn_norm, y_f, y_b, r, kr, vr, r3(rk_gate),
                     rk_r_k, rk_ln_w, rk_ln_b)


def _dft_geometry(l):
    nf = 2 * l
    p = nf // DFT_Q
    n1 = p // 2
    k1 = p // 2 + 1
    k1p = -(-k1 // SUBLANE) * SUBLANE
    return nf, p, n1, k1, k1p


@functools.lru_cache(maxsize=None)
def _dft_tables(l):
    nf, p, n1c, k1c, k1p = _dft_geometry(l)
    q = DFT_Q
    n2 = np.arange(q)[:, None, None]
    k1 = np.arange(k1c)[None, :, None]
    n1 = np.arange(n1c)[None, None, :]
    ph = -2.0 * np.pi * (((n1 * k1) % p) / p + ((n2 * k1) % nf) / nf)
    fa = np.zeros((q, 2 * k1p, n1c))
    fa[:, :k1c] = np.cos(ph)
    fa[:, k1p:k1p + k1c] = np.sin(ph)
    wgt = np.full((k1c,), 2.0)
    wgt[0] = 1.0
    wgt[-1] = 1.0
    th = -ph.transpose(0, 2, 1)
    gd = np.zeros((q, n1c, 2 * k1p))
    gd[:, :, :k1c] = np.cos(th) * wgt / nf
    gd[:, :, k1p:k1p + k1c] = -np.sin(th) * wgt / nf
    a = np.arange(q)
    ang = -2.0 * np.pi * ((a[:, None] * a[None, :]) % q) / q
    cr, ci = np.cos(ang), np.sin(ang)
    fb = np.block([[cr, -ci], [ci, cr]])
    fc = np.block([[cr, ci], [-ci, cr]])

    def hl(m):
        m32 = m.astype(np.float32)
        hi = m32.astype(BF16)
        lo = (m32 - hi.astype(np.float32)).astype(BF16)
        return hi, lo

    return tuple(hl(m) for m in (fa, fb, fc, gd))


def _fdot(fs, x):
    x_hi = x.astype(BF16)
    out = jnp.dot(fs[0], x_hi, preferred_element_type=F32)
    if len(fs) == 2:
        x_lo = (x - x_hi.astype(F32)).astype(BF16)
        out = out + jnp.dot(fs[1], x_hi, preferred_element_type=F32) + jnp.dot(fs[0], x_lo, preferred_element_type=F32)
    return out


def _stage_a(u_ref, y_re, y_im, fa, geo):
    nf, p, n1c, k1c, k1p = geo

    def body(n2, carry):
        slab = u_ref[pl.ds(n2, n1c, stride=DFT_Q), :]
        out = _fdot([f[n2] for f in fa], slab)
        y_re[pl.ds(n2, k1p, stride=DFT_Q), :] = out[:k1p]
        y_im[pl.ds(n2, k1p, stride=DFT_Q), :] = out[k1p:]
        return carry

    lax.fori_loop(0, DFT_Q, body, 0)


def _stage_b(y_re, y_im, k1, fb):
    rows = pl.ds(pl.multiple_of(k1 * DFT_Q, DFT_Q), DFT_Q)
    w = jnp.concatenate([y_re[rows, :], y_im[rows, :]], axis=0)
    z = _fdot([f[...] for f in fb], w)
    return rows, z[:DFT_Q], z[DFT_Q:]


def _hy_conv_kernel(u_ref, m_ref, skip_ref, hr_ref, hi_ref, fa, fb, fc, gd, o_ref, y_re, y_im, *, geo):
    nf, p, n1c, k1c, k1p = geo
    u2 = u_ref.at[0]
    o2 = o_ref.at[0]
    _stage_a(u2, y_re, y_im, (fa,), geo)

    def mid(k1, carry):
        rows, zr, zi = _stage_b(y_re, y_im, k1, (fb,))
        hr = hr_ref[rows, :]
        hi = hi_ref[rows, :]
        pr = zr * hr - zi * hi
        pi = zr * hi + zi * hr
        a = _fdot((fc[...],), jnp.concatenate([pr, pi], axis=0))
        y_re[rows, :] = a[:DFT_Q]
        y_im[rows, :] = a[DFT_Q:]
        return carry

    lax.fori_loop(0, k1c, mid, 0)

    def last(n2, carry):
        a = jnp.concatenate([y_re[pl.ds(n2, k1p, stride=DFT_Q), :], y_im[pl.ds(n2, k1p, stride=DFT_Q), :]], axis=0)
        o2[pl.ds(n2, n1c, stride=DFT_Q), :] = _fdot((gd[n2],), a)
        return carry

    lax.fori_loop(0, DFT_Q, last, 0)

    blk = 512
    skip = skip_ref[...]

    def epi(i, carry):
        rows = pl.ds(pl.multiple_of(i * blk, blk), blk)
        uu = u2[rows, :]
        o2[rows, :] = m_ref[0, rows, :] * (o2[rows, :] + skip * uu)
        return carry

    lax.fori_loop(0, u_ref.shape[1] // blk, epi, 0)


def _single(shape, index_map):
    return pl.BlockSpec(shape, index_map, pipeline_mode=pl.Buffered(1))


def _hy_conv(u, mult, skip, h_re, h_im, tables):
    b, l, ch = u.shape
    geo = _dft_geometry(l)
    nf, p, n1c, k1c, k1p = geo
    ct = HY_CT
    consts = [jnp.asarray(pair[0]) for pair in tables]
    seq = pl.BlockSpec((1, l, ct), lambda ci, bi: (bi, 0, ci))
    spec = _single((k1p * DFT_Q, ct), lambda ci, bi: (0, ci))
    cspecs = [_single(c.shape, (lambda ci, bi, nd=c.ndim: (0,) * nd)) for c in consts]
    return pl.pallas_call(
        functools.partial(_hy_conv_kernel, geo=geo),
        grid=(ch // ct, b),
        in_specs=[seq, seq, pl.BlockSpec((1, ct), lambda ci, bi: (0, ci)), spec, spec] + cspecs,
        out_specs=seq,
        out_shape=jax.ShapeDtypeStruct((b, l, ch), F32),
        scratch_shapes=[pltpu.VMEM((k1p * DFT_Q, ct), F32), pltpu.VMEM((k1p * DFT_Q, ct), F32)],
        compiler_params=_cparams(("parallel", "parallel")),
    )(u, mult, skip, h_re, h_im, *consts)


def _hy_mlp_kernel(f_ref, w1_ref, b1_ref, w2_ref, b2_ref, w3_ref, b3_ref, fr_ref, o_ref):
    fr = fr_ref[...]
    hdn = jnp.sin(fr * (_dot32(f_ref[...], w1_ref[...]) + b1_ref[...]))
    hdn = jnp.sin(fr * (_dot32(hdn, w2_ref[...]) + b2_ref[...]))
    o_ref[...] = jnp.sin(fr * (_dot32(hdn, w3_ref[...]) + b3_ref[...]))


def _hy_mlp(feats, w1, b1, w2, b2, w3, b3, freq):
    l = feats.shape[0]
    t = min(l, 1024)
    fw = HY_FW
    row = lambda a: a.astype(F32).reshape(1, fw)
    w1p = jnp.zeros((LANE, fw), F32).at[:HY_EMB].set(w1.astype(F32))
    return pl.pallas_call(
        _hy_mlp_kernel,
        grid=(l // t,),
        in_specs=[pl.BlockSpec((t, LANE), lambda i: (i, 0)), _full((LANE, fw)), _full((1, fw)), _full((fw, fw)),
                  _full((1, fw)), _full((fw, fw)), _full((1, fw)), _full((1, fw))],
        out_specs=pl.BlockSpec((t, fw), lambda i: (i, 0)),
        out_shape=jax.ShapeDtypeStruct((l, fw), F32),
        compiler_params=_cparams(("parallel",)),
    )(feats, w1p, row(b1), w2.astype(F32), row(b2), w3.astype(F32), row(b3), row(freq))


def _hy_filter_kernel(hdn_ref, wf_ref, wb_ref, df_ref, db_ref, fa_hi, fa_lo, fb_hi, fb_lo,
                      hr_ref, hi_ref, filt, yr_f, yi_f, yr_b, yi_b, *, geo):
    nf, p, n1c, k1c, k1p = geo
    fa = (fa_hi, fa_lo)
    fb = (fb_hi, fb_lo)
    l = hdn_ref.shape[0]
    blk = min(l, 512)
    nblk = l // blk

    def build(w_ref, d_ref, drop_first):
        def body(i, acc):
            rows = pl.ds(pl.multiple_of(i * blk, blk), blk)
            hx = hdn_ref[rows, :]
            tt = hx[:, HY_FW:HY_FW + 1]
            hv = _dot32(hx, w_ref[0]) * jnp.exp(-tt * jnp.abs(d_ref[0]))
            if drop_first:
                pos = lax.broadcasted_iota(jnp.int32, hv.shape, 0) + i * blk
                hv = jnp.where(pos == 0, 0.0, hv)
            filt[rows, :] = hv
            return acc + jnp.sum(jnp.abs(hv), axis=0, keepdims=True)

        return lax.fori_loop(0, nblk, body, jnp.zeros((1, filt.shape[1]), F32))

    l1 = build(wf_ref, df_ref, False)
    _stage_a(filt, yr_f, yi_f, fa, geo)
    l1 = l1 + build(wb_ref, db_ref, True)
    _stage_a(filt, yr_b, yi_b, fa, geo)
    inv = 1.0 / (l1 + RMS_EPS)

    def mid(k1, carry):
        rows, fr, fi = _stage_b(yr_f, yi_f, k1, fb)
        _, br, bi = _stage_b(yr_b, yi_b, k1, fb)
        hr_ref[0, rows, :] = (fr + br) * inv
        hi_ref[0, rows, :] = (fi - bi) * inv
        return carry

    lax.fori_loop(0, k1c, mid, 0)
    pad0 = k1c * DFT_Q
    if k1p > k1c:
        zeros = jnp.zeros(((k1p - k1c) * DFT_Q, hr_ref.shape[2]), F32)
        hr_ref[0, pad0:, :] = zeros
        hi_ref[0, pad0:, :] = zeros


def _hy_filters(hdn, tcol, w_out, deltas, tables, l):
    geo = _dft_geometry(l)
    nf, p, n1c, k1c, k1p = geo
    ch = w_out.shape[1] // (2 * HY_ORDER)
    ct = HY_CT
    nct = ch // ct
    hdn_x = jnp.concatenate([hdn, tcol, jnp.zeros((l, LANE - HY_FW - 1), F32)], axis=1)
    w4 = w_out.astype(F32).reshape(HY_FW, 2 * HY_ORDER, ch).transpose(1, 0, 2)
    w4 = jnp.pad(w4, ((0, 0), (0, LANE - HY_FW), (0, 0)))
    d4 = deltas.astype(F32).reshape(2 * HY_ORDER, 1, ch)
    consts = [jnp.asarray(t) for pair in tables[:2] for t in pair]
    cspecs = [_single(c.shape, (lambda o, ci, nd=c.ndim: (0,) * nd)) for c in consts]
    wspec = lambda d: pl.BlockSpec((1, LANE, ct), lambda o, ci: (2 * o + d, 0, ci))
    dspec = lambda d: pl.BlockSpec((1, 1, ct), lambda o, ci: (2 * o + d, 0, ci))
    ospec = pl.BlockSpec((1, k1p * DFT_Q, ct), lambda o, ci: (o, 0, ci))
    ysc = pltpu.VMEM((k1p * DFT_Q, ct), F32)
    return pl.pallas_call(
        functools.partial(_hy_filter_kernel, geo=geo),
        grid=(HY_ORDER, nct),
        in_specs=[_single((l, LANE), lambda o, ci: (0, 0)), wspec(0), wspec(1), dspec(0), dspec(1)] + cspecs,
        out_specs=[ospec, ospec],
        out_shape=[jax.ShapeDtypeStruct((HY_ORDER, k1p * DFT_Q, ch), F32)] * 2,
        scratch_shapes=[pltpu.VMEM((l, ct), F32), ysc, ysc, ysc, ysc],
        compiler_params=_cparams(("parallel", "parallel")),
    )(hdn_x, w4, w4, d4, d4, *consts)


def _position_features(l):
    bands = (HY_EMB - 1) // 2
    t = jnp.linspace(0.0, 1.0, l, dtype=F32)[:, None]
    f = jnp.linspace(1e-4, bands - 1, bands, dtype=F32)[None, :]
    ang = (2.0 * math.pi / l) * jnp.arange(l, dtype=F32)[:, None] * f
    feats = jnp.concatenate([t, jnp.cos(ang), -jnp.sin(ang)], axis=-1)
    return jnp.pad(feats, ((0, 0), (0, LANE - HY_EMB))), t


def _hy_prep_kernel(x_ref, xp_ref, xn_ref, g_ref, cw_ref, cb_ref, x1_ref, m2_ref, v_ref):
    t_idx = pl.program_id(1)
    n_t = pl.num_programs(1)
    cur = x_ref[0]
    prev8 = jnp.where(t_idx > 0, xp_ref[0], 0.0)
    next8 = jnp.where(t_idx < n_t - 1, xn_ref[0], 0.0)
    y = (_shifted(prev8, cur, next8, -1) * cw_ref[0:1, :] + cur * cw_ref[1:2, :]
         + _shifted(prev8, cur, next8, 1) * cw_ref[2:3, :] + cb_ref[...])
    c = g_ref.shape[2]
    x1_ref[0] = y[:, :c]
    m2_ref[0] = y[:, c:2 * c] * _silu(g_ref[0])
    v_ref[0] = y[:, 2 * c:]


def _hy_prep(xv, gate, conv_w, conv_b):
    b, l, w3 = xv.shape
    c = gate.shape[2]
    t = ROW_TILE
    prev, nxt = _halo_specs(t, w3, 0, l)
    blk = lambda width: pl.BlockSpec((1, t, width), lambda bi, ti: (bi, ti, 0))
    return pl.pallas_call(
        _hy_prep_kernel,
        grid=(b, l // t),
        in_specs=[blk(w3), prev, nxt, blk(c), _full((HY_SHORT, w3)), _full((1, w3))],
        out_specs=[blk(c)] * 3,
        out_shape=[jax.ShapeDtypeStruct((b, l, c), F32)] * 3,
        compiler_params=_cparams(("parallel", "parallel")),
    )(xv, xv, xv, gate, conv_w.astype(F32), conv_b.astype(F32).reshape(1, w3))


def _odd_layer(h, w_in, conv_w, conv_b, f_w1, f_b1, f_w2, f_b2, f_w3, f_b3, f_freq, f_out, deltas, skip):
    b, l, d = h.shape
    c = skip.shape[1]
    xv, gate = _project(h.reshape(b * l, d), w_in.astype(BF16), (3 * c, c))
    x1, m2, v = _hy_prep(xv.reshape(b, l, 3 * c), gate.reshape(b, l, c), conv_w, conv_b)
    tables = _dft_tables(l)
    feats, tcol = _position_features(l)
    hdn = _hy_mlp(feats, f_w1, f_b1, f_w2, f_b2, f_w3, f_b3, f_freq)
    h_re, h_im = _hy_filters(hdn, tcol, f_out, deltas, tables, l)
    skip = skip.astype(F32)
    z = _hy_conv(v, x1, skip[0:1], h_re[0], h_im[0], tables)
    return _hy_conv(z, m2, skip[1:2], h_re[1], h_im[1], tables)


def kernel(x, p, even_w_in, dn_conv, dn_a_log, dn_dt_bias, dn_norm, rk_mu, rk_w0, rk_w2, rk_a0, rk_a2, rk_k_k, rk_k_a, rk_r_k, rk_ln_w, rk_ln_b, odd_w_in, hy_conv_w, hy_conv_b, hy_ffn_w1, hy_ffn_b1, hy_ffn_w2, hy_ffn_b2, hy_ffn_w3, hy_ffn_b3, hy_ffn_freq, hy_ffn_out, hy_deltas, hy_skip, w_out, ln_g, ln_b, ple_w, ple_norm, ple_gate):
    b, l, d = x.shape
    depth = p.shape[0]
    alpha = (2.0 * depth) ** 0.25
    h = x
    for i in range(depth):
        j = i // 2
        if i % 2 == 0:
            mix = _even_layer(h, even_w_in[j], dn_conv[j], dn_a_log[j], dn_dt_bias[j], dn_norm[j], rk_mu[j],
                              rk_w0[j], rk_w2[j], rk_a0[j], rk_a2[j], rk_k_k[j].reshape(-1), rk_k_a[j].reshape(-1),
                              rk_r_k[j].reshape(-1), rk_ln_w[j], rk_ln_b[j])
        else:
            mix = _odd_layer(h, odd_w_in[j], hy_conv_w[j], hy_conv_b[j], hy_ffn_w1[j], hy_ffn_b1[j],
                             hy_ffn_w2[j], hy_ffn_b2[j], hy_ffn_w3[j], hy_ffn_b3[j], hy_ffn_freq[j],
                             hy_ffn_out[j], hy_deltas[j], hy_skip[j])
        h2 = _post_layer(h.reshape(b * l, d), mix.reshape(b * l, mix.shape[-1]), p[i].reshape(b * l, p.shape[-1]),
                         w_out[i], ple_w[i], ple_gate[i], ln_g[i], ln_b[i], ple_norm[i], alpha)
        h = h2.reshape(b, l, d)
    return h
```

```python
import functools
import math

import numpy as np
import jax
import jax.numpy as jnp
from jax import lax
from jax.experimental import pallas as pl
from jax.experimental.pallas import tpu as pltpu

F32 = jnp.float32
BF16 = jnp.bfloat16

LN_EPS = 1e-5
RMS_EPS = 1e-6

DN_HEADS = 4
DN_DK = 128
DN_DV = 128
DN_WIDTH = DN_HEADS * DN_DV
DN_QKV = 2 * DN_HEADS * DN_DK + DN_WIDTH
DN_AB = 4 * DN_HEADS
DN_CONV = 5
DN_CHUNK = 64

RK_HEADS = 8
RK_HEAD = 64
RK_WIDTH = RK_HEADS * RK_HEAD
RK_LORA = 64
RK_SHIFT = 3 * RK_WIDTH + 3 * RK_LORA
RK_SHIFT_PAD = 1792
RK_GN_EPS = 64e-5
RK_CHUNK = 64

HY_ORDER = 2
HY_SHORT = 3
HY_EMB = 33
HY_FW = 64

LANE = 128
SUBLANE = 8
DFT_Q = 128
VMEM_LIMIT = 56 * 1024 * 1024

ROW_TILE = 256
SCAN_TILE = 512
HY_CT = 128

HI = lax.Precision.HIGHEST


def _cparams(sem):
    return pltpu.CompilerParams(dimension_semantics=sem, vmem_limit_bytes=VMEM_LIMIT)


_DIMS = {
    "nn": (((1,), (0,)), ((), ())),
    "nt": (((1,), (1,)), ((), ())),
    "tn": (((0,), (0,)), ((), ())),
}


def _dot16(a, b, dims="nn"):
    return lax.dot_general(a.astype(BF16), b.astype(BF16), _DIMS[dims], preferred_element_type=F32)


def _dot32(a, b, dims="nn"):
    return lax.dot_general(a.astype(F32), b.astype(F32), _DIMS[dims], precision=HI,
                           preferred_element_type=F32)


def _split2(x):
    hi = x.astype(BF16)
    lo = (x - hi.astype(F32)).astype(BF16)
    return hi, lo


def _dot_exact_rhs(x, m16):
    hi, lo = _split2(x)
    return (jnp.dot(hi, m16, preferred_element_type=F32) + jnp.dot(lo, m16, preferred_element_type=F32))


def _sigmoid(x):
    return 1.0 / (1.0 + jnp.exp(-x))


def _silu(x):
    return x * _sigmoid(x)


def _softplus(x):
    return jnp.maximum(x, 0.0) + jnp.log1p(jnp.exp(-jnp.abs(x)))


def _shifted(prev8, cur, next8, d):
    t = cur.shape[0]
    ext = jnp.concatenate([prev8, cur, next8], axis=0)
    return ext[SUBLANE + d:SUBLANE + d + t]


def _halo_specs(t_rows, width, col, l_total):
    nb = t_rows // SUBLANE
    last = l_total // SUBLANE - 1
    prev = pl.BlockSpec((1, SUBLANE, width), lambda b, t: (b, jnp.maximum(t * nb - 1, 0), col))
    nxt = pl.BlockSpec((1, SUBLANE, width), lambda b, t: (b, jnp.minimum((t + 1) * nb, last), col))
    return prev, nxt


def _full(shape):
    nd = len(shape)
    return pl.BlockSpec(shape, lambda *_: (0,) * nd)


def _proj_kernel(a_ref, w_ref, *o_refs, offs):
    a = a_ref[...].astype(BF16)
    for o_ref, (lo, hi) in zip(o_refs, offs):
        o_ref[...] = jnp.dot(a, w_ref[:, lo:hi], preferred_element_type=F32)


def _project(a, w16, widths):
    m, k = a.shape
    offs, o = [], 0
    for w in widths:
        offs.append((o, o + w))
        o += w
    n = o
    tm = ROW_TILE
    return pl.pallas_call(
        functools.partial(_proj_kernel, offs=tuple(offs)), name="project",
        grid=(m // tm,),
        in_specs=[pl.BlockSpec((tm, k), lambda i: (i, 0)), _full((k, n))],
        out_specs=[pl.BlockSpec((tm, w), lambda i: (i, 0)) for w in widths],
        out_shape=[jax.ShapeDtypeStruct((m, w), F32) for w in widths],
        compiler_params=_cparams(("parallel",)),
    )(a, w16)


def _post_kernel(h_ref, mix_ref, p_ref, wo_ref, pw_ref, pg_ref, lng_ref, lnb_ref, pn_ref, o_ref, *, alpha):
    t = alpha * h_ref[...] + jnp.dot(mix_ref[...].astype(BF16), wo_ref[...], preferred_element_type=F32)
    mu = jnp.mean(t, axis=-1, keepdims=True)
    tc = t - mu
    var = jnp.mean(tc * tc, axis=-1, keepdims=True)
    y = tc * lax.rsqrt(var + LN_EPS) * lng_ref[...] + lnb_ref[...]
    e = jnp.dot(p_ref[...].astype(BF16), pw_ref[...], preferred_element_type=F32)
    e = e * lax.rsqrt(jnp.mean(e * e, axis=-1, keepdims=True) + RMS_EPS) * pn_ref[...]
    gate = _sigmoid(jnp.dot(y.astype(BF16), pg_ref[...], preferred_element_type=F32))
    o_ref[...] = y + gate * e


def _post_layer(h, mix, p, w_out, ple_w, ple_gate, ln_g, ln_b, ple_norm, alpha):
    m, d = h.shape
    pd = p.shape[1]
    tm = ROW_TILE
    row = lambda w: pl.BlockSpec((tm, w), lambda i: (i, 0))
    return pl.pallas_call(
        functools.partial(_post_kernel, alpha=alpha), name="post_layer",
        grid=(m // tm,),
        in_specs=[row(d), row(mix.shape[1]), row(pd), _full(w_out.shape), _full(ple_w.shape),
                  _full(ple_gate.shape), _full((1, d)), _full((1, d)), _full((1, d))],
        out_specs=row(d),
        out_shape=jax.ShapeDtypeStruct((m, d), F32),
        compiler_params=_cparams(("parallel",)),
    )(h, mix, p, w_out.astype(BF16), ple_w.astype(BF16), ple_gate.astype(BF16),
      ln_g.reshape(1, d), ln_b.reshape(1, d), ple_norm.reshape(1, d))


def _dn_prep_kernel(x_ref, xp_ref, xn_ref, ab_ref, cw_ref, ga_ref, gbias_ref, q_ref, k_ref, v_ref, gb_ref):
    t_idx = pl.program_id(1)
    n_t = pl.num_programs(1)
    cur = x_ref[0]
    prev8 = jnp.where(t_idx > 0, xp_ref[0], 0.0)
    next8 = jnp.where(t_idx < n_t - 1, xn_ref[0], 0.0)
    pad = DN_CONV // 2
    acc = cur * cw_ref[pad:pad + 1, :]
    for j in range(DN_CONV):
        if j != pad:
            acc = acc + _shifted(prev8, cur, next8, j - pad) * cw_ref[j:j + 1, :]
    y = _silu(acc)
    nqk = DN_HEADS * DN_DK
    for h in range(DN_HEADS):
        qh = y[:, h * DN_DK:(h + 1) * DN_DK]
        kh = y[:, nqk + h * DN_DK:nqk + (h + 1) * DN_DK]
        qn = lax.rsqrt(jnp.sum(qh * qh, axis=-1, keepdims=True) + RMS_EPS) * (DN_DK ** -0.5)
        kn = lax.rsqrt(jnp.sum(kh * kh, axis=-1, keepdims=True) + RMS_EPS)
        q_ref[0, :, h * DN_DK:(h + 1) * DN_DK] = qh * qn
        k_ref[0, :, h * DN_DK:(h + 1) * DN_DK] = kh * kn
    v_ref[0] = y[:, 2 * nqk:]
    ab = ab_ref[0]
    lane = lax.broadcasted_iota(jnp.int32, ab.shape, 1)
    g = ga_ref[...] * _softplus(ab + gbias_ref[...])
    gb_ref[0] = jnp.where(lane < 2 * DN_HEADS, g, _sigmoid(ab))


def _dn_prep(qkv, ab, conv_w, a_log, dt_bias):
    b, l, _ = qkv.shape
    t = SCAN_TILE
    ga = jnp.zeros((1, LANE), F32).at[0, :2 * DN_HEADS].set(-jnp.exp(a_log.astype(F32)).reshape(-1))
    gbias = jnp.zeros((1, LANE), F32).at[0, :2 * DN_HEADS].set(dt_bias.astype(F32).reshape(-1))
    prev, nxt = _halo_specs(t, DN_QKV, 0, l)
    blk = lambda w: pl.BlockSpec((1, t, w), lambda bi, ti: (bi, ti, 0))
    return pl.pallas_call(
        _dn_prep_kernel, name="dn_prep",
        grid=(b, l // t),
        in_specs=[blk(DN_QKV), prev, nxt, blk(LANE), _full((DN_CONV, DN_QKV)), _full((1, LANE)), _full((1, LANE))],
        out_specs=[blk(DN_WIDTH), blk(DN_WIDTH), blk(DN_WIDTH), blk(LANE)],
        out_shape=[jax.ShapeDtypeStruct((b, l, DN_WIDTH), F32)] * 3 + [jax.ShapeDtypeStruct((b, l, LANE), F32)],
        compiler_params=_cparams(("parallel", "parallel")),
    )(qkv, qkv, qkv, ab, conv_w.astype(F32), ga, gbias)


def _tri_masks(c, reverse):
    r = lax.broadcasted_iota(jnp.int32, (c, c), 0)
    s = lax.broadcasted_iota(jnp.int32, (c, c), 1)
    if reverse:
        return s >= r, s > r, r >= s
    return s <= r, s < r, r <= s


def _nilpotent_inverses(xs, eye):
    c = xs[0].shape[0]
    rs = [eye + x for x in xs]
    ps = list(xs)
    for _ in range(int(math.log2(c)) - 1):
        ps = [_dot16(p, p) for p in ps]
        rs = [r + _dot16(r, p) for r, p in zip(rs, ps)]
    return rs


def _dn_scan_kernel(qf_ref, kf_ref, vf_ref, gf_ref, qb_ref, kb_ref, vb_ref, gb_ref, of_ref, ob_ref, s_ref):
    c = DN_CHUNK
    n_sub = qf_ref.shape[1] // c

    @pl.when(pl.program_id(1) == 0)
    def _():
        s_ref[...] = jnp.zeros_like(s_ref)

    eye_b = lax.broadcasted_iota(jnp.int32, (c, c), 0) == lax.broadcasted_iota(jnp.int32, (c, c), 1)
    eye = eye_b.astype(F32)
    neg = jnp.float32(-1e30)
    masks = (_tri_masks(c, False), _tri_masks(c, True))
    sides = ((qf_ref, kf_ref, vf_ref, gf_ref, of_ref), (qb_ref, kb_ref, vb_ref, gb_ref, ob_ref))

    def chunk(j, carry):
        chains = []
        for d, (q_ref, k_ref, v_ref, g_ref, o_ref) in enumerate(sides):
            jj = (n_sub - 1 - j) if d else j
            rows = pl.ds(pl.multiple_of(jj * c, c), c)
            incl, strict, incl_t = masks[d]
            gb = g_ref[0, rows, :]
            for h in range(DN_HEADS):
                g_col = gb[:, d * DN_HEADS + h:d * DN_HEADS + h + 1]
                beta = gb[:, (2 + d) * DN_HEADS + h:(2 + d) * DN_HEADS + h + 1]
                g_row = jnp.sum(jnp.where(eye_b, g_col, 0.0), axis=0, keepdims=True)
                cum_col = jnp.sum(jnp.where(incl, g_row, 0.0), axis=1, keepdims=True)
                cum_row = jnp.sum(jnp.where(incl_t, g_col, 0.0), axis=0, keepdims=True)
                g_tot = jnp.sum(g_col, axis=0, keepdims=True)
                lanes = slice(h * DN_DK, (h + 1) * DN_DK)
                q = q_ref[0, rows, lanes]
                k = k_ref[0, rows, lanes]
                v = v_ref[0, rows, lanes]
                eg = jnp.exp(cum_col)
                k_beta = k * beta
                chains.append(dict(
                    d=d, h=h, rows=rows, lanes=lanes, o_ref=o_ref, strict=strict,
                    decay=jnp.exp(jnp.where(incl, cum_col - cum_row, neg)),
                    e_tot=jnp.exp(g_tot), q_dec=q * eg, k=k,
                    lhs=jnp.concatenate([k_beta, q], axis=0),
                    rhs=jnp.concatenate([v * beta, k_beta * eg], axis=1),
                    k_tail=k * jnp.exp(g_tot - cum_col)))
        gram = [_dot16(ch["lhs"], ch["k"], "nt") for ch in chains]
        kks = [g[:c] * ch["decay"] for g, ch in zip(gram, chains)]
        qks = [g[c:] * ch["decay"] for g, ch in zip(gram, chains)]
        t_inv = _nilpotent_inverses([jnp.where(ch["strict"], -kk, 0.0) for kk, ch in zip(kks, chains)], eye)
        uw = [_dot16(t, ch["rhs"]) for t, ch in zip(t_inv, chains)]
        quw = [_dot16(qk, x) for qk, x in zip(qks, uw)]
        kuw = [_dot16(ch["k_tail"], x, "tn") for ch, x in zip(chains, uw)]
        states = [s_ref[ch["d"] * DN_HEADS + ch["h"]] for ch in chains]
        outs = [_dot16(ch["q_dec"] - x[:, DN_DV:], s) + x[:, :DN_DV] for ch, x, s in zip(chains, quw, states)]
        news = [s * ch["e_tot"] - _dot16(x[:, DN_DV:], s) + x[:, :DN_DV] for ch, x, s in zip(chains, kuw, states)]
        for ch, o, s_new in zip(chains, outs, news):
            ch["o_ref"][0, ch["rows"], ch["lanes"]] = o
            s_ref[ch["d"] * DN_HEADS + ch["h"]] = s_new
        return carry

    lax.fori_loop(0, n_sub, chunk, 0)


def _dn_scan(q, k, v, gb):
    b, l, _ = q.shape
    t = SCAN_TILE
    n = l // t
    fwd = lambda w: pl.BlockSpec((1, t, w), lambda bi, ti: (bi, ti, 0))
    bwd = lambda w: pl.BlockSpec((1, t, w), lambda bi, ti: (bi, n - 1 - ti, 0))
    w = DN_WIDTH
    return pl.pallas_call(
        _dn_scan_kernel, name="dn_scan",
        grid=(b, n),
        in_specs=[fwd(w), fwd(w), fwd(w), fwd(LANE), bwd(w), bwd(w), bwd(w), bwd(LANE)],
        out_specs=[fwd(w), bwd(w)],
        out_shape=[jax.ShapeDtypeStruct((b, l, w), F32)] * 2,
        scratch_shapes=[pltpu.VMEM((2 * DN_HEADS, DN_DK, DN_DV), F32)],
        compiler_params=_cparams(("parallel", "arbitrary")),
    )(q, k, v, gb, q, k, v, gb)


def _rk_prep_kernel(x_ref, xp_ref, xn_ref, mu_ref, w2_ref, w0_ref, a2_ref, a0_ref, kk_w_ref, ka_ref, seg_ref,
                    r_ref, k_ref, v_ref, kk_ref, a_ref, lw_ref):
    t_idx = pl.program_id(1)
    n_t = pl.num_programs(1)
    cur = x_ref[0]
    prev8 = jnp.where(t_idx > 0, xp_ref[0], 0.0)
    next8 = jnp.where(t_idx < n_t - 1, xn_ref[0], 0.0)
    prev = _shifted(prev8, cur, next8, -1)
    nxt = _shifted(prev8, cur, next8, 1)
    s = cur + mu_ref[0:1, :] * (prev - cur) + mu_ref[1:2, :] * (nxt - cur)
    w = RK_WIDTH
    r = s[:, 0:w]
    k = s[:, w:2 * w]
    v = s[:, 2 * w:3 * w]
    wd = s[:, 3 * w:3 * w + 2 * RK_LORA]
    ad = s[:, 3 * w + 2 * RK_LORA:3 * w + 4 * RK_LORA]
    lora_w = _dot16(jnp.tanh(wd), w2_ref[...])
    w_log = -_softplus(-(w0_ref[...] + lora_w)) - 0.5
    lw_ref[0] = -jnp.exp(w_log)
    a = _sigmoid(a0_ref[...] + _dot16(ad, a2_ref[...]))
    kk_raw = k * kk_w_ref[...]
    ssq = _dot_exact_rhs(kk_raw * kk_raw, seg_ref[...])
    kk_ref[0] = kk_raw * lax.rsqrt(ssq + RMS_EPS)
    r_ref[0] = r
    k_ref[0] = k * (1.0 + (a - 1.0) * ka_ref[...])
    v_ref[0] = v
    a_ref[0] = a


def _seg_ones(width, group):
    i = np.arange(width) // group
    return (i[:, None] == i[None, :]).astype(np.float32)


def _rk_prep(rk, mu, w0, w2, a0, a2, k_k, k_a):
    b, l, wp = rk.shape
    t = ROW_TILE
    w = RK_WIDTH
    mu_p = jnp.zeros((2, wp), F32).at[:, :RK_SHIFT].set(mu.astype(F32))
    w2cat = jnp.zeros((2 * RK_LORA, 2 * w), F32)
    w2cat = w2cat.at[:RK_LORA, :w].set(w2[0]).at[RK_LORA:, w:].set(w2[1]).astype(BF16)
    w0cat = w0.astype(F32).reshape(1, 2 * w)
    a2p = jnp.zeros((2 * RK_LORA, w), F32).at[:RK_LORA].set(a2).astype(BF16)
    prev, nxt = _halo_specs(t, wp, 0, l)
    blk = lambda width: pl.BlockSpec((1, t, width), lambda bi, ti: (bi, ti, 0))
    return pl.pallas_call(
        _rk_prep_kernel, name="rk_prep",
        grid=(b, l // t),
        in_specs=[blk(wp), prev, nxt, _full((2, wp)), _full((2 * RK_LORA, 2 * w)), _full((1, 2 * w)),
                  _full((2 * RK_LORA, w)), _full((1, w)), _full((1, w)), _full((1, w)), _full((w, w))],
        out_specs=[blk(w)] * 5 + [blk(2 * w)],
        out_shape=[jax.ShapeDtypeStruct((b, l, w), F32)] * 5 + [jax.ShapeDtypeStruct((b, l, 2 * w), F32)],
        compiler_params=_cparams(("parallel", "parallel")),
    )(rk, rk, rk, mu_p, w2cat, w0cat, a2p, a0.astype(F32).reshape(1, w), k_k.astype(F32).reshape(1, w),
      k_a.astype(F32).reshape(1, w), jnp.asarray(_seg_ones(w, RK_HEAD), dtype=BF16))


def _rk_scan_kernel(rf_ref, kf_ref, vf_ref, kkf_ref, af_ref, lwf_ref, rb_ref, kb_ref, vb_ref, kkb_ref, ab_ref,
                    lwb_ref, yf_ref, yb_ref, s_ref):
    c = RK_CHUNK
    n_sub = rf_ref.shape[1] // c
    hd = RK_HEAD

    @pl.when(pl.program_id(1) == 0)
    def _():
        s_ref[...] = jnp.zeros_like(s_ref)

    eye = (lax.broadcasted_iota(jnp.int32, (c, c), 0) == lax.broadcasted_iota(jnp.int32, (c, c), 1)).astype(F32)
    masks = (_tri_masks(c, False), _tri_masks(c, True))
    sides = ((rf_ref, kf_ref, vf_ref, kkf_ref, af_ref, lwf_ref, yf_ref),
             (rb_ref, kb_ref, vb_ref, kkb_ref, ab_ref, lwb_ref, yb_ref))

    def chunk(j, carry):
        chains = []
        for d, (r_ref, k_ref, v_ref, kk_ref, a_ref, lw_ref, y_ref) in enumerate(sides):
            jj = (n_sub - 1 - j) if d else j
            rows = pl.ds(pl.multiple_of(jj * c, c), c)
            incl, strict, _ = masks[d]
            tri16 = incl.astype(BF16)
            lw = lw_ref[0, rows, :]
            l1 = lw.astype(BF16)
            rem = lw - l1.astype(F32)
            l2 = rem.astype(BF16)
            l3 = (rem - l2.astype(F32)).astype(BF16)
            cum = (jnp.dot(tri16, l1, preferred_element_type=F32) + jnp.dot(tri16, l2, preferred_element_type=F32)
                   + jnp.dot(tri16, l3, preferred_element_type=F32))
            tot = jnp.sum(lw, axis=0, keepdims=True)
            e_neg = jnp.exp(-cum)
            e_tail = jnp.exp(tot - cum)
            e_tot = jnp.exp(tot)
            k = k_ref[0, rows, :]
            v = v_ref[0, rows, :]
            kk = kk_ref[0, rows, :]
            b_vec = kk * a_ref[0, rows, :]
            ra = r_ref[0, rows, :] * jnp.exp(cum)
            aa = -kk * jnp.exp(cum - lw)
            bb = b_vec * e_neg
            kb = k * e_neg
            bt = b_vec * e_tail
            kt = k * e_tail
            for h in range(RK_HEADS):
                lanes = slice(h * hd, (h + 1) * hd)
                chains.append(dict(
                    d=d, h=h, rows=rows, lanes=lanes, y_ref=y_ref, incl=incl, strict=strict,
                    aa=aa[:, lanes], ra=ra[:, lanes], bb=bb[:, lanes], kb=kb[:, lanes], v=v[:, lanes],
                    bt=bt[:, lanes], kt=kt[:, lanes], e_tot=e_tot[:, lanes]))
        for ch in chains:
            ch["lhs"] = jnp.concatenate([ch["aa"], ch["ra"]], axis=0)
        gb = [_dot16(ch["lhs"], ch["bb"], "nt") for ch in chains]
        gk = [_dot16(ch["lhs"], ch["kb"], "nt") for ch in chains]
        a_ab = [jnp.where(ch["strict"], g[:c], 0.0) for g, ch in zip(gb, chains)]
        m_rb = [jnp.where(ch["incl"], g[c:], 0.0) for g, ch in zip(gb, chains)]
        a_ak = [jnp.where(ch["strict"], g[:c], 0.0) for g, ch in zip(gk, chains)]
        m_rk = [jnp.where(ch["incl"], g[c:], 0.0) for g, ch in zip(gk, chains)]
        av = [_dot16(m, ch["v"]) for m, ch in zip(a_ak, chains)]
        yv = [_dot16(m, ch["v"]) for m, ch in zip(m_rk, chains)]
        t_inv = _nilpotent_inverses(a_ab, eye)
        ta = [_dot16(t, ch["aa"]) for t, ch in zip(t_inv, chains)]
        qm = [_dot16(t, x) for t, x in zip(t_inv, av)]
        ya = [ch["ra"] + _dot16(m, x) for ch, m, x in zip(chains, m_rb, ta)]
        yb = [_dot16(m, x) + y for m, x, y in zip(m_rb, qm, yv)]
        wm = [_dot16(x, ch["bt"], "tn") for x, ch in zip(ta, chains)]
        hc = [_dot16(jnp.concatenate([x, ch["v"]], axis=0), jnp.concatenate([ch["bt"], ch["kt"]], axis=0), "tn")
              for x, ch in zip(qm, chains)]
        states = [s_ref[ch["d"] * RK_HEADS + ch["h"]] for ch in chains]
        outs = [_dot16(a, s, "nt") + b for a, b, s in zip(ya, yb, states)]
        news = [s * ch["e_tot"] + _dot16(s, w) + x for s, ch, w, x in zip(states, chains, wm, hc)]
        for ch, y, s_new in zip(chains, outs, news):
            ch["y_ref"][0, ch["rows"], ch["lanes"]] = y
            s_ref[ch["d"] * RK_HEADS + ch["h"]] = s_new
        return carry

    lax.fori_loop(0, n_sub, chunk, 0)


def _rk_scan(r, k, v, kk, a, lw):
    b, l, w = r.shape
    t = ROW_TILE
    n = l // t
    fwd = pl.BlockSpec((1, t, w), lambda bi, ti: (bi, ti, 0))
    bwd = pl.BlockSpec((1, t, w), lambda bi, ti: (bi, n - 1 - ti, 0))
    bwd_lw = pl.BlockSpec((1, t, w), lambda bi, ti: (bi, n - 1 - ti, 1))
    return pl.pallas_call(
        _rk_scan_kernel, name="rk_scan",
        grid=(b, n),
        in_specs=[fwd] * 6 + [bwd] * 5 + [bwd_lw],
        out_specs=[fwd, bwd],
        out_shape=[jax.ShapeDtypeStruct((b, l, w), F32)] * 2,
        scratch_shapes=[pltpu.VMEM((2 * RK_HEADS, RK_HEAD, RK_HEAD), F32)],
        compiler_params=_cparams(("parallel", "arbitrary")),
    )(r, k, v, kk, a, lw, r, k, v, kk, a, lw)


def _even_mix_kernel(of_ref, ob_ref, dg_ref, dnw_ref, yf_ref, yb_ref, r_ref, k_ref, v_ref, rg_ref,
                     rk_ref, lnw_ref, lnb_ref, segm_ref, seg1_ref, o_ref):
    o = of_ref[0] + ob_ref[0]
    gate = _silu(dg_ref[0])
    for h in range(DN_HEADS):
        lanes = slice(h * DN_DV, (h + 1) * DN_DV)
        oh = o[:, lanes]
        ms = jnp.mean(oh * oh, axis=-1, keepdims=True)
        o_ref[0, :, lanes] = oh * lax.rsqrt(ms + RMS_EPS) * dnw_ref[...] * gate[:, lanes]
    wkv = yf_ref[0] + yb_ref[0]
    mean = _dot_exact_rhs(wkv, segm_ref[...])
    cen = wkv - mean
    var = _dot_exact_rhs(cen * cen, segm_ref[...])
    wkv = cen * lax.rsqrt(var + RK_GN_EPS) * lnw_ref[...] + lnb_ref[...]
    v = v_ref[0]
    bonus = _dot_exact_rhs(r_ref[0] * k_ref[0] * rk_ref[...], seg1_ref[...]) * v
    o_ref[0, :, DN_WIDTH:] = (wkv + bonus) * _silu(rg_ref[0])


def _even_mix(o_f, o_b, dn_gate, dn_norm, y_f, y_b, r, k, v, rk_gate, r_k, ln_w, ln_b):
    b, l, _ = o_f.shape
    t = ROW_TILE
    w = RK_WIDTH
    blk = lambda width: pl.BlockSpec((1, t, width), lambda bi, ti: (bi, ti, 0))
    seg1 = jnp.asarray(_seg_ones(w, RK_HEAD), dtype=BF16)
    segm = jnp.asarray(_seg_ones(w, RK_HEAD) / RK_HEAD, dtype=BF16)
    return pl.pallas_call(
        _even_mix_kernel, name="even_mix",
        grid=(b, l // t),
        in_specs=[blk(DN_WIDTH), blk(DN_WIDTH), blk(DN_WIDTH), _full((1, DN_DV)),
                  blk(w), blk(w), blk(w), blk(w), blk(w), blk(w),
                  _full((1, w)), _full((1, w)), _full((1, w)), _full((w, w)), _full((w, w))],
        out_specs=blk(DN_WIDTH + w),
        out_shape=jax.ShapeDtypeStruct((b, l, DN_WIDTH + w), F32),
        compiler_params=_cparams(("parallel", "parallel")),
    )(o_f, o_b, dn_gate, dn_norm.astype(F32).reshape(1, DN_DV), y_f, y_b, r, k, v, rk_gate,
      r_k.astype(F32).reshape(1, w), ln_w.astype(F32).reshape(1, w), ln_b.astype(F32).reshape(1, w), segm, seg1)


def _even_layer(h, w_in, dn_conv, dn_a_log, dn_dt_bias, dn_norm, rk_mu, rk_w0, rk_w2, rk_a0, rk_a2,
                rk_k_k, rk_k_a, rk_r_k, rk_ln_w, rk_ln_b):
    b, l, d = h.shape
    s0 = DN_QKV
    s1 = s0 + DN_AB
    s2 = s1 + DN_WIDTH
    s3 = s2 + RK_SHIFT
    pad = lambda m, width: jnp.pad(m, ((0, 0), (0, width - m.shape[1])))
    widths = (DN_QKV, LANE, DN_WIDTH, RK_SHIFT_PAD, RK_WIDTH)
    w16 = jnp.concatenate([w_in[:, :s0], pad(w_in[:, s0:s1], LANE), w_in[:, s1:s2],
                           pad(w_in[:, s2:s3], RK_SHIFT_PAD), w_in[:, s3:]], axis=1).astype(BF16)
    qkv, ab, dn_gate, rk, rk_gate = _project(h.reshape(b * l, d), w16, widths)
    r3 = lambda m: m.reshape(b, l, m.shape[-1])
    q, k, v, gb = _dn_prep(r3(qkv), r3(ab), dn_conv, dn_a_log, dn_dt_bias)
    o_f, o_b = _dn_scan(q, k, v, gb)
    r, kr, vr, kk, a, lw = _rk_prep(r3(rk), rk_mu, rk_w0, rk_w2, rk_a0, rk_a2, rk_k_k, rk_k_a)
    y_f, y_b = _rk_scan(r, kr, vr, kk, a, lw)
    return _even_mix(o_f, o_b, r3(dn_gate), dn_norm, y_f, y_b, r, kr, vr, r3(rk_gate),
                     rk_r_k, rk_ln_w, rk_ln_b)


def _dft_geometry(l):
    nf = 2 * l
    p = nf // DFT_Q
    n1 = p // 2
    k1 = p // 2 + 1
    k1p = -(-k1 // SUBLANE) * SUBLANE
    return nf, p, n1, k1, k1p


@functools.lru_cache(maxsize=None)
def _dft_tables(l):
    nf, p, n1c, k1c, k1p = _dft_geometry(l)
    q = DFT_Q
    n2 = np.arange(q)[:, None, None]
    k1 = np.arange(k1c)[None, :, None]
    n1 = np.arange(n1c)[None, None, :]
    ph = -2.0 * np.pi * (((n1 * k1) % p) / p + ((n2 * k1) % nf) / nf)
    fa = np.zeros((q, 2 * k1p, n1c))
    fa[:, :k1c] = np.cos(ph)
    fa[:, k1p:k1p + k1c] = np.sin(ph)
    wgt = np.full((k1c,), 2.0)
    wgt[0] = 1.0
    wgt[-1] = 1.0
    th = -ph.transpose(0, 2, 1)
    gd = np.zeros((q, n1c, 2 * k1p))
    gd[:, :, :k1c] = np.cos(th) * wgt / nf
    gd[:, :, k1p:k1p + k1c] = -np.sin(th) * wgt / nf
    a = np.arange(q)
    ang = -2.0 * np.pi * ((a[:, None] * a[None, :]) % q) / q
    cr, ci = np.cos(ang), np.sin(ang)
    fb = np.block([[cr, -ci], [ci, cr]])
    fc = np.block([[cr, ci], [-ci, cr]])

    def hl(m):
        m32 = m.astype(np.float32)
        hi = m32.astype(BF16)
        lo = (m32 - hi.astype(np.float32)).astype(BF16)
        return hi, lo

    return tuple(hl(m) for m in (fa, fb, fc, gd))


def _fdot(fs, x):
    x_hi = x.astype(BF16)
    out = jnp.dot(fs[0], x_hi, preferred_element_type=F32)
    if len(fs) == 2:
        x_lo = (x - x_hi.astype(F32)).astype(BF16)
        out = out + jnp.dot(fs[1], x_hi, preferred_element_type=F32) + jnp.dot(fs[0], x_lo, preferred_element_type=F32)
    return out


DFT_GROUP_N2 = 8
DFT_GROUP_K1 = 4


def _stage_a(u_ref, y_re, y_im, fa, geo):
    nf, p, n1c, k1c, k1p = geo
    g = DFT_GROUP_N2

    def body(i, carry):
        n2s = [i * g + t for t in range(g)]
        slabs = [u_ref[pl.ds(n2, n1c, stride=DFT_Q), :] for n2 in n2s]
        outs = [_fdot([f[n2] for f in fa], slab) for n2, slab in zip(n2s, slabs)]
        for n2, out in zip(n2s, outs):
            y_re[pl.ds(n2, k1p, stride=DFT_Q), :] = out[:k1p]
            y_im[pl.ds(n2, k1p, stride=DFT_Q), :] = out[k1p:]
        return carry

    lax.fori_loop(0, DFT_Q // g, body, 0)


def _stage_b(y_re, y_im, i, fb):
    g = DFT_GROUP_K1
    rows = [pl.ds(pl.multiple_of((i * g + t) * DFT_Q, DFT_Q), DFT_Q) for t in range(g)]
    ws = [jnp.concatenate([y_re[r, :], y_im[r, :]], axis=0) for r in rows]
    zs = [_fdot([f[...] for f in fb], w) for w in ws]
    return rows, zs


def _hy_conv_kernel(u_ref, m_ref, skip_ref, hr_ref, hi_ref, fa, fb, fc, gd, o_ref, y_re, y_im, *, geo):
    nf, p, n1c, k1c, k1p = geo
    u2 = u_ref.at[0]
    o2 = o_ref.at[0]
    _stage_a(u2, y_re, y_im, (fa,), geo)

    def mid(i, carry):
        rows, zs = _stage_b(y_re, y_im, i, (fb,))
        prods = []
        for r, z in zip(rows, zs):
            zr, zi = z[:DFT_Q], z[DFT_Q:]
            hr = hr_ref[r, :]
            hi = hi_ref[r, :]
            prods.append(jnp.concatenate([zr * hr - zi * hi, zr * hi + zi * hr], axis=0))
        outs = [_fdot((fc[...],), x) for x in prods]
        for r, a in zip(rows, outs):
            y_re[r, :] = a[:DFT_Q]
            y_im[r, :] = a[DFT_Q:]
        return carry

    lax.fori_loop(0, k1p // DFT_GROUP_K1, mid, 0)

    def last(i, carry):
        n2s = [i * DFT_GROUP_N2 + t for t in range(DFT_GROUP_N2)]
        ins = [jnp.concatenate([y_re[pl.ds(n2, k1p, stride=DFT_Q), :], y_im[pl.ds(n2, k1p, stride=DFT_Q), :]], axis=0)
               for n2 in n2s]
        outs = [_fdot((gd[n2],), a) for n2, a in zip(n2s, ins)]
        for n2, out in zip(n2s, outs):
            o2[pl.ds(n2, n1c, stride=DFT_Q), :] = out
        return carry

    lax.fori_loop(0, DFT_Q // DFT_GROUP_N2, last, 0)

    blk = 512
    skip = skip_ref[...]

    def epi(i, carry):
        rows = pl.ds(pl.multiple_of(i * blk, blk), blk)
        uu = u2[rows, :]
        o2[rows, :] = m_ref[0, rows, :] * (o2[rows, :] + skip * uu)
        return carry

    lax.fori_loop(0, u_ref.shape[1] // blk, epi, 0)


def _single(shape, index_map):
    return pl.BlockSpec(shape, index_map, pipeline_mode=pl.Buffered(1))


def _hy_conv(u, mult, skip, h_re, h_im, tables):
    b, l, ch = u.shape
    geo = _dft_geometry(l)
    nf, p, n1c, k1c, k1p = geo
    ct = HY_CT
    consts = [jnp.asarray(pair[0]) for pair in tables]
    seq = pl.BlockSpec((1, l, ct), lambda ci, bi: (bi, 0, ci))
    spec = _single((k1p * DFT_Q, ct), lambda ci, bi: (0, ci))
    cspecs = [_single(c.shape, (lambda ci, bi, nd=c.ndim: (0,) * nd)) for c in consts]
    return pl.pallas_call(
        functools.partial(_hy_conv_kernel, geo=geo), name="hy_conv",
        grid=(ch // ct, b),
        in_specs=[seq, seq, pl.BlockSpec((1, ct), lambda ci, bi: (0, ci)), spec, spec] + cspecs,
        out_specs=seq,
        out_shape=jax.ShapeDtypeStruct((b, l, ch), F32),
        scratch_shapes=[pltpu.VMEM((k1p * DFT_Q, ct), F32), pltpu.VMEM((k1p * DFT_Q, ct), F32)],
        compiler_params=_cparams(("parallel", "parallel")),
    )(u, mult, skip, h_re, h_im, *consts)


def _hy_mlp_kernel(f_ref, w1_ref, b1_ref, w2_ref, b2_ref, w3_ref, b3_ref, fr_ref, o_ref):
    fr = fr_ref[...]
    hdn = jnp.sin(fr * (_dot32(f_ref[...], w1_ref[...]) + b1_ref[...]))
    hdn = jnp.sin(fr * (_dot32(hdn, w2_ref[...]) + b2_ref[...]))
    o_ref[...] = jnp.sin(fr * (_dot32(hdn, w3_ref[...]) + b3_ref[...]))


def _hy_mlp(feats, w1, b1, w2, b2, w3, b3, freq):
    l = feats.shape[0]
    t = min(l, 1024)
    fw = HY_FW
    row = lambda a: a.astype(F32).reshape(1, fw)
    w1p = jnp.zeros((LANE, fw), F32).at[:HY_EMB].set(w1.astype(F32))
    return pl.pallas_call(
        _hy_mlp_kernel, name="hy_mlp",
        grid=(l // t,),
        in_specs=[pl.BlockSpec((t, LANE), lambda i: (i, 0)), _full((LANE, fw)), _full((1, fw)), _full((fw, fw)),
                  _full((1, fw)), _full((fw, fw)), _full((1, fw)), _full((1, fw))],
        out_specs=pl.BlockSpec((t, fw), lambda i: (i, 0)),
        out_shape=jax.ShapeDtypeStruct((l, fw), F32),
        compiler_params=_cparams(("parallel",)),
    )(feats, w1p, row(b1), w2.astype(F32), row(b2), w3.astype(F32), row(b3), row(freq))


def _hy_filter_kernel(hdn_ref, wf_ref, wb_ref, df_ref, db_ref, fa_hi, fa_lo, fb_hi, fb_lo,
                      hr_ref, hi_ref, filt, yr_f, yi_f, yr_b, yi_b, *, geo):
    nf, p, n1c, k1c, k1p = geo
    fa = (fa_hi, fa_lo)
    fb = (fb_hi, fb_lo)
    l = hdn_ref.shape[0]
    blk = min(l, 512)
    nblk = l // blk

    def build(w_ref, d_ref, drop_first):
        def body(i, acc):
            rows = pl.ds(pl.multiple_of(i * blk, blk), blk)
            hx = hdn_ref[rows, :]
            tt = hx[:, HY_FW:HY_FW + 1]
            hv = _dot32(hx, w_ref[0]) * jnp.exp(-tt * jnp.abs(d_ref[0]))
            if drop_first:
                pos = lax.broadcasted_iota(jnp.int32, hv.shape, 0) + i * blk
                hv = jnp.where(pos == 0, 0.0, hv)
            filt[rows, :] = hv
            return acc + jnp.sum(jnp.abs(hv), axis=0, keepdims=True)

        return lax.fori_loop(0, nblk, body, jnp.zeros((1, filt.shape[1]), F32))

    l1 = build(wf_ref, df_ref, False)
    _stage_a(filt, yr_f, yi_f, fa, geo)
    l1 = l1 + build(wb_ref, db_ref, True)
    _stage_a(filt, yr_b, yi_b, fa, geo)
    inv = 1.0 / (l1 + RMS_EPS)

    def mid(i, carry):
        rows, zf = _stage_b(yr_f, yi_f, i, fb)
        _, zb = _stage_b(yr_b, yi_b, i, fb)
        for r, f, b in zip(rows, zf, zb):
            hr_ref[0, r, :] = (f[:DFT_Q] + b[:DFT_Q]) * inv
            hi_ref[0, r, :] = (f[DFT_Q:] - b[DFT_Q:]) * inv
        return carry

    lax.fori_loop(0, k1p // DFT_GROUP_K1, mid, 0)


def _hy_filters(hdn, tcol, w_out, deltas, tables, l):
    geo = _dft_geometry(l)
    nf, p, n1c, k1c, k1p = geo
    ch = w_out.shape[1] // (2 * HY_ORDER)
    ct = HY_CT
    nct = ch // ct
    hdn_x = jnp.concatenate([hdn, tcol, jnp.zeros((l, LANE - HY_FW - 1), F32)], axis=1)
    w4 = w_out.astype(F32).reshape(HY_FW, 2 * HY_ORDER, ch).transpose(1, 0, 2)
    w4 = jnp.pad(w4, ((0, 0), (0, LANE - HY_FW), (0, 0)))
    d4 = deltas.astype(F32).reshape(2 * HY_ORDER, 1, ch)
    consts = [jnp.asarray(t) for pair in tables[:2] for t in pair]
    cspecs = [_single(c.shape, (lambda o, ci, nd=c.ndim: (0,) * nd)) for c in consts]
    wspec = lambda d: pl.BlockSpec((1, LANE, ct), lambda o, ci: (2 * o + d, 0, ci))
    dspec = lambda d: pl.BlockSpec((1, 1, ct), lambda o, ci: (2 * o + d, 0, ci))
    ospec = pl.BlockSpec((1, k1p * DFT_Q, ct), lambda o, ci: (o, 0, ci))
    ysc = pltpu.VMEM((k1p * DFT_Q, ct), F32)
    return pl.pallas_call(
        functools.partial(_hy_filter_kernel, geo=geo), name="hy_filters",
        grid=(HY_ORDER, nct),
        in_specs=[_single((l, LANE), lambda o, ci: (0, 0)), wspec(0), wspec(1), dspec(0), dspec(1)] + cspecs,
        out_specs=[ospec, ospec],
        out_shape=[jax.ShapeDtypeStruct((HY_ORDER, k1p * DFT_Q, ch), F32)] * 2,
        scratch_shapes=[pltpu.VMEM((l, ct), F32), ysc, ysc, ysc, ysc],
        compiler_params=_cparams(("parallel", "parallel")),
    )(hdn_x, w4, w4, d4, d4, *consts)


def _position_features(l):
    bands = (HY_EMB - 1) // 2
    t = jnp.linspace(0.0, 1.0, l, dtype=F32)[:, None]
    f = jnp.linspace(1e-4, bands - 1, bands, dtype=F32)[None, :]
    ang = (2.0 * math.pi / l) * jnp.arange(l, dtype=F32)[:, None] * f
    feats = jnp.concatenate([t, jnp.cos(ang), -jnp.sin(ang)], axis=-1)
    return jnp.pad(feats, ((0, 0), (0, LANE - HY_EMB))), t


def _hy_prep_kernel(x_ref, xp_ref, xn_ref, g_ref, cw_ref, cb_ref, x1_ref, m2_ref, v_ref):
    t_idx = pl.program_id(1)
    n_t = pl.num_programs(1)
    cur = x_ref[0]
    prev8 = jnp.where(t_idx > 0, xp_ref[0], 0.0)
    next8 = jnp.where(t_idx < n_t - 1, xn_ref[0], 0.0)
    y = (_shifted(prev8, cur, next8, -1) * cw_ref[0:1, :] + cur * cw_ref[1:2, :]
         + _shifted(prev8, cur, next8, 1) * cw_ref[2:3, :] + cb_ref[...])
    c = g_ref.shape[2]
    x1_ref[0] = y[:, :c]
    m2_ref[0] = y[:, c:2 * c] * _silu(g_ref[0])
    v_ref[0] = y[:, 2 * c:]


def _hy_prep(xv, gate, conv_w, conv_b):
    b, l, w3 = xv.shape
    c = gate.shape[2]
    t = ROW_TILE
    prev, nxt = _halo_specs(t, w3, 0, l)
    blk = lambda width: pl.BlockSpec((1, t, width), lambda bi, ti: (bi, ti, 0))
    return pl.pallas_call(
        _hy_prep_kernel, name="hy_prep",
        grid=(b, l // t),
        in_specs=[blk(w3), prev, nxt, blk(c), _full((HY_SHORT, w3)), _full((1, w3))],
        out_specs=[blk(c)] * 3,
        out_shape=[jax.ShapeDtypeStruct((b, l, c), F32)] * 3,
        compiler_params=_cparams(("parallel", "parallel")),
    )(xv, xv, xv, gate, conv_w.astype(F32), conv_b.astype(F32).reshape(1, w3))


def _odd_layer(h, w_in, conv_w, conv_b, f_w1, f_b1, f_w2, f_b2, f_w3, f_b3, f_freq, f_out, deltas, skip):
    b, l, d = h.shape
    c = skip.shape[1]
    xv, gate = _project(h.reshape(b * l, d), w_in.astype(BF16), (3 * c, c))
    x1, m2, v = _hy_prep(xv.reshape(b, l, 3 * c), gate.reshape(b, l, c), conv_w, conv_b)
    tables = _dft_tables(l)
    feats, tcol = _position_features(l)
    hdn = _hy_mlp(feats, f_w1, f_b1, f_w2, f_b2, f_w3, f_b3, f_freq)
    h_re, h_im = _hy_filters(hdn, tcol, f_out, deltas, tables, l)
    skip = skip.astype(F32)
    z = _hy_conv(v, x1, skip[0:1], h_re[0], h_im[0], tables)
    return _hy_conv(z, m2, skip[1:2], h_re[1], h_im[1], tables)


def kernel(x, p, even_w_in, dn_conv, dn_a_log, dn_dt_bias, dn_norm, rk_mu, rk_w0, rk_w2, rk_a0, rk_a2, rk_k_k, rk_k_a, rk_r_k, rk_ln_w, rk_ln_b, odd_w_in, hy_conv_w, hy_conv_b, hy_ffn_w1, hy_ffn_b1, hy_ffn_w2, hy_ffn_b2, hy_ffn_w3, hy_ffn_b3, hy_ffn_freq, hy_ffn_out, hy_deltas, hy_skip, w_out, ln_g, ln_b, ple_w, ple_norm, ple_gate):
    b, l, d = x.shape
    depth = p.shape[0]
    alpha = (2.0 * depth) ** 0.25
    h = x
    for i in range(depth):
        j = i // 2
        if i % 2 == 0:
            mix = _even_layer(h, even_w_in[j], dn_conv[j], dn_a_log[j], dn_dt_bias[j], dn_norm[j], rk_mu[j],
                              rk_w0[j], rk_w2[j], rk_a0[j], rk_a2[j], rk_k_k[j].reshape(-1), rk_k_a[j].reshape(-1),
                              rk_r_k[j].reshape(-1), rk_ln_w[j], rk_ln_b[j])
        else:
            mix = _odd_layer(h, odd_w_in[j], hy_conv_w[j], hy_conv_b[j], hy_ffn_w1[j], hy_ffn_b1[j],
                             hy_ffn_w2[j], hy_ffn_b2[j], hy_ffn_w3[j], hy_ffn_b3[j], hy_ffn_freq[j],
                             hy_ffn_out[j], hy_deltas[j], hy_skip[j])
        h2 = _post_layer(h.reshape(b * l, d), mix.reshape(b * l, mix.shape[-1]), p[i].reshape(b * l, p.shape[-1]),
                         w_out[i], ple_w[i], ple_gate[i], ln_g[i], ln_b[i], ple_norm[i], alpha)
        h = h2.reshape(b, l, d)
    return h
```

```python
import functools
import math

import numpy as np
import jax
import jax.numpy as jnp
from jax import lax
from jax.experimental import pallas as pl
from jax.experimental.pallas import tpu as pltpu

F32 = jnp.float32
BF16 = jnp.bfloat16

LN_EPS = 1e-5
RMS_EPS = 1e-6

DN_HEADS = 4
DN_DK = 128
DN_DV = 128
DN_WIDTH = DN_HEADS * DN_DV
DN_QKV = 2 * DN_HEADS * DN_DK + DN_WIDTH
DN_AB = 4 * DN_HEADS
DN_CONV = 5
DN_CHUNK = 64

RK_HEADS = 8
RK_HEAD = 64
RK_WIDTH = RK_HEADS * RK_HEAD
RK_LORA = 64
RK_SHIFT = 3 * RK_WIDTH + 3 * RK_LORA
RK_SHIFT_PAD = 1792
RK_GN_EPS = 64e-5
RK_CHUNK = 64

HY_ORDER = 2
HY_SHORT = 3
HY_EMB = 33
HY_FW = 64

LANE = 128
SUBLANE = 8
DFT_Q = 128
VMEM_LIMIT = 56 * 1024 * 1024

ROW_TILE = 256
SCAN_TILE = 512
HY_CT = 128

HI = lax.Precision.HIGHEST
ACT = BF16


def _cparams(sem):
    return pltpu.CompilerParams(dimension_semantics=sem, vmem_limit_bytes=VMEM_LIMIT)


_DIMS = {
    "nn": (((1,), (0,)), ((), ())),
    "nt": (((1,), (1,)), ((), ())),
    "tn": (((0,), (0,)), ((), ())),
}


def _dot16(a, b, dims="nn"):
    return lax.dot_general(a.astype(BF16), b.astype(BF16), _DIMS[dims], preferred_element_type=F32)


def _dot32(a, b, dims="nn"):
    return lax.dot_general(a.astype(F32), b.astype(F32), _DIMS[dims], precision=HI,
                           preferred_element_type=F32)


def _split2(x):
    hi = x.astype(BF16)
    lo = (x - hi.astype(F32)).astype(BF16)
    return hi, lo


def _dot_exact_rhs(x, m16):
    hi, lo = _split2(x)
    return (jnp.dot(hi, m16, preferred_element_type=F32) + jnp.dot(lo, m16, preferred_element_type=F32))


def _sigmoid(x):
    return 1.0 / (1.0 + jnp.exp(-x))


def _silu(x):
    return x * _sigmoid(x)


def _softplus(x):
    return jnp.maximum(x, 0.0) + jnp.log1p(jnp.exp(-jnp.abs(x)))


HALO = 16


def _shifted(prev, cur, nxt, d):
    t = cur.shape[0]
    ext = jnp.concatenate([prev, cur, nxt], axis=0)
    return ext[HALO + d:HALO + d + t]


def _halo_specs(t_rows, width, col, l_total):
    nb = t_rows // HALO
    last = l_total // HALO - 1
    prev = pl.BlockSpec((1, HALO, width), lambda b, t: (b, jnp.maximum(t * nb - 1, 0), col))
    nxt = pl.BlockSpec((1, HALO, width), lambda b, t: (b, jnp.minimum((t + 1) * nb, last), col))
    return prev, nxt


def _load_with_halo(x_ref, xp_ref, xn_ref):
    t_idx = pl.program_id(1)
    cur = x_ref[0].astype(F32)
    prev = jnp.where(t_idx > 0, xp_ref[0].astype(F32), 0.0)
    nxt = jnp.where(t_idx < pl.num_programs(1) - 1, xn_ref[0].astype(F32), 0.0)
    return prev, cur, nxt


def _full(shape):
    nd = len(shape)
    return pl.BlockSpec(shape, lambda *_: (0,) * nd)


def _proj_kernel(a_ref, w_ref, *o_refs, offs):
    a = a_ref[...].astype(BF16)
    for o_ref, (lo, hi) in zip(o_refs, offs):
        o_ref[...] = jnp.dot(a, w_ref[:, lo:hi], preferred_element_type=F32).astype(o_ref.dtype)


def _project(a, w16, widths, dtypes):
    m, k = a.shape
    offs, o = [], 0
    for w in widths:
        offs.append((o, o + w))
        o += w
    n = o
    tm = ROW_TILE
    return pl.pallas_call(
        functools.partial(_proj_kernel, offs=tuple(offs)), name="project",
        grid=(m // tm,),
        in_specs=[pl.BlockSpec((tm, k), lambda i: (i, 0)), _full((k, n))],
        out_specs=[pl.BlockSpec((tm, w), lambda i: (i, 0)) for w in widths],
        out_shape=[jax.ShapeDtypeStruct((m, w), dt) for w, dt in zip(widths, dtypes)],
        compiler_params=_cparams(("parallel",)),
    )(a, w16)


def _post_kernel(h_ref, mix_ref, p_ref, wo_ref, pw_ref, pg_ref, lng_ref, lnb_ref, pn_ref, o_ref, *, alpha):
    t = alpha * h_ref[...] + jnp.dot(mix_ref[...], wo_ref[...], preferred_element_type=F32)
    mu = jnp.mean(t, axis=-1, keepdims=True)
    tc = t - mu
    var = jnp.mean(tc * tc, axis=-1, keepdims=True)
    y = tc * lax.rsqrt(var + LN_EPS) * lng_ref[...] + lnb_ref[...]
    e = jnp.dot(p_ref[...].astype(BF16), pw_ref[...], preferred_element_type=F32)
    e = e * lax.rsqrt(jnp.mean(e * e, axis=-1, keepdims=True) + RMS_EPS) * pn_ref[...]
    gate = _sigmoid(jnp.dot(y.astype(BF16), pg_ref[...], preferred_element_type=F32))
    o_ref[...] = y + gate * e


def _post_layer(h, mix, p, w_out, ple_w, ple_gate, ln_g, ln_b, ple_norm, alpha):
    m, d = h.shape
    pd = p.shape[1]
    tm = ROW_TILE
    row = lambda w: pl.BlockSpec((tm, w), lambda i: (i, 0))
    return pl.pallas_call(
        functools.partial(_post_kernel, alpha=alpha), name="post_layer",
        grid=(m // tm,),
        in_specs=[row(d), row(mix.shape[1]), row(pd), _full(w_out.shape), _full(ple_w.shape),
                  _full(ple_gate.shape), _full((1, d)), _full((1, d)), _full((1, d))],
        out_specs=row(d),
        out_shape=jax.ShapeDtypeStruct((m, d), F32),
        compiler_params=_cparams(("parallel",)),
    )(h, mix, p, w_out.astype(BF16), ple_w.astype(BF16), ple_gate.astype(BF16),
      ln_g.reshape(1, d), ln_b.reshape(1, d), ple_norm.reshape(1, d))


def _dn_prep_kernel(x_ref, xp_ref, xn_ref, ab_ref, cw_ref, ga_ref, gbias_ref, q_ref, k_ref, v_ref, gb_ref):
    prev, cur, nxt = _load_with_halo(x_ref, xp_ref, xn_ref)
    pad = DN_CONV // 2
    acc = cur * cw_ref[pad:pad + 1, :]
    for j in range(DN_CONV):
        if j != pad:
            acc = acc + _shifted(prev, cur, nxt, j - pad) * cw_ref[j:j + 1, :]
    y = _silu(acc)
    nqk = DN_HEADS * DN_DK
    for h in range(DN_HEADS):
        qh = y[:, h * DN_DK:(h + 1) * DN_DK]
        kh = y[:, nqk + h * DN_DK:nqk + (h + 1) * DN_DK]
        qn = lax.rsqrt(jnp.sum(qh * qh, axis=-1, keepdims=True) + RMS_EPS) * (DN_DK ** -0.5)
        kn = lax.rsqrt(jnp.sum(kh * kh, axis=-1, keepdims=True) + RMS_EPS)
        q_ref[0, :, h * DN_DK:(h + 1) * DN_DK] = (qh * qn).astype(q_ref.dtype)
        k_ref[0, :, h * DN_DK:(h + 1) * DN_DK] = (kh * kn).astype(k_ref.dtype)
    v_ref[0] = y[:, 2 * nqk:].astype(v_ref.dtype)
    ab = ab_ref[0]
    lane = lax.broadcasted_iota(jnp.int32, ab.shape, 1)
    g = ga_ref[...] * _softplus(ab + gbias_ref[...])
    gb_ref[0] = jnp.where(lane < 2 * DN_HEADS, g, _sigmoid(ab))


def _dn_prep(qkv, ab, conv_w, a_log, dt_bias):
    b, l, _ = qkv.shape
    t = SCAN_TILE
    ga = jnp.zeros((1, LANE), F32).at[0, :2 * DN_HEADS].set(-jnp.exp(a_log.astype(F32)).reshape(-1))
    gbias = jnp.zeros((1, LANE), F32).at[0, :2 * DN_HEADS].set(dt_bias.astype(F32).reshape(-1))
    prev, nxt = _halo_specs(t, DN_QKV, 0, l)
    blk = lambda w: pl.BlockSpec((1, t, w), lambda bi, ti: (bi, ti, 0))
    return pl.pallas_call(
        _dn_prep_kernel, name="dn_prep",
        grid=(b, l // t),
        in_specs=[blk(DN_QKV), prev, nxt, blk(LANE), _full((DN_CONV, DN_QKV)), _full((1, LANE)), _full((1, LANE))],
        out_specs=[blk(DN_WIDTH), blk(DN_WIDTH), blk(DN_WIDTH), blk(LANE)],
        out_shape=[jax.ShapeDtypeStruct((b, l, DN_WIDTH), ACT)] * 3 + [jax.ShapeDtypeStruct((b, l, LANE), F32)],
        compiler_params=_cparams(("parallel", "parallel")),
    )(qkv, qkv, qkv, ab, conv_w.astype(F32), ga, gbias)


def _tri_masks(c, reverse):
    r = lax.broadcasted_iota(jnp.int32, (c, c), 0)
    s = lax.broadcasted_iota(jnp.int32, (c, c), 1)
    if reverse:
        return s >= r, s > r, r >= s
    return s <= r, s < r, r <= s


def _nilpotent_inverses(xs, eye):
    c = xs[0].shape[0]
    rs = [eye + x for x in xs]
    ps = list(xs)
    for _ in range(int(math.log2(c)) - 1):
        ps = [_dot16(p, p) for p in ps]
        rs = [r + _dot16(r, p) for r, p in zip(rs, ps)]
    return rs


def _dn_scan_kernel(qf_ref, kf_ref, vf_ref, gf_ref, qb_ref, kb_ref, vb_ref, gb_ref, of_ref, ob_ref, s_ref):
    c = DN_CHUNK
    n_sub = qf_ref.shape[1] // c

    @pl.when(pl.program_id(1) == 0)
    def _():
        s_ref[...] = jnp.zeros_like(s_ref)

    eye_b = lax.broadcasted_iota(jnp.int32, (c, c), 0) == lax.broadcasted_iota(jnp.int32, (c, c), 1)
    eye = eye_b.astype(F32)
    neg = jnp.float32(-1e30)
    masks = (_tri_masks(c, False), _tri_masks(c, True))
    sides = ((qf_ref, kf_ref, vf_ref, gf_ref, of_ref), (qb_ref, kb_ref, vb_ref, gb_ref, ob_ref))

    def chunk(j, carry):
        chains = []
        for d, (q_ref, k_ref, v_ref, g_ref, o_ref) in enumerate(sides):
            jj = (n_sub - 1 - j) if d else j
            rows = pl.ds(pl.multiple_of(jj * c, c), c)
            incl, strict, incl_t = masks[d]
            gb = g_ref[0, rows, :]
            for h in range(DN_HEADS):
                g_col = gb[:, d * DN_HEADS + h:d * DN_HEADS + h + 1]
                beta = gb[:, (2 + d) * DN_HEADS + h:(2 + d) * DN_HEADS + h + 1]
                g_row = jnp.sum(jnp.where(eye_b, g_col, 0.0), axis=0, keepdims=True)
                cum_col = jnp.sum(jnp.where(incl, g_row, 0.0), axis=1, keepdims=True)
                cum_row = jnp.sum(jnp.where(incl_t, g_col, 0.0), axis=0, keepdims=True)
                g_tot = jnp.sum(g_col, axis=0, keepdims=True)
                lanes = slice(h * DN_DK, (h + 1) * DN_DK)
                q = q_ref[0, rows, lanes].astype(F32)
                k = k_ref[0, rows, lanes].astype(F32)
                v = v_ref[0, rows, lanes].astype(F32)
                eg = jnp.exp(cum_col)
                k_beta = k * beta
                chains.append(dict(
                    d=d, h=h, rows=rows, lanes=lanes, o_ref=o_ref, strict=strict,
                    decay=jnp.exp(jnp.where(incl, cum_col - cum_row, neg)),
                    e_tot=jnp.exp(g_tot), q_dec=q * eg, k=k,
                    lhs=jnp.concatenate([k_beta, q], axis=0),
                    rhs=jnp.concatenate([v * beta, k_beta * eg], axis=1),
                    k_tail=k * jnp.exp(g_tot - cum_col)))
        gram = [_dot16(ch["lhs"], ch["k"], "nt") for ch in chains]
        kks = [g[:c] * ch["decay"] for g, ch in zip(gram, chains)]
        qks = [g[c:] * ch["decay"] for g, ch in zip(gram, chains)]
        t_inv = _nilpotent_inverses([jnp.where(ch["strict"], -kk, 0.0) for kk, ch in zip(kks, chains)], eye)
        uw = [_dot16(t, ch["rhs"]) for t, ch in zip(t_inv, chains)]
        quw = [_dot16(qk, x) for qk, x in zip(qks, uw)]
        kuw = [_dot16(ch["k_tail"], x, "tn") for ch, x in zip(chains, uw)]
        states = [s_ref[ch["d"] * DN_HEADS + ch["h"]] for ch in chains]
        outs = [_dot16(ch["q_dec"] - x[:, DN_DV:], s) + x[:, :DN_DV] for ch, x, s in zip(chains, quw, states)]
        news = [s * ch["e_tot"] - _dot16(x[:, DN_DV:], s) + x[:, :DN_DV] for ch, x, s in zip(chains, kuw, states)]
        for ch, o, s_new in zip(chains, outs, news):
            ch["o_ref"][0, ch["rows"], ch["lanes"]] = o.astype(ch["o_ref"].dtype)
            s_ref[ch["d"] * DN_HEADS + ch["h"]] = s_new
        return carry

    lax.fori_loop(0, n_sub, chunk, 0)


def _dn_scan(q, k, v, gb):
    b, l, _ = q.shape
    t = SCAN_TILE
    n = l // t
    fwd = lambda w: pl.BlockSpec((1, t, w), lambda bi, ti: (bi, ti, 0))
    bwd = lambda w: pl.BlockSpec((1, t, w), lambda bi, ti: (bi, n - 1 - ti, 0))
    w = DN_WIDTH
    return pl.pallas_call(
        _dn_scan_kernel, name="dn_scan",
        grid=(b, n),
        in_specs=[fwd(w), fwd(w), fwd(w), fwd(LANE), bwd(w), bwd(w), bwd(w), bwd(LANE)],
        out_specs=[fwd(w), bwd(w)],
        out_shape=[jax.ShapeDtypeStruct((b, l, w), ACT)] * 2,
        scratch_shapes=[pltpu.VMEM((2 * DN_HEADS, DN_DK, DN_DV), F32)],
        compiler_params=_cparams(("parallel", "arbitrary")),
    )(q, k, v, gb, q, k, v, gb)


def _rk_prep_kernel(x_ref, xp_ref, xn_ref, mu_ref, w2_ref, w0_ref, a2_ref, a0_ref, kk_w_ref, ka_ref, seg_ref,
                    r_ref, k_ref, v_ref, kk_ref, a_ref, lw_ref):
    halo_p, cur, halo_n = _load_with_halo(x_ref, xp_ref, xn_ref)
    prev = _shifted(halo_p, cur, halo_n, -1)
    nxt = _shifted(halo_p, cur, halo_n, 1)
    s = cur + mu_ref[0:1, :] * (prev - cur) + mu_ref[1:2, :] * (nxt - cur)
    w = RK_WIDTH
    r = s[:, 0:w]
    k = s[:, w:2 * w]
    v = s[:, 2 * w:3 * w]
    wd = s[:, 3 * w:3 * w + 2 * RK_LORA]
    ad = s[:, 3 * w + 2 * RK_LORA:3 * w + 4 * RK_LORA]
    lora_w = _dot16(jnp.tanh(wd), w2_ref[...])
    w_log = -_softplus(-(w0_ref[...] + lora_w)) - 0.5
    lw_ref[0] = -jnp.exp(w_log)
    a = _sigmoid(a0_ref[...] + _dot16(ad, a2_ref[...]))
    kk_raw = k * kk_w_ref[...]
    ssq = _dot_exact_rhs(kk_raw * kk_raw, seg_ref[...])
    kk_ref[0] = (kk_raw * lax.rsqrt(ssq + RMS_EPS)).astype(kk_ref.dtype)
    r_ref[0] = r.astype(r_ref.dtype)
    k_ref[0] = (k * (1.0 + (a - 1.0) * ka_ref[...])).astype(k_ref.dtype)
    v_ref[0] = v.astype(v_ref.dtype)
    a_ref[0] = a.astype(a_ref.dtype)


def _seg_ones(width, group):
    i = np.arange(width) // group
    return (i[:, None] == i[None, :]).astype(np.float32)


def _rk_prep(rk, mu, w0, w2, a0, a2, k_k, k_a):
    b, l, wp = rk.shape
    t = ROW_TILE
    w = RK_WIDTH
    mu_p = jnp.zeros((2, wp), F32).at[:, :RK_SHIFT].set(mu.astype(F32))
    w2cat = jnp.zeros((2 * RK_LORA, 2 * w), F32)
    w2cat = w2cat.at[:RK_LORA, :w].set(w2[0]).at[RK_LORA:, w:].set(w2[1]).astype(BF16)
    w0cat = w0.astype(F32).reshape(1, 2 * w)
    a2p = jnp.zeros((2 * RK_LORA, w), F32).at[:RK_LORA].set(a2).astype(BF16)
    prev, nxt = _halo_specs(t, wp, 0, l)
    blk = lambda width: pl.BlockSpec((1, t, width), lambda bi, ti: (bi, ti, 0))
    return pl.pallas_call(
        _rk_prep_kernel, name="rk_prep",
        grid=(b, l // t),
        in_specs=[blk(wp), prev, nxt, _full((2, wp)), _full((2 * RK_LORA, 2 * w)), _full((1, 2 * w)),
                  _full((2 * RK_LORA, w)), _full((1, w)), _full((1, w)), _full((1, w)), _full((w, w))],
        out_specs=[blk(w)] * 5 + [blk(2 * w)],
        out_shape=[jax.ShapeDtypeStruct((b, l, w), ACT)] * 5 + [jax.ShapeDtypeStruct((b, l, 2 * w), F32)],
        compiler_params=_cparams(("parallel", "parallel")),
    )(rk, rk, rk, mu_p, w2cat, w0cat, a2p, a0.astype(F32).reshape(1, w), k_k.astype(F32).reshape(1, w),
      k_a.astype(F32).reshape(1, w), jnp.asarray(_seg_ones(w, RK_HEAD), dtype=BF16))


def _rk_scan_kernel(rf_ref, kf_ref, vf_ref, kkf_ref, af_ref, lwf_ref, rb_ref, kb_ref, vb_ref, kkb_ref, ab_ref,
                    lwb_ref, yf_ref, yb_ref, s_ref):
    c = RK_CHUNK
    n_sub = rf_ref.shape[1] // c
    hd = RK_HEAD

    @pl.when(pl.program_id(1) == 0)
    def _():
        s_ref[...] = jnp.zeros_like(s_ref)

    eye = (lax.broadcasted_iota(jnp.int32, (c, c), 0) == lax.broadcasted_iota(jnp.int32, (c, c), 1)).astype(F32)
    masks = (_tri_masks(c, False), _tri_masks(c, True))
    sides = ((rf_ref, kf_ref, vf_ref, kkf_ref, af_ref, lwf_ref, yf_ref),
             (rb_ref, kb_ref, vb_ref, kkb_ref, ab_ref, lwb_ref, yb_ref))

    def chunk(j, carry):
        chains = []
        for d, (r_ref, k_ref, v_ref, kk_ref, a_ref, lw_ref, y_ref) in enumerate(sides):
            jj = (n_sub - 1 - j) if d else j
            rows = pl.ds(pl.multiple_of(jj * c, c), c)
            incl, strict, _ = masks[d]
            tri16 = incl.astype(BF16)
            lw = lw_ref[0, rows, :]
            l1 = lw.astype(BF16)
            rem = lw - l1.astype(F32)
            l2 = rem.astype(BF16)
            l3 = (rem - l2.astype(F32)).astype(BF16)
            cum = (jnp.dot(tri16, l1, preferred_element_type=F32) + jnp.dot(tri16, l2, preferred_element_type=F32)
                   + jnp.dot(tri16, l3, preferred_element_type=F32))
            tot = jnp.sum(lw, axis=0, keepdims=True)
            e_neg = jnp.exp(-cum)
            e_tail = jnp.exp(tot - cum)
            e_tot = jnp.exp(tot)
            k = k_ref[0, rows, :].astype(F32)
            v = v_ref[0, rows, :].astype(F32)
            kk = kk_ref[0, rows, :].astype(F32)
            b_vec = kk * a_ref[0, rows, :].astype(F32)
            ra = r_ref[0, rows, :].astype(F32) * jnp.exp(cum)
            aa = -kk * jnp.exp(cum - lw)
            bb = b_vec * e_neg
            kb = k * e_neg
            bt = b_vec * e_tail
            kt = k * e_tail
            for h in range(RK_HEADS):
                lanes = slice(h * hd, (h + 1) * hd)
                chains.append(dict(
                    d=d, h=h, rows=rows, lanes=lanes, y_ref=y_ref, incl=incl, strict=strict,
                    aa=aa[:, lanes], ra=ra[:, lanes], bb=bb[:, lanes], kb=kb[:, lanes], v=v[:, lanes],
                    bt=bt[:, lanes], kt=kt[:, lanes], e_tot=e_tot[:, lanes]))
        for ch in chains:
            ch["lhs"] = jnp.concatenate([ch["aa"], ch["ra"]], axis=0)
        gb = [_dot16(ch["lhs"], ch["bb"], "nt") for ch in chains]
        gk = [_dot16(ch["lhs"], ch["kb"], "nt") for ch in chains]
        a_ab = [jnp.where(ch["strict"], g[:c], 0.0) for g, ch in zip(gb, chains)]
        m_rb = [jnp.where(ch["incl"], g[c:], 0.0) for g, ch in zip(gb, chains)]
        a_ak = [jnp.where(ch["strict"], g[:c], 0.0) for g, ch in zip(gk, chains)]
        m_rk = [jnp.where(ch["incl"], g[c:], 0.0) for g, ch in zip(gk, chains)]
        av = [_dot16(m, ch["v"]) for m, ch in zip(a_ak, chains)]
        yv = [_dot16(m, ch["v"]) for m, ch in zip(m_rk, chains)]
        t_inv = _nilpotent_inverses(a_ab, eye)
        ta = [_dot16(t, ch["aa"]) for t, ch in zip(t_inv, chains)]
        qm = [_dot16(t, x) for t, x in zip(t_inv, av)]
        ya = [ch["ra"] + _dot16(m, x) for ch, m, x in zip(chains, m_rb, ta)]
        yb = [_dot16(m, x) + y for m, x, y in zip(m_rb, qm, yv)]
        wm = [_dot16(x, ch["bt"], "tn") for x, ch in zip(ta, chains)]
        hc = [_dot16(jnp.concatenate([x, ch["v"]], axis=0), jnp.concatenate([ch["bt"], ch["kt"]], axis=0), "tn")
              for x, ch in zip(qm, chains)]
        states = [s_ref[ch["d"] * RK_HEADS + ch["h"]] for ch in chains]
        outs = [_dot16(a, s, "nt") + b for a, b, s in zip(ya, yb, states)]
        news = [s * ch["e_tot"] + _dot16(s, w) + x for s, ch, w, x in zip(states, chains, wm, hc)]
        for ch, s_new in zip(chains, news):
            s_ref[ch["d"] * RK_HEADS + ch["h"]] = s_new
        for d in range(2):
            mine = [(ch, y) for ch, y in zip(chains, outs) if ch["d"] == d]
            y_all = jnp.concatenate([y for _, y in mine], axis=1)
            ch0 = mine[0][0]
            ch0["y_ref"][0, ch0["rows"], :] = y_all.astype(ch0["y_ref"].dtype)
        return carry

    lax.fori_loop(0, n_sub, chunk, 0)


def _rk_scan(r, k, v, kk, a, lw):
    b, l, w = r.shape
    t = ROW_TILE
    n = l // t
    fwd = pl.BlockSpec((1, t, w), lambda bi, ti: (bi, ti, 0))
    bwd = pl.BlockSpec((1, t, w), lambda bi, ti: (bi, n - 1 - ti, 0))
    bwd_lw = pl.BlockSpec((1, t, w), lambda bi, ti: (bi, n - 1 - ti, 1))
    return pl.pallas_call(
        _rk_scan_kernel, name="rk_scan",
        grid=(b, n),
        in_specs=[fwd] * 6 + [bwd] * 5 + [bwd_lw],
        out_specs=[fwd, bwd],
        out_shape=[jax.ShapeDtypeStruct((b, l, w), ACT)] * 2,
        scratch_shapes=[pltpu.VMEM((2 * RK_HEADS, RK_HEAD, RK_HEAD), F32)],
        compiler_params=_cparams(("parallel", "arbitrary")),
    )(r, k, v, kk, a, lw, r, k, v, kk, a, lw)


def _even_mix_kernel(of_ref, ob_ref, dg_ref, dnw_ref, yf_ref, yb_ref, r_ref, k_ref, v_ref, rg_ref,
                     rk_ref, lnw_ref, lnb_ref, segm_ref, seg1_ref, o_ref):
    f32 = lambda ref: ref[0].astype(F32)
    o = f32(of_ref) + f32(ob_ref)
    gate = _silu(f32(dg_ref))
    for h in range(DN_HEADS):
        lanes = slice(h * DN_DV, (h + 1) * DN_DV)
        oh = o[:, lanes]
        ms = jnp.mean(oh * oh, axis=-1, keepdims=True)
        o_ref[0, :, lanes] = (oh * lax.rsqrt(ms + RMS_EPS) * dnw_ref[...] * gate[:, lanes]).astype(o_ref.dtype)
    wkv = f32(yf_ref) + f32(yb_ref)
    mean = _dot_exact_rhs(wkv, segm_ref[...])
    cen = wkv - mean
    var = _dot_exact_rhs(cen * cen, segm_ref[...])
    wkv = cen * lax.rsqrt(var + RK_GN_EPS) * lnw_ref[...] + lnb_ref[...]
    bonus = _dot_exact_rhs(f32(r_ref) * f32(k_ref) * rk_ref[...], seg1_ref[...]) * f32(v_ref)
    o_ref[0, :, DN_WIDTH:] = ((wkv + bonus) * _silu(f32(rg_ref))).astype(o_ref.dtype)


def _even_mix(o_f, o_b, dn_gate, dn_norm, y_f, y_b, r, k, v, rk_gate, r_k, ln_w, ln_b):
    b, l, _ = o_f.shape
    t = ROW_TILE
    w = RK_WIDTH
    blk = lambda width: pl.BlockSpec((1, t, width), lambda bi, ti: (bi, ti, 0))
    seg1 = jnp.asarray(_seg_ones(w, RK_HEAD), dtype=BF16)
    segm = jnp.asarray(_seg_ones(w, RK_HEAD) / RK_HEAD, dtype=BF16)
    return pl.pallas_call(
        _even_mix_kernel, name="even_mix",
        grid=(b, l // t),
        in_specs=[blk(DN_WIDTH), blk(DN_WIDTH), blk(DN_WIDTH), _full((1, DN_DV)),
                  blk(w), blk(w), blk(w), blk(w), blk(w), blk(w),
                  _full((1, w)), _full((1, w)), _full((1, w)), _full((w, w)), _full((w, w))],
        out_specs=blk(DN_WIDTH + w),
        out_shape=jax.ShapeDtypeStruct((b, l, DN_WIDTH + w), ACT),
        compiler_params=_cparams(("parallel", "parallel")),
    )(o_f, o_b, dn_gate, dn_norm.astype(F32).reshape(1, DN_DV), y_f, y_b, r, k, v, rk_gate,
      r_k.astype(F32).reshape(1, w), ln_w.astype(F32).reshape(1, w), ln_b.astype(F32).reshape(1, w), segm, seg1)


def _even_layer(h, w_in, dn_conv, dn_a_log, dn_dt_bias, dn_norm, rk_mu, rk_w0, rk_w2, rk_a0, rk_a2,
                rk_k_k, rk_k_a, rk_r_k, rk_ln_w, rk_ln_b):
    b, l, d = h.shape
    s0 = DN_QKV
    s1 = s0 + DN_AB
    s2 = s1 + DN_WIDTH
    s3 = s2 + RK_SHIFT
    pad = lambda m, width: jnp.pad(m, ((0, 0), (0, width - m.shape[1])))
    widths = (DN_QKV, LANE, DN_WIDTH, RK_SHIFT_PAD, RK_WIDTH)
    w16 = jnp.concatenate([w_in[:, :s0], pad(w_in[:, s0:s1], LANE), w_in[:, s1:s2],
                           pad(w_in[:, s2:s3], RK_SHIFT_PAD), w_in[:, s3:]], axis=1).astype(BF16)
    qkv, ab, dn_gate, rk, rk_gate = _project(h.reshape(b * l, d), w16, widths, (ACT, F32, ACT, ACT, ACT))
    r3 = lambda m: m.reshape(b, l, m.shape[-1])
    q, k, v, gb = _dn_prep(r3(qkv), r3(ab), dn_conv, dn_a_log, dn_dt_bias)
    o_f, o_b = _dn_scan(q, k, v, gb)
    r, kr, vr, kk, a, lw = _rk_prep(r3(rk), rk_mu, rk_w0, rk_w2, rk_a0, rk_a2, rk_k_k, rk_k_a)
    y_f, y_b = _rk_scan(r, kr, vr, kk, a, lw)
    return _even_mix(o_f, o_b, r3(dn_gate), dn_norm, y_f, y_b, r, kr, vr, r3(rk_gate),
                     rk_r_k, rk_ln_w, rk_ln_b)


def _dft_geometry(l):
    nf = 2 * l
    p = nf // DFT_Q
    n1 = p // 2
    k1 = p // 2 + 1
    k1p = -(-k1 // SUBLANE) * SUBLANE
    return nf, p, n1, k1, k1p


@functools.lru_cache(maxsize=None)
def _dft_tables(l):
    nf, p, n1c, k1c, k1p = _dft_geometry(l)
    q = DFT_Q
    n2 = np.arange(q)[:, None, None]
    k1 = np.arange(k1c)[None, :, None]
    n1 = np.arange(n1c)[None, None, :]
    ph = -2.0 * np.pi * (((n1 * k1) % p) / p + ((n2 * k1) % nf) / nf)
    fa = np.zeros((q, 2 * k1p, n1c))
    fa[:, :k1c] = np.cos(ph)
    fa[:, k1p:k1p + k1c] = np.sin(ph)
    wgt = np.full((k1c,), 2.0)
    wgt[0] = 1.0
    wgt[-1] = 1.0
    th = -ph.transpose(0, 2, 1)
    gd = np.zeros((q, n1c, 2 * k1p))
    gd[:, :, :k1c] = np.cos(th) * wgt / nf
    gd[:, :, k1p:k1p + k1c] = -np.sin(th) * wgt / nf
    a = np.arange(q)
    ang = -2.0 * np.pi * ((a[:, None] * a[None, :]) % q) / q
    cr, ci = np.cos(ang), np.sin(ang)
    fb = np.block([[cr, -ci], [ci, cr]])
    fc = np.block([[cr, ci], [-ci, cr]])

    return tuple(m.astype(np.float32).astype(BF16) for m in (fa, fb, fc, gd))


def _fdot(f, x):
    return jnp.dot(f, x.astype(BF16), preferred_element_type=F32)


DFT_GROUP_N2 = 8
DFT_GROUP_K1 = 8
DFT_PITCH = DFT_Q + SUBLANE


def _stage_a(src, y_re, y_im, fa, geo):
    nf, p, n1c, k1c, k1p = geo
    g = DFT_GROUP_N2

    def body(i, carry):
        n2s = [i * g + t for t in range(g)]
        slabs = [src[pl.ds(n2, n1c, stride=DFT_PITCH), :] for n2 in n2s]
        outs = [_fdot(fa[n2], slab) for n2, slab in zip(n2s, slabs)]
        for n2, out in zip(n2s, outs):
            y_re[pl.ds(n2, k1p, stride=DFT_PITCH), :] = out[:k1p]
            y_im[pl.ds(n2, k1p, stride=DFT_PITCH), :] = out[k1p:]
        return carry

    lax.fori_loop(0, DFT_Q // g, body, 0)


def _fdot_pairs(f, xs):
    outs = []
    for a, b in zip(xs[0::2], xs[1::2]):
        z = _fdot(f, jnp.concatenate([a, b], axis=1))
        outs += [z[:, :a.shape[1]], z[:, a.shape[1]:]]
    return outs


def _stage_b(y_re, y_im, i, fb):
    g = DFT_GROUP_K1
    k1s = [i * g + t for t in range(g)]
    rows = [pl.ds(pl.multiple_of(k1 * DFT_PITCH, SUBLANE), DFT_Q) for k1 in k1s]
    ws = [jnp.concatenate([y_re[r, :], y_im[r, :]], axis=0) for r in rows]
    return k1s, rows, _fdot_pairs(fb[...], ws)


def _hy_conv_kernel(u_ref, m_ref, skip_ref, hr_ref, hi_ref, fa, fb, fc, gd, o_ref, pad, y_re, y_im, *, geo):
    nf, p, n1c, k1c, k1p = geo
    for n1 in range(n1c):
        pad[n1 * DFT_PITCH:n1 * DFT_PITCH + DFT_Q, :] = u_ref[0, n1 * DFT_Q:(n1 + 1) * DFT_Q, :].astype(F32)
    _stage_a(pad, y_re, y_im, fa, geo)

    def mid(i, carry):
        k1s, rows, zs = _stage_b(y_re, y_im, i, fb)
        prods = []
        for k1, z in zip(k1s, zs):
            zr, zi = z[:DFT_Q], z[DFT_Q:]
            hrows = pl.ds(pl.multiple_of(k1 * DFT_Q, DFT_Q), DFT_Q)
            hr = hr_ref[hrows, :]
            hi = hi_ref[hrows, :]
            prods.append(jnp.concatenate([zr * hr - zi * hi, zr * hi + zi * hr], axis=0))
        outs = _fdot_pairs(fc[...], prods)
        for r, a in zip(rows, outs):
            y_re[r, :] = a[:DFT_Q]
            y_im[r, :] = a[DFT_Q:]
        return carry

    lax.fori_loop(0, k1p // DFT_GROUP_K1, mid, 0)

    def last(i, carry):
        n2s = [i * DFT_GROUP_N2 + t for t in range(DFT_GROUP_N2)]
        ins = [jnp.concatenate([y_re[pl.ds(n2, k1p, stride=DFT_PITCH), :], y_im[pl.ds(n2, k1p, stride=DFT_PITCH), :]],
                               axis=0) for n2 in n2s]
        outs = [_fdot(gd[n2], a) for n2, a in zip(n2s, ins)]
        for n2, out in zip(n2s, outs):
            pad[pl.ds(n2, n1c, stride=DFT_PITCH), :] = out
        return carry

    lax.fori_loop(0, DFT_Q // DFT_GROUP_N2, last, 0)

    skip = skip_ref[...]
    for n1 in range(n1c):
        rows = slice(n1 * DFT_Q, (n1 + 1) * DFT_Q)
        conv = pad[n1 * DFT_PITCH:n1 * DFT_PITCH + DFT_Q, :]
        o_ref[0, rows, :] = (m_ref[0, rows, :].astype(F32)
                             * (conv + skip * u_ref[0, rows, :].astype(F32))).astype(o_ref.dtype)


def _single(shape, index_map):
    return pl.BlockSpec(shape, index_map, pipeline_mode=pl.Buffered(1))


def _hy_conv(u, mult, skip, h_re, h_im, tables):
    b, l, ch = u.shape
    geo = _dft_geometry(l)
    nf, p, n1c, k1c, k1p = geo
    ct = HY_CT
    consts = [jnp.asarray(t) for t in tables]
    seq = pl.BlockSpec((1, l, ct), lambda ci, bi: (bi, 0, ci))
    spec = _single((k1p * DFT_Q, ct), lambda ci, bi: (0, ci))
    cspecs = [_single(c.shape, (lambda ci, bi, nd=c.ndim: (0,) * nd)) for c in consts]
    ysc = pltpu.VMEM((k1p * DFT_PITCH, ct), F32)
    return pl.pallas_call(
        functools.partial(_hy_conv_kernel, geo=geo), name="hy_conv",
        grid=(ch // ct, b),
        in_specs=[seq, seq, pl.BlockSpec((1, ct), lambda ci, bi: (0, ci)), spec, spec] + cspecs,
        out_specs=seq,
        out_shape=jax.ShapeDtypeStruct((b, l, ch), ACT),
        scratch_shapes=[pltpu.VMEM((n1c * DFT_PITCH, ct), F32), ysc, ysc],
        compiler_params=_cparams(("parallel", "parallel")),
    )(u, mult, skip, h_re, h_im, *consts)


def _hy_mlp_kernel(f_ref, w1_ref, b1_ref, w2_ref, b2_ref, w3_ref, b3_ref, fr_ref, o_ref):
    fr = fr_ref[...]
    hdn = jnp.sin(fr * (_dot32(f_ref[...], w1_ref[...]) + b1_ref[...]))
    hdn = jnp.sin(fr * (_dot32(hdn, w2_ref[...]) + b2_ref[...]))
    o_ref[...] = jnp.sin(fr * (_dot32(hdn, w3_ref[...]) + b3_ref[...]))


def _hy_mlp(feats, w1, b1, w2, b2, w3, b3, freq):
    l = feats.shape[0]
    t = min(l, 1024)
    fw = HY_FW
    row = lambda a: a.astype(F32).reshape(1, fw)
    w1p = jnp.zeros((LANE, fw), F32).at[:HY_EMB].set(w1.astype(F32))
    return pl.pallas_call(
        _hy_mlp_kernel, name="hy_mlp",
        grid=(l // t,),
        in_specs=[pl.BlockSpec((t, LANE), lambda i: (i, 0)), _full((LANE, fw)), _full((1, fw)), _full((fw, fw)),
                  _full((1, fw)), _full((fw, fw)), _full((1, fw)), _full((1, fw))],
        out_specs=pl.BlockSpec((t, fw), lambda i: (i, 0)),
        out_shape=jax.ShapeDtypeStruct((l, fw), F32),
        compiler_params=_cparams(("parallel",)),
    )(feats, w1p, row(b1), w2.astype(F32), row(b2), w3.astype(F32), row(b3), row(freq))


def _hy_filter_kernel(hdn_ref, wf_ref, wb_ref, df_ref, db_ref, fa, fb,
                      hr_ref, hi_ref, filt, yr_f, yi_f, yr_b, yi_b, *, geo):
    nf, p, n1c, k1c, k1p = geo

    def build(w_ref, d_ref, drop_first):
        grp = 4

        def body(i, acc):
            n1s = [i * grp + t for t in range(grp)]
            hxs = [hdn_ref[pl.ds(pl.multiple_of(n1 * DFT_Q, DFT_Q), DFT_Q), :] for n1 in n1s]
            raw = [_dot32(hx, w_ref[0]) for hx in hxs]
            for n1, hx, hv in zip(n1s, hxs, raw):
                hv = hv * jnp.exp(-hx[:, HY_FW:HY_FW + 1] * jnp.abs(d_ref[0]))
                if drop_first:
                    pos = lax.broadcasted_iota(jnp.int32, hv.shape, 0) + n1 * DFT_Q
                    hv = jnp.where(pos == 0, 0.0, hv)
                filt[pl.ds(pl.multiple_of(n1 * DFT_PITCH, SUBLANE), DFT_Q), :] = hv
                acc = acc + jnp.sum(jnp.abs(hv), axis=0, keepdims=True)
            return acc

        return lax.fori_loop(0, n1c // grp, body, jnp.zeros((1, filt.shape[1]), F32))

    l1 = build(wf_ref, df_ref, False)
    _stage_a(filt, yr_f, yi_f, fa, geo)
    l1 = l1 + build(wb_ref, db_ref, True)
    _stage_a(filt, yr_b, yi_b, fa, geo)
    inv = 1.0 / (l1 + RMS_EPS)

    def mid(i, carry):
        k1s, _, zf = _stage_b(yr_f, yi_f, i, fb)
        _, _, zb = _stage_b(yr_b, yi_b, i, fb)
        for k1, f, b in zip(k1s, zf, zb):
            r = pl.ds(pl.multiple_of(k1 * DFT_Q, DFT_Q), DFT_Q)
            hr_ref[0, r, :] = (f[:DFT_Q] + b[:DFT_Q]) * inv
            hi_ref[0, r, :] = (f[DFT_Q:] - b[DFT_Q:]) * inv
        return carry

    lax.fori_loop(0, k1p // DFT_GROUP_K1, mid, 0)


def _hy_filters(hdn, tcol, w_out, deltas, tables, l):
    geo = _dft_geometry(l)
    nf, p, n1c, k1c, k1p = geo
    ch = w_out.shape[1] // (2 * HY_ORDER)
    ct = HY_CT
    nct = ch // ct
    hdn_x = jnp.concatenate([hdn, tcol, jnp.zeros((l, LANE - HY_FW - 1), F32)], axis=1)
    w4 = w_out.astype(F32).reshape(HY_FW, 2 * HY_ORDER, ch).transpose(1, 0, 2)
    w4 = jnp.pad(w4, ((0, 0), (0, LANE - HY_FW), (0, 0)))
    d4 = deltas.astype(F32).reshape(2 * HY_ORDER, 1, ch)
    consts = [jnp.asarray(t) for t in tables[:2]]
    cspecs = [_single(c.shape, (lambda o, ci, nd=c.ndim: (0,) * nd)) for c in consts]
    wspec = lambda d: pl.BlockSpec((1, LANE, ct), lambda o, ci: (2 * o + d, 0, ci))
    dspec = lambda d: pl.BlockSpec((1, 1, ct), lambda o, ci: (2 * o + d, 0, ci))
    ospec = pl.BlockSpec((1, k1p * DFT_Q, ct), lambda o, ci: (o, 0, ci))
    ysc = pltpu.VMEM((k1p * DFT_PITCH, ct), F32)
    return pl.pallas_call(
        functools.partial(_hy_filter_kernel, geo=geo), name="hy_filters",
        grid=(HY_ORDER, nct),
        in_specs=[_single((l, LANE), lambda o, ci: (0, 0)), wspec(0), wspec(1), dspec(0), dspec(1)] + cspecs,
        out_specs=[ospec, ospec],
        out_shape=[jax.ShapeDtypeStruct((HY_ORDER, k1p * DFT_Q, ch), F32)] * 2,
        scratch_shapes=[pltpu.VMEM((n1c * DFT_PITCH, ct), F32), ysc, ysc, ysc, ysc],
        compiler_params=_cparams(("parallel", "parallel")),
    )(hdn_x, w4, w4, d4, d4, *consts)


def _position_features(l):
    bands = (HY_EMB - 1) // 2
    t = jnp.linspace(0.0, 1.0, l, dtype=F32)[:, None]
    f = jnp.linspace(1e-4, bands - 1, bands, dtype=F32)[None, :]
    ang = (2.0 * math.pi / l) * jnp.arange(l, dtype=F32)[:, None] * f
    feats = jnp.concatenate([t, jnp.cos(ang), -jnp.sin(ang)], axis=-1)
    return jnp.pad(feats, ((0, 0), (0, LANE - HY_EMB))), t


def _hy_prep_kernel(x_ref, xp_ref, xn_ref, g_ref, cw_ref, cb_ref, x1_ref, m2_ref, v_ref):
    prev, cur, nxt = _load_with_halo(x_ref, xp_ref, xn_ref)
    y = (_shifted(prev, cur, nxt, -1) * cw_ref[0:1, :] + cur * cw_ref[1:2, :]
         + _shifted(prev, cur, nxt, 1) * cw_ref[2:3, :] + cb_ref[...])
    c = g_ref.shape[2]
    x1_ref[0] = y[:, :c].astype(x1_ref.dtype)
    m2_ref[0] = (y[:, c:2 * c] * _silu(g_ref[0].astype(F32))).astype(m2_ref.dtype)
    v_ref[0] = y[:, 2 * c:].astype(v_ref.dtype)


def _hy_prep(xv, gate, conv_w, conv_b):
    b, l, w3 = xv.shape
    c = gate.shape[2]
    t = ROW_TILE
    prev, nxt = _halo_specs(t, w3, 0, l)
    blk = lambda width: pl.BlockSpec((1, t, width), lambda bi, ti: (bi, ti, 0))
    return pl.pallas_call(
        _hy_prep_kernel, name="hy_prep",
        grid=(b, l // t),
        in_specs=[blk(w3), prev, nxt, blk(c), _full((HY_SHORT, w3)), _full((1, w3))],
        out_specs=[blk(c)] * 3,
        out_shape=[jax.ShapeDtypeStruct((b, l, c), ACT)] * 3,
        compiler_params=_cparams(("parallel", "parallel")),
    )(xv, xv, xv, gate, conv_w.astype(F32), conv_b.astype(F32).reshape(1, w3))


def _odd_layer(h, w_in, conv_w, conv_b, f_w1, f_b1, f_w2, f_b2, f_w3, f_b3, f_freq, f_out, deltas, skip):
    b, l, d = h.shape
    c = skip.shape[1]
    xv, gate = _project(h.reshape(b * l, d), w_in.astype(BF16), (3 * c, c), (ACT, ACT))
    x1, m2, v = _hy_prep(xv.reshape(b, l, 3 * c), gate.reshape(b, l, c), conv_w, conv_b)
    tables = _dft_tables(l)
    feats, tcol = _position_features(l)
    hdn = _hy_mlp(feats, f_w1, f_b1, f_w2, f_b2, f_w3, f_b3, f_freq)
    h_re, h_im = _hy_filters(hdn, tcol, f_out, deltas, tables, l)
    skip = skip.astype(F32)
    z = _hy_conv(v, x1, skip[0:1], h_re[0], h_im[0], tables)
    return _hy_conv(z, m2, skip[1:2], h_re[1], h_im[1], tables)


def kernel(x, p, even_w_in, dn_conv, dn_a_log, dn_dt_bias, dn_norm, rk_mu, rk_w0, rk_w2, rk_a0, rk_a2, rk_k_k, rk_k_a, rk_r_k, rk_ln_w, rk_ln_b, odd_w_in, hy_conv_w, hy_conv_b, hy_ffn_w1, hy_ffn_b1, hy_ffn_w2, hy_ffn_b2, hy_ffn_w3, hy_ffn_b3, hy_ffn_freq, hy_ffn_out, hy_deltas, hy_skip, w_out, ln_g, ln_b, ple_w, ple_norm, ple_gate):
    b, l, d = x.shape
    depth = p.shape[0]
    alpha = (2.0 * depth) ** 0.25
    h = x
    for i in range(depth):
        j = i // 2
        if i % 2 == 0:
            mix = _even_layer(h, even_w_in[j], dn_conv[j], dn_a_log[j], dn_dt_bias[j], dn_norm[j], rk_mu[j],
                              rk_w0[j], rk_w2[j], rk_a0[j], rk_a2[j], rk_k_k[j].reshape(-1), rk_k_a[j].reshape(-1),
                              rk_r_k[j].reshape(-1), rk_ln_w[j], rk_ln_b[j])
        else:
            mix = _odd_layer(h, odd_w_in[j], hy_conv_w[j], hy_conv_b[j], hy_ffn_w1[j], hy_ffn_b1[j],
                             hy_ffn_w2[j], hy_ffn_b2[j], hy_ffn_w3[j], hy_ffn_b3[j], hy_ffn_freq[j],
                             hy_ffn_out[j], hy_deltas[j], hy_skip[j])
        h2 = _post_layer(h.reshape(b * l, d), mix.reshape(b * l, mix.shape[-1]), p[i].reshape(b * l, p.shape[-1]),
                         w_out[i], ple_w[i], ple_gate[i], ln_g[i], ln_b[i], ple_norm[i], alpha)
        h = h2.reshape(b, l, d)
    return h
```

```python
import functools
import math

import numpy as np
import jax
import jax.numpy as jnp
from jax import lax
from jax.experimental import pallas as pl
from jax.experimental.pallas import tpu as pltpu

F32 = jnp.float32
BF16 = jnp.bfloat16

LN_EPS = 1e-5
RMS_EPS = 1e-6

DN_HEADS = 4
DN_DK = 128
DN_DV = 128
DN_WIDTH = DN_HEADS * DN_DV
DN_QKV = 2 * DN_HEADS * DN_DK + DN_WIDTH
DN_AB = 4 * DN_HEADS
DN_CONV = 5
DN_CHUNK = 64
DN_UNROLL = 4

RK_HEADS = 8
RK_HEAD = 64
RK_WIDTH = RK_HEADS * RK_HEAD
RK_LORA = 64
RK_SHIFT = 3 * RK_WIDTH + 3 * RK_LORA
RK_SHIFT_PAD = 1792
RK_GN_EPS = 64e-5
RK_CHUNK = 64
RK_UNROLL = 2

HY_ORDER = 2
HY_SHORT = 3
HY_EMB = 33
HY_FW = 64

LANE = 128
SUBLANE = 8
DFT_Q = 128
VMEM_LIMIT = 56 * 1024 * 1024

ROW_TILE = 256
SCAN_TILE = 512
HY_CT = 128

HI = lax.Precision.HIGHEST
ACT = BF16


def _cparams(sem):
    return pltpu.CompilerParams(dimension_semantics=sem, vmem_limit_bytes=VMEM_LIMIT)


_DIMS = {
    "nn": (((1,), (0,)), ((), ())),
    "nt": (((1,), (1,)), ((), ())),
    "tn": (((0,), (0,)), ((), ())),
}


def _dot16(a, b, dims="nn"):
    return lax.dot_general(a.astype(BF16), b.astype(BF16), _DIMS[dims], preferred_element_type=F32)


def _dot32(a, b, dims="nn"):
    return lax.dot_general(a.astype(F32), b.astype(F32), _DIMS[dims], precision=HI,
                           preferred_element_type=F32)


def _split2(x):
    hi = x.astype(BF16)
    lo = (x - hi.astype(F32)).astype(BF16)
    return hi, lo


def _dot_exact_rhs(x, m16):
    hi, lo = _split2(x)
    return (jnp.dot(hi, m16, preferred_element_type=F32) + jnp.dot(lo, m16, preferred_element_type=F32))


def _sigmoid(x):
    return 1.0 / (1.0 + jnp.exp(-x))


def _silu(x):
    return x * _sigmoid(x)


def _softplus(x):
    return jnp.maximum(x, 0.0) + jnp.log1p(jnp.exp(-jnp.abs(x)))


HALO = 16


def _shifted(prev, cur, nxt, d):
    t = cur.shape[0]
    ext = jnp.concatenate([prev, cur, nxt], axis=0)
    return ext[HALO + d:HALO + d + t]


def _halo_specs(t_rows, width, col, l_total):
    nb = t_rows // HALO
    last = l_total // HALO - 1
    prev = pl.BlockSpec((1, HALO, width), lambda b, t: (b, jnp.maximum(t * nb - 1, 0), col))
    nxt = pl.BlockSpec((1, HALO, width), lambda b, t: (b, jnp.minimum((t + 1) * nb, last), col))
    return prev, nxt


def _load_with_halo(x_ref, xp_ref, xn_ref):
    t_idx = pl.program_id(1)
    cur = x_ref[0].astype(F32)
    prev = jnp.where(t_idx > 0, xp_ref[0].astype(F32), 0.0)
    nxt = jnp.where(t_idx < pl.num_programs(1) - 1, xn_ref[0].astype(F32), 0.0)
    return prev, cur, nxt


def _full(shape):
    nd = len(shape)
    return pl.BlockSpec(shape, lambda *_: (0,) * nd)


def _proj_kernel(a_ref, w_ref, *o_refs, offs):
    a = a_ref[...].astype(BF16)
    for o_ref, (lo, hi) in zip(o_refs, offs):
        o_ref[...] = jnp.dot(a, w_ref[:, lo:hi], preferred_element_type=F32).astype(o_ref.dtype)


def _project(a, w16, widths, dtypes):
    m, k = a.shape
    offs, o = [], 0
    for w in widths:
        offs.append((o, o + w))
        o += w
    n = o
    tm = ROW_TILE
    return pl.pallas_call(
        functools.partial(_proj_kernel, offs=tuple(offs)), name="project",
        grid=(m // tm,),
        in_specs=[pl.BlockSpec((tm, k), lambda i: (i, 0)), _full((k, n))],
        out_specs=[pl.BlockSpec((tm, w), lambda i: (i, 0)) for w in widths],
        out_shape=[jax.ShapeDtypeStruct((m, w), dt) for w, dt in zip(widths, dtypes)],
        compiler_params=_cparams(("parallel",)),
    )(a, w16)


def _post_kernel(h_ref, mix_ref, p_ref, wo_ref, pw_ref, pg_ref, lng_ref, lnb_ref, pn_ref, o_ref, *, alpha):
    t = alpha * h_ref[...] + jnp.dot(mix_ref[...], wo_ref[...], preferred_element_type=F32)
    mu = jnp.mean(t, axis=-1, keepdims=True)
    tc = t - mu
    var = jnp.mean(tc * tc, axis=-1, keepdims=True)
    y = tc * lax.rsqrt(var + LN_EPS) * lng_ref[...] + lnb_ref[...]
    e = jnp.dot(p_ref[...].astype(BF16), pw_ref[...], preferred_element_type=F32)
    e = e * lax.rsqrt(jnp.mean(e * e, axis=-1, keepdims=True) + RMS_EPS) * pn_ref[...]
    gate = _sigmoid(jnp.dot(y.astype(BF16), pg_ref[...], preferred_element_type=F32))
    o_ref[...] = y + gate * e


def _post_layer(h, mix, p_all, layer, w_out, ple_w, ple_gate, ln_g, ln_b, ple_norm, alpha):
    m, d = h.shape
    pd = p_all.shape[2]
    tm = ROW_TILE
    row = lambda w: pl.BlockSpec((tm, w), lambda i: (i, 0))
    p_spec = pl.BlockSpec((None, tm, pd), lambda i: (layer, i, 0))
    return pl.pallas_call(
        functools.partial(_post_kernel, alpha=alpha), name="post_layer",
        grid=(m // tm,),
        in_specs=[row(d), row(mix.shape[1]), p_spec, _full(w_out.shape), _full(ple_w.shape),
                  _full(ple_gate.shape), _full((1, d)), _full((1, d)), _full((1, d))],
        out_specs=row(d),
        out_shape=jax.ShapeDtypeStruct((m, d), F32),
        compiler_params=_cparams(("parallel",)),
    )(h, mix, p_all, w_out.astype(BF16), ple_w.astype(BF16), ple_gate.astype(BF16),
      ln_g.reshape(1, d), ln_b.reshape(1, d), ple_norm.reshape(1, d))


def _dn_prep_kernel(x_ref, xp_ref, xn_ref, ab_ref, cw_ref, ga_ref, gbias_ref, q_ref, k_ref, v_ref, gb_ref):
    prev, cur, nxt = _load_with_halo(x_ref, xp_ref, xn_ref)
    pad = DN_CONV // 2
    acc = cur * cw_ref[pad:pad + 1, :]
    for j in range(DN_CONV):
        if j != pad:
            acc = acc + _shifted(prev, cur, nxt, j - pad) * cw_ref[j:j + 1, :]
    y = _silu(acc)
    nqk = DN_HEADS * DN_DK
    for h in range(DN_HEADS):
        qh = y[:, h * DN_DK:(h + 1) * DN_DK]
        kh = y[:, nqk + h * DN_DK:nqk + (h + 1) * DN_DK]
        qn = lax.rsqrt(jnp.sum(qh * qh, axis=-1, keepdims=True) + RMS_EPS) * (DN_DK ** -0.5)
        kn = lax.rsqrt(jnp.sum(kh * kh, axis=-1, keepdims=True) + RMS_EPS)
        q_ref[0, :, h * DN_DK:(h + 1) * DN_DK] = (qh * qn).astype(q_ref.dtype)
        k_ref[0, :, h * DN_DK:(h + 1) * DN_DK] = (kh * kn).astype(k_ref.dtype)
    v_ref[0] = y[:, 2 * nqk:].astype(v_ref.dtype)
    ab = ab_ref[0]
    lane = lax.broadcasted_iota(jnp.int32, ab.shape, 1)
    g = ga_ref[...] * _softplus(ab + gbias_ref[...])
    gb_ref[0] = jnp.where(lane < 2 * DN_HEADS, g, _sigmoid(ab))


def _dn_prep(qkv, ab, conv_w, a_log, dt_bias):
    b, l, _ = qkv.shape
    t = SCAN_TILE
    ga = jnp.zeros((1, LANE), F32).at[0, :2 * DN_HEADS].set(-jnp.exp(a_log.astype(F32)).reshape(-1))
    gbias = jnp.zeros((1, LANE), F32).at[0, :2 * DN_HEADS].set(dt_bias.astype(F32).reshape(-1))
    prev, nxt = _halo_specs(t, DN_QKV, 0, l)
    blk = lambda w: pl.BlockSpec((1, t, w), lambda bi, ti: (bi, ti, 0))
    return pl.pallas_call(
        _dn_prep_kernel, name="dn_prep",
        grid=(b, l // t),
        in_specs=[blk(DN_QKV), prev, nxt, blk(LANE), _full((DN_CONV, DN_QKV)), _full((1, LANE)), _full((1, LANE))],
        out_specs=[blk(DN_WIDTH), blk(DN_WIDTH), blk(DN_WIDTH), blk(LANE)],
        out_shape=[jax.ShapeDtypeStruct((b, l, DN_WIDTH), ACT)] * 3 + [jax.ShapeDtypeStruct((b, l, LANE), F32)],
        compiler_params=_cparams(("parallel", "parallel")),
    )(qkv, qkv, qkv, ab, conv_w.astype(F32), ga, gbias)


def _tri_masks(c, reverse):
    r = lax.broadcasted_iota(jnp.int32, (c, c), 0)
    s = lax.broadcasted_iota(jnp.int32, (c, c), 1)
    if reverse:
        return s >= r, s > r, r >= s
    return s <= r, s < r, r <= s


def _nilpotent_inverses(xs, eye):
    c = xs[0].shape[0]
    rs = [eye + x for x in xs]
    ps = list(xs)
    for _ in range(int(math.log2(c)) - 1):
        ps = [_dot16(p, p) for p in ps]
        rs = [r + _dot16(r, p) for r, p in zip(rs, ps)]
    return rs


def _dn_scan_kernel(qf_ref, kf_ref, vf_ref, gf_ref, qb_ref, kb_ref, vb_ref, gb_ref, of_ref, ob_ref, s_ref):
    c = DN_CHUNK
    n_sub = qf_ref.shape[1] // c

    @pl.when(pl.program_id(1) == 0)
    def _():
        s_ref[...] = jnp.zeros_like(s_ref)

    eye_b = lax.broadcasted_iota(jnp.int32, (c, c), 0) == lax.broadcasted_iota(jnp.int32, (c, c), 1)
    eye = eye_b.astype(F32)
    neg = jnp.float32(-1e30)
    masks = (_tri_masks(c, False), _tri_masks(c, True))
    sides = ((qf_ref, kf_ref, vf_ref, gf_ref, of_ref), (qb_ref, kb_ref, vb_ref, gb_ref, ob_ref))

    def chunk(i, carry):
        chains = []
        for u, d in [(u, d) for u in range(DN_UNROLL) for d in range(2)]:
            q_ref, k_ref, v_ref, g_ref, o_ref = sides[d]
            j = i * DN_UNROLL + u
            jj = (n_sub - 1 - j) if d else j
            rows = pl.ds(pl.multiple_of(jj * c, c), c)
            incl, strict, incl_t = masks[d]
            gb = g_ref[0, rows, :]
            for h in range(DN_HEADS):
                g_col = gb[:, d * DN_HEADS + h:d * DN_HEADS + h + 1]
                beta = gb[:, (2 + d) * DN_HEADS + h:(2 + d) * DN_HEADS + h + 1]
                g_row = jnp.sum(jnp.where(eye_b, g_col, 0.0), axis=0, keepdims=True)
                cum_col = jnp.sum(jnp.where(incl, g_row, 0.0), axis=1, keepdims=True)
                cum_row = jnp.sum(jnp.where(incl_t, g_col, 0.0), axis=0, keepdims=True)
                g_tot = jnp.sum(g_col, axis=0, keepdims=True)
                lanes = slice(h * DN_DK, (h + 1) * DN_DK)
                q = q_ref[0, rows, lanes].astype(F32)
                k = k_ref[0, rows, lanes].astype(F32)
                v = v_ref[0, rows, lanes].astype(F32)
                eg = jnp.exp(cum_col)
                k_beta = k * beta
                chains.append(dict(
                    u=u, slot=d * DN_HEADS + h, rows=rows, lanes=lanes, o_ref=o_ref, strict=strict,
                    decay=jnp.exp(jnp.where(incl, cum_col - cum_row, neg)),
                    e_tot=jnp.exp(g_tot), q_dec=q * eg, k=k,
                    lhs=jnp.concatenate([k_beta, q], axis=0),
                    rhs=jnp.concatenate([v * beta, k_beta * eg], axis=1),
                    k_tail=k * jnp.exp(g_tot - cum_col)))
        gram = [_dot16(ch["lhs"], ch["k"], "nt") for ch in chains]
        kks = [g[:c] * ch["decay"] for g, ch in zip(gram, chains)]
        qks = [g[c:] * ch["decay"] for g, ch in zip(gram, chains)]
        t_inv = _nilpotent_inverses([jnp.where(ch["strict"], -kk, 0.0) for kk, ch in zip(kks, chains)], eye)
        uw = [_dot16(t, ch["rhs"]) for t, ch in zip(t_inv, chains)]
        quw = [_dot16(qk, x) for qk, x in zip(qks, uw)]
        kuw = [_dot16(ch["k_tail"], x, "tn") for ch, x in zip(chains, uw)]
        o_a = [ch["q_dec"] - x[:, DN_DV:] for ch, x in zip(chains, quw)]
        state = [s_ref[slot] for slot in range(2 * DN_HEADS)]
        for u in range(DN_UNROLL):
            mine = [n for n, ch in enumerate(chains) if ch["u"] == u]
            cur = [state[chains[n]["slot"]] for n in mine]
            outs = [_dot16(o_a[n], s) + quw[n][:, :DN_DV] for n, s in zip(mine, cur)]
            news = [s * chains[n]["e_tot"] - _dot16(kuw[n][:, DN_DV:], s) + kuw[n][:, :DN_DV] for n, s in zip(mine, cur)]
            for n, o, s_new in zip(mine, outs, news):
                ch = chains[n]
                ch["o_ref"][0, ch["rows"], ch["lanes"]] = o.astype(ch["o_ref"].dtype)
                state[ch["slot"]] = s_new
        for slot in range(2 * DN_HEADS):
            s_ref[slot] = state[slot]
        return carry

    lax.fori_loop(0, n_sub // DN_UNROLL, chunk, 0)


def _dn_scan(q, k, v, gb):
    b, l, _ = q.shape
    t = SCAN_TILE
    n = l // t
    fwd = lambda w: pl.BlockSpec((1, t, w), lambda bi, ti: (bi, ti, 0))
    bwd = lambda w: pl.BlockSpec((1, t, w), lambda bi, ti: (bi, n - 1 - ti, 0))
    w = DN_WIDTH
    return pl.pallas_call(
        _dn_scan_kernel, name="dn_scan",
        grid=(b, n),
        in_specs=[fwd(w), fwd(w), fwd(w), fwd(LANE), bwd(w), bwd(w), bwd(w), bwd(LANE)],
        out_specs=[fwd(w), bwd(w)],
        out_shape=[jax.ShapeDtypeStruct((b, l, w), ACT)] * 2,
        scratch_shapes=[pltpu.VMEM((2 * DN_HEADS, DN_DK, DN_DV), F32)],
        compiler_params=_cparams(("parallel", "arbitrary")),
    )(q, k, v, gb, q, k, v, gb)


def _rk_prep_kernel(x_ref, xp_ref, xn_ref, mu_ref, w2_ref, w0_ref, a2_ref, a0_ref, kk_w_ref, ka_ref, seg_ref,
                    r_ref, k_ref, v_ref, kk_ref, a_ref, lw_ref):
    halo_p, cur, halo_n = _load_with_halo(x_ref, xp_ref, xn_ref)
    prev = _shifted(halo_p, cur, halo_n, -1)
    nxt = _shifted(halo_p, cur, halo_n, 1)
    s = cur + mu_ref[0:1, :] * (prev - cur) + mu_ref[1:2, :] * (nxt - cur)
    w = RK_WIDTH
    r = s[:, 0:w]
    k = s[:, w:2 * w]
    v = s[:, 2 * w:3 * w]
    wd = s[:, 3 * w:3 * w + 2 * RK_LORA]
    ad = s[:, 3 * w + 2 * RK_LORA:3 * w + 4 * RK_LORA]
    lora_w = _dot16(jnp.tanh(wd), w2_ref[...])
    w_log = -_softplus(-(w0_ref[...] + lora_w)) - 0.5
    lw_ref[0] = -jnp.exp(w_log)
    a = _sigmoid(a0_ref[...] + _dot16(ad, a2_ref[...]))
    kk_raw = k * kk_w_ref[...]
    ssq = _dot_exact_rhs(kk_raw * kk_raw, seg_ref[...])
    kk_ref[0] = (kk_raw * lax.rsqrt(ssq + RMS_EPS)).astype(kk_ref.dtype)
    r_ref[0] = r.astype(r_ref.dtype)
    k_ref[0] = (k * (1.0 + (a - 1.0) * ka_ref[...])).astype(k_ref.dtype)
    v_ref[0] = v.astype(v_ref.dtype)
    a_ref[0] = a.astype(a_ref.dtype)


def _seg_ones(width, group):
    i = np.arange(width) // group
    return (i[:, None] == i[None, :]).astype(np.float32)


def _rk_prep(rk, mu, w0, w2, a0, a2, k_k, k_a):
    b, l, wp = rk.shape
    t = ROW_TILE
    w = RK_WIDTH
    mu_p = jnp.zeros((2, wp), F32).at[:, :RK_SHIFT].set(mu.astype(F32))
    w2cat = jnp.zeros((2 * RK_LORA, 2 * w), F32)
    w2cat = w2cat.at[:RK_LORA, :w].set(w2[0]).at[RK_LORA:, w:].set(w2[1]).astype(BF16)
    w0cat = w0.astype(F32).reshape(1, 2 * w)
    a2p = jnp.zeros((2 * RK_LORA, w), F32).at[:RK_LORA].set(a2).astype(BF16)
    prev, nxt = _halo_specs(t, wp, 0, l)
    blk = lambda width: pl.BlockSpec((1, t, width), lambda bi, ti: (bi, ti, 0))
    return pl.pallas_call(
        _rk_prep_kernel, name="rk_prep",
        grid=(b, l // t),
        in_specs=[blk(wp), prev, nxt, _full((2, wp)), _full((2 * RK_LORA, 2 * w)), _full((1, 2 * w)),
                  _full((2 * RK_LORA, w)), _full((1, w)), _full((1, w)), _full((1, w)), _full((w, w))],
        out_specs=[blk(w)] * 5 + [blk(2 * w)],
        out_shape=[jax.ShapeDtypeStruct((b, l, w), ACT)] * 5 + [jax.ShapeDtypeStruct((b, l, 2 * w), F32)],
        compiler_params=_cparams(("parallel", "parallel")),
    )(rk, rk, rk, mu_p, w2cat, w0cat, a2p, a0.astype(F32).reshape(1, w), k_k.astype(F32).reshape(1, w),
      k_a.astype(F32).reshape(1, w), jnp.asarray(_seg_ones(w, RK_HEAD), dtype=BF16))


def _pair_diag(x):
    low = lax.broadcasted_iota(jnp.int32, x.shape, 1) < RK_HEAD
    return jnp.concatenate([jnp.where(low, x, 0.0), jnp.where(low, 0.0, x)], axis=0).astype(BF16)


def _rk_scan_kernel(rf_ref, kf_ref, vf_ref, kkf_ref, af_ref, lwf_ref, rb_ref, kb_ref, vb_ref, kkb_ref, ab_ref,
                    lwb_ref, yf_ref, yb_ref, s_ref):
    c = RK_CHUNK
    n_sub = rf_ref.shape[1] // c
    n_pair = RK_WIDTH // LANE

    @pl.when(pl.program_id(1) == 0)
    def _():
        s_ref[...] = jnp.zeros_like(s_ref)

    row = lax.broadcasted_iota(jnp.int32, (c, LANE), 0)
    col = lax.broadcasted_iota(jnp.int32, (c, LANE), 1) % RK_HEAD
    eye2 = (row == col).astype(F32)
    masks2 = ((col <= row, col < row), (col >= row, col > row))
    low_half = lax.broadcasted_iota(jnp.int32, (RK_HEAD, LANE), 1) < RK_HEAD
    same_block = (lax.broadcasted_iota(jnp.int32, (LANE, LANE), 0) // RK_HEAD) == (
        lax.broadcasted_iota(jnp.int32, (LANE, LANE), 1) // RK_HEAD)
    masks = (_tri_masks(c, False), _tri_masks(c, True))
    sides = ((rf_ref, kf_ref, vf_ref, kkf_ref, af_ref, lwf_ref, yf_ref),
             (rb_ref, kb_ref, vb_ref, kkb_ref, ab_ref, lwb_ref, yb_ref))

    def chunk(i, carry):
        chains = []
        for u, d in [(u, d) for u in range(RK_UNROLL) for d in range(2)]:
            r_ref, k_ref, v_ref, kk_ref, a_ref, lw_ref, y_ref = sides[d]
            j = i * RK_UNROLL + u
            jj = (n_sub - 1 - j) if d else j
            rows = pl.ds(pl.multiple_of(jj * c, c), c)
            incl, _, _ = masks[d]
            tri16 = incl.astype(BF16)
            lw = lw_ref[0, rows, :]
            l1 = lw.astype(BF16)
            rem = lw - l1.astype(F32)
            l2 = rem.astype(BF16)
            l3 = (rem - l2.astype(F32)).astype(BF16)
            cum = (jnp.dot(tri16, l1, preferred_element_type=F32) + jnp.dot(tri16, l2, preferred_element_type=F32)
                   + jnp.dot(tri16, l3, preferred_element_type=F32))
            tot = jnp.sum(lw, axis=0, keepdims=True)
            e_neg = jnp.exp(-cum)
            e_tail = jnp.exp(tot - cum)
            e_tot = jnp.exp(tot)
            k = k_ref[0, rows, :].astype(F32)
            v = v_ref[0, rows, :].astype(F32)
            kk = kk_ref[0, rows, :].astype(F32)
            b_vec = kk * a_ref[0, rows, :].astype(F32)
            ra = r_ref[0, rows, :].astype(F32) * jnp.exp(cum)
            aa = -kk * jnp.exp(cum - lw)
            bb = b_vec * e_neg
            kb = k * e_neg
            bt = b_vec * e_tail
            kt = k * e_tail
            for g in range(n_pair):
                lanes = slice(g * LANE, (g + 1) * LANE)
                chains.append(dict(
                    u=u, slot=d * n_pair + g, rows=rows, lanes=lanes, y_ref=y_ref, incl=masks2[d][0],
                    strict=masks2[d][1],
                    aa=aa[:, lanes], ra=ra[:, lanes], bb=bb[:, lanes], kb=kb[:, lanes], v=v[:, lanes],
                    bt=bt[:, lanes], kt=kt[:, lanes], e_tot=e_tot[:, lanes]))
        for ch in chains:
            ch["lhs"] = jnp.concatenate([ch["aa"], ch["ra"]], axis=0)
        gb = [_dot16(ch["lhs"], _pair_diag(ch["bb"]), "nt") for ch in chains]
        gk = [_dot16(ch["lhs"], _pair_diag(ch["kb"]), "nt") for ch in chains]
        a_ab = [jnp.where(ch["strict"], g[:c], 0.0) for g, ch in zip(gb, chains)]
        m_rb = [jnp.where(ch["incl"], g[c:], 0.0) for g, ch in zip(gb, chains)]
        akrk = [jnp.concatenate([jnp.where(ch["strict"], g[:c], 0.0), jnp.where(ch["incl"], g[c:], 0.0)], axis=0)
                for g, ch in zip(gk, chains)]
        avyv = [_dot16(m, _pair_diag(ch["v"])) for m, ch in zip(akrk, chains)]
        rs = [eye2 + x for x in a_ab]
        ps = [_dot16(x, _pair_diag(x)) for x in a_ab]
        for _ in range(int(math.log2(c)) - 2):
            zs = [_dot16(jnp.concatenate([r, p], axis=0), _pair_diag(p)) for r, p in zip(rs, ps)]
            rs = [r + z[:c] for r, z in zip(rs, zs)]
            ps = [z[c:] for z in zs]
        t_inv = [r + _dot16(r, _pair_diag(p)) for r, p in zip(rs, ps)]
        tq = [_dot16(t, jnp.concatenate([_pair_diag(ch["aa"]), _pair_diag(x[:c])], axis=1))
              for t, ch, x in zip(t_inv, chains, avyv)]
        yy = [_dot16(m, jnp.concatenate([_pair_diag(x[:, :LANE]), _pair_diag(x[:, LANE:])], axis=1))
              for m, x in zip(m_rb, tq)]
        ya = [ch["ra"] + y[:, :LANE] for ch, y in zip(chains, yy)]
        yb = [y[:, LANE:] + x[c:] for y, x in zip(yy, avyv)]
        wm = [jnp.where(same_block, _dot16(x[:, :LANE], ch["bt"], "tn"), 0.0) for x, ch in zip(tq, chains)]
        hc_full = [_dot16(jnp.concatenate([x[:, LANE:], ch["v"]], axis=0),
                          jnp.concatenate([ch["bt"], ch["kt"]], axis=0), "tn") for x, ch in zip(tq, chains)]
        hc = [jnp.where(low_half, x[:RK_HEAD], x[RK_HEAD:]) for x in hc_full]
        state = [s_ref[slot] for slot in range(2 * n_pair)]
        for u in range(RK_UNROLL):
            mine = [n for n, ch in enumerate(chains) if ch["u"] == u]
            cur = [state[chains[n]["slot"]] for n in mine]
            outs = [_dot16(ya[n], _pair_diag(s), "nt") + yb[n] for n, s in zip(mine, cur)]
            news = [s * chains[n]["e_tot"] + _dot16(s, wm[n]) + hc[n] for n, s in zip(mine, cur)]
            for n, y, s_new in zip(mine, outs, news):
                ch = chains[n]
                ch["y_ref"][0, ch["rows"], ch["lanes"]] = y.astype(ch["y_ref"].dtype)
                state[ch["slot"]] = s_new
        for slot in range(2 * n_pair):
            s_ref[slot] = state[slot]
        return carry

    lax.fori_loop(0, n_sub // RK_UNROLL, chunk, 0)


def _rk_scan(r, k, v, kk, a, lw):
    b, l, w = r.shape
    t = ROW_TILE
    n = l // t
    fwd = pl.BlockSpec((1, t, w), lambda bi, ti: (bi, ti, 0))
    bwd = pl.BlockSpec((1, t, w), lambda bi, ti: (bi, n - 1 - ti, 0))
    bwd_lw = pl.BlockSpec((1, t, w), lambda bi, ti: (bi, n - 1 - ti, 1))
    return pl.pallas_call(
        _rk_scan_kernel, name="rk_scan",
        grid=(b, n),
        in_specs=[fwd] * 6 + [bwd] * 5 + [bwd_lw],
        out_specs=[fwd, bwd],
        out_shape=[jax.ShapeDtypeStruct((b, l, w), ACT)] * 2,
        scratch_shapes=[pltpu.VMEM((2 * RK_WIDTH // LANE, RK_HEAD, LANE), F32)],
        compiler_params=_cparams(("parallel", "arbitrary")),
    )(r, k, v, kk, a, lw, r, k, v, kk, a, lw)


def _even_mix_kernel(of_ref, ob_ref, dg_ref, dnw_ref, yf_ref, yb_ref, r_ref, k_ref, v_ref, rg_ref,
                     rk_ref, lnw_ref, lnb_ref, segm_ref, seg1_ref, o_ref):
    f32 = lambda ref: ref[0].astype(F32)
    o = f32(of_ref) + f32(ob_ref)
    gate = _silu(f32(dg_ref))
    for h in range(DN_HEADS):
        lanes = slice(h * DN_DV, (h + 1) * DN_DV)
        oh = o[:, lanes]
        ms = jnp.mean(oh * oh, axis=-1, keepdims=True)
        o_ref[0, :, lanes] = (oh * lax.rsqrt(ms + RMS_EPS) * dnw_ref[...] * gate[:, lanes]).astype(o_ref.dtype)
    wkv = f32(yf_ref) + f32(yb_ref)
    mean = _dot_exact_rhs(wkv, segm_ref[...])
    cen = wkv - mean
    var = _dot_exact_rhs(cen * cen, segm_ref[...])
    wkv = cen * lax.rsqrt(var + RK_GN_EPS) * lnw_ref[...] + lnb_ref[...]
    bonus = _dot_exact_rhs(f32(r_ref) * f32(k_ref) * rk_ref[...], seg1_ref[...]) * f32(v_ref)
    o_ref[0, :, DN_WIDTH:] = ((wkv + bonus) * _silu(f32(rg_ref))).astype(o_ref.dtype)


def _even_mix(o_f, o_b, dn_gate, dn_norm, y_f, y_b, r, k, v, rk_gate, r_k, ln_w, ln_b):
    b, l, _ = o_f.shape
    t = ROW_TILE
    w = RK_WIDTH
    blk = lambda width: pl.BlockSpec((1, t, width), lambda bi, ti: (bi, ti, 0))
    seg1 = jnp.asarray(_seg_ones(w, RK_HEAD), dtype=BF16)
    segm = jnp.asarray(_seg_ones(w, RK_HEAD) / RK_HEAD, dtype=BF16)
    return pl.pallas_call(
        _even_mix_kernel, name="even_mix",
        grid=(b, l // t),
        in_specs=[blk(DN_WIDTH), blk(DN_WIDTH), blk(DN_WIDTH), _full((1, DN_DV)),
                  blk(w), blk(w), blk(w), blk(w), blk(w), blk(w),
                  _full((1, w)), _full((1, w)), _full((1, w)), _full((w, w)), _full((w, w))],
        out_specs=blk(DN_WIDTH + w),
        out_shape=jax.ShapeDtypeStruct((b, l, DN_WIDTH + w), ACT),
        compiler_params=_cparams(("parallel", "parallel")),
    )(o_f, o_b, dn_gate, dn_norm.astype(F32).reshape(1, DN_DV), y_f, y_b, r, k, v, rk_gate,
      r_k.astype(F32).reshape(1, w), ln_w.astype(F32).reshape(1, w), ln_b.astype(F32).reshape(1, w), segm, seg1)


def _even_layer(h, w_in, dn_conv, dn_a_log, dn_dt_bias, dn_norm, rk_mu, rk_w0, rk_w2, rk_a0, rk_a2,
                rk_k_k, rk_k_a, rk_r_k, rk_ln_w, rk_ln_b):
    b, l, d = h.shape
    s0 = DN_QKV
    s1 = s0 + DN_AB
    s2 = s1 + DN_WIDTH
    s3 = s2 + RK_SHIFT
    pad = lambda m, width: jnp.pad(m, ((0, 0), (0, width - m.shape[1])))
    widths = (DN_QKV, LANE, DN_WIDTH, RK_SHIFT_PAD, RK_WIDTH)
    w16 = jnp.concatenate([w_in[:, :s0], pad(w_in[:, s0:s1], LANE), w_in[:, s1:s2],
                           pad(w_in[:, s2:s3], RK_SHIFT_PAD), w_in[:, s3:]], axis=1).astype(BF16)
    qkv, ab, dn_gate, rk, rk_gate = _project(h.reshape(b * l, d), w16, widths, (ACT, F32, ACT, ACT, ACT))
    r3 = lambda m: m.reshape(b, l, m.shape[-1])
    q, k, v, gb = _dn_prep(r3(qkv), r3(ab), dn_conv, dn_a_log, dn_dt_bias)
    o_f, o_b = _dn_scan(q, k, v, gb)
    r, kr, vr, kk, a, lw = _rk_prep(r3(rk), rk_mu, rk_w0, rk_w2, rk_a0, rk_a2, rk_k_k, rk_k_a)
    y_f, y_b = _rk_scan(r, kr, vr, kk, a, lw)
    return _even_mix(o_f, o_b, r3(dn_gate), dn_norm, y_f, y_b, r, kr, vr, r3(rk_gate),
                     rk_r_k, rk_ln_w, rk_ln_b)


def _dft_geometry(l):
    nf = 2 * l
    p = nf // DFT_Q
    n1 = p // 2
    k1 = p // 2 + 1
    k1p = -(-k1 // SUBLANE) * SUBLANE
    return nf, p, n1, k1, k1p


@functools.lru_cache(maxsize=None)
def _dft_tables(l):
    nf, p, n1c, k1c, k1p = _dft_geometry(l)
    q = DFT_Q
    n2 = np.arange(q)[:, None, None]
    k1 = np.arange(k1c)[None, :, None]
    n1 = np.arange(n1c)[None, None, :]
    ph = -2.0 * np.pi * (((n1 * k1) % p) / p + ((n2 * k1) % nf) / nf)
    fa = np.zeros((q, 2 * k1p, n1c))
    fa[:, :k1c] = np.cos(ph)
    fa[:, k1p:k1p + k1c] = np.sin(ph)
    wgt = np.full((k1c,), 2.0)
    wgt[0] = 1.0
    wgt[-1] = 1.0
    th = -ph.transpose(0, 2, 1)
    gd = np.zeros((q, n1c, 2 * k1p))
    gd[:, :, :k1c] = np.cos(th) * wgt / nf
    gd[:, :, k1p:k1p + k1c] = -np.sin(th) * wgt / nf
    a = np.arange(q)
    ang = -2.0 * np.pi * ((a[:, None] * a[None, :]) % q) / q
    cr, ci = np.cos(ang), np.sin(ang)
    fb = np.block([[cr, -ci], [ci, cr]])
    fc = np.block([[cr, ci], [-ci, cr]])

    return tuple(m.astype(np.float32).astype(BF16) for m in (fa, fb, fc, gd))


def _fdot(f, x):
    return jnp.dot(f, x.astype(BF16), preferred_element_type=F32)


DFT_GROUP_N2 = 8
DFT_GROUP_K1 = 8
DFT_PITCH = DFT_Q + SUBLANE


def _stage_a(src, y_re, y_im, fa, geo):
    nf, p, n1c, k1c, k1p = geo
    g = DFT_GROUP_N2

    def body(i, carry):
        n2s = [i * g + t for t in range(g)]
        slabs = [src[pl.ds(n2, n1c, stride=DFT_PITCH), :] for n2 in n2s]
        outs = [_fdot(fa[n2], slab) for n2, slab in zip(n2s, slabs)]
        for n2, out in zip(n2s, outs):
            y_re[pl.ds(n2, k1p, stride=DFT_PITCH), :] = out[:k1p]
            y_im[pl.ds(n2, k1p, stride=DFT_PITCH), :] = out[k1p:]
        return carry

    lax.fori_loop(0, DFT_Q // g, body, 0)


def _fdot_pairs(f, xs):
    outs = []
    for a, b in zip(xs[0::2], xs[1::2]):
        z = _fdot(f, jnp.concatenate([a, b], axis=1))
        outs += [z[:, :a.shape[1]], z[:, a.shape[1]:]]
    return outs


def _stage_b(y_re, y_im, i, fb):
    g = DFT_GROUP_K1
    k1s = [i * g + t for t in range(g)]
    rows = [pl.ds(pl.multiple_of(k1 * DFT_PITCH, SUBLANE), DFT_Q) for k1 in k1s]
    ws = [jnp.concatenate([y_re[r, :], y_im[r, :]], axis=0) for r in rows]
    return k1s, rows, _fdot_pairs(fb[...], ws)


def _hy_conv_kernel(u_ref, m_ref, skip_ref, hr_ref, hi_ref, fa, fb, fc, gd, o_ref, pad, y_re, y_im, *, geo):
    nf, p, n1c, k1c, k1p = geo
    for n1 in range(n1c):
        pad[n1 * DFT_PITCH:n1 * DFT_PITCH + DFT_Q, :] = u_ref[0, n1 * DFT_Q:(n1 + 1) * DFT_Q, :].astype(F32)
    _stage_a(pad, y_re, y_im, fa, geo)

    def mid(i, carry):
        k1s, rows, zs = _stage_b(y_re, y_im, i, fb)
        prods = []
        for k1, z in zip(k1s, zs):
            zr, zi = z[:DFT_Q], z[DFT_Q:]
            hrows = pl.ds(pl.multiple_of(k1 * DFT_Q, DFT_Q), DFT_Q)
            hr = hr_ref[hrows, :]
            hi = hi_ref[hrows, :]
            prods.append(jnp.concatenate([zr * hr - zi * hi, zr * hi + zi * hr], axis=0))
        outs = _fdot_pairs(fc[...], prods)
        for r, a in zip(rows, outs):
            y_re[r, :] = a[:DFT_Q]
            y_im[r, :] = a[DFT_Q:]
        return carry

    lax.fori_loop(0, k1p // DFT_GROUP_K1, mid, 0)

    def last(i, carry):
        n2s = [i * DFT_GROUP_N2 + t for t in range(DFT_GROUP_N2)]
        ins = [jnp.concatenate([y_re[pl.ds(n2, k1p, stride=DFT_PITCH), :], y_im[pl.ds(n2, k1p, stride=DFT_PITCH), :]],
                               axis=0) for n2 in n2s]
        outs = [_fdot(gd[n2], a) for n2, a in zip(n2s, ins)]
        for n2, out in zip(n2s, outs):
            pad[pl.ds(n2, n1c, stride=DFT_PITCH), :] = out
        return carry

    lax.fori_loop(0, DFT_Q // DFT_GROUP_N2, last, 0)

    skip = skip_ref[...]
    for n1 in range(n1c):
        rows = slice(n1 * DFT_Q, (n1 + 1) * DFT_Q)
        conv = pad[n1 * DFT_PITCH:n1 * DFT_PITCH + DFT_Q, :]
        o_ref[0, rows, :] = (m_ref[0, rows, :].astype(F32)
                             * (conv + skip * u_ref[0, rows, :].astype(F32))).astype(o_ref.dtype)


def _single(shape, index_map):
    return pl.BlockSpec(shape, index_map, pipeline_mode=pl.Buffered(1))


def _hy_conv(u, mult, skip, h_re, h_im, tables):
    b, l, ch = u.shape
    geo = _dft_geometry(l)
    nf, p, n1c, k1c, k1p = geo
    ct = HY_CT
    consts = [jnp.asarray(t) for t in tables]
    seq = pl.BlockSpec((1, l, ct), lambda ci, bi: (bi, 0, ci))
    spec = _single((k1p * DFT_Q, ct), lambda ci, bi: (0, ci))
    cspecs = [_single(c.shape, (lambda ci, bi, nd=c.ndim: (0,) * nd)) for c in consts]
    ysc = pltpu.VMEM((k1p * DFT_PITCH, ct), F32)
    return pl.pallas_call(
        functools.partial(_hy_conv_kernel, geo=geo), name="hy_conv",
        grid=(ch // ct, b),
        in_specs=[seq, seq, pl.BlockSpec((1, ct), lambda ci, bi: (0, ci)), spec, spec] + cspecs,
        out_specs=seq,
        out_shape=jax.ShapeDtypeStruct((b, l, ch), ACT),
        scratch_shapes=[pltpu.VMEM((n1c * DFT_PITCH, ct), F32), ysc, ysc],
        compiler_params=_cparams(("parallel", "parallel")),
    )(u, mult, skip, h_re, h_im, *consts)


def _hy_mlp_kernel(f_ref, w1_ref, b1_ref, w2_ref, b2_ref, w3_ref, b3_ref, fr_ref, o_ref):
    fr = fr_ref[...]
    hdn = jnp.sin(fr * (_dot32(f_ref[...], w1_ref[...]) + b1_ref[...]))
    hdn = jnp.sin(fr * (_dot32(hdn, w2_ref[...]) + b2_ref[...]))
    o_ref[...] = jnp.sin(fr * (_dot32(hdn, w3_ref[...]) + b3_ref[...]))


def _hy_mlp(feats, w1, b1, w2, b2, w3, b3, freq):
    l = feats.shape[0]
    t = min(l, 1024)
    fw = HY_FW
    row = lambda a: a.astype(F32).reshape(1, fw)
    w1p = jnp.zeros((LANE, fw), F32).at[:HY_EMB].set(w1.astype(F32))
    return pl.pallas_call(
        _hy_mlp_kernel, name="hy_mlp",
        grid=(l // t,),
        in_specs=[pl.BlockSpec((t, LANE), lambda i: (i, 0)), _full((LANE, fw)), _full((1, fw)), _full((fw, fw)),
                  _full((1, fw)), _full((fw, fw)), _full((1, fw)), _full((1, fw))],
        out_specs=pl.BlockSpec((t, fw), lambda i: (i, 0)),
        out_shape=jax.ShapeDtypeStruct((l, fw), F32),
        compiler_params=_cparams(("parallel",)),
    )(feats, w1p, row(b1), w2.astype(F32), row(b2), w3.astype(F32), row(b3), row(freq))


def _hy_filter_kernel(hdn_ref, wf_ref, wb_ref, df_ref, db_ref, fa, fb,
                      hr_ref, hi_ref, filt, yr_f, yi_f, yr_b, yi_b, *, geo):
    nf, p, n1c, k1c, k1p = geo

    def build(w_ref, d_ref, drop_first):
        grp = 4

        def body(i, acc):
            n1s = [i * grp + t for t in range(grp)]
            hxs = [hdn_ref[pl.ds(pl.multiple_of(n1 * DFT_Q, DFT_Q), DFT_Q), :] for n1 in n1s]
            raw = [_dot32(hx, w_ref[0]) for hx in hxs]
            for n1, hx, hv in zip(n1s, hxs, raw):
                hv = hv * jnp.exp(-hx[:, HY_FW:HY_FW + 1] * jnp.abs(d_ref[0]))
                if drop_first:
                    pos = lax.broadcasted_iota(jnp.int32, hv.shape, 0) + n1 * DFT_Q
                    hv = jnp.where(pos == 0, 0.0, hv)
                filt[pl.ds(pl.multiple_of(n1 * DFT_PITCH, SUBLANE), DFT_Q), :] = hv
                acc = acc + jnp.sum(jnp.abs(hv), axis=0, keepdims=True)
            return acc

        return lax.fori_loop(0, n1c // grp, body, jnp.zeros((1, filt.shape[1]), F32))

    l1 = build(wf_ref, df_ref, False)
    _stage_a(filt, yr_f, yi_f, fa, geo)
    l1 = l1 + build(wb_ref, db_ref, True)
    _stage_a(filt, yr_b, yi_b, fa, geo)
    inv = 1.0 / (l1 + RMS_EPS)

    def mid(i, carry):
        k1s, _, zf = _stage_b(yr_f, yi_f, i, fb)
        _, _, zb = _stage_b(yr_b, yi_b, i, fb)
        for k1, f, b in zip(k1s, zf, zb):
            r = pl.ds(pl.multiple_of(k1 * DFT_Q, DFT_Q), DFT_Q)
            hr_ref[0, r, :] = (f[:DFT_Q] + b[:DFT_Q]) * inv
            hi_ref[0, r, :] = (f[DFT_Q:] - b[DFT_Q:]) * inv
        return carry

    lax.fori_loop(0, k1p // DFT_GROUP_K1, mid, 0)


def _hy_filters(hdn, tcol, w_out, deltas, tables, l):
    geo = _dft_geometry(l)
    nf, p, n1c, k1c, k1p = geo
    ch = w_out.shape[1] // (2 * HY_ORDER)
    ct = HY_CT
    nct = ch // ct
    hdn_x = jnp.concatenate([hdn, tcol, jnp.zeros((l, LANE - HY_FW - 1), F32)], axis=1)
    w4 = w_out.astype(F32).reshape(HY_FW, 2 * HY_ORDER, ch).transpose(1, 0, 2)
    w4 = jnp.pad(w4, ((0, 0), (0, LANE - HY_FW), (0, 0)))
    d4 = deltas.astype(F32).reshape(2 * HY_ORDER, 1, ch)
    consts = [jnp.asarray(t) for t in tables[:2]]
    cspecs = [_single(c.shape, (lambda o, ci, nd=c.ndim: (0,) * nd)) for c in consts]
    wspec = lambda d: pl.BlockSpec((1, LANE, ct), lambda o, ci: (2 * o + d, 0, ci))
    dspec = lambda d: pl.BlockSpec((1, 1, ct), lambda o, ci: (2 * o + d, 0, ci))
    ospec = pl.BlockSpec((1, k1p * DFT_Q, ct), lambda o, ci: (o, 0, ci))
    ysc = pltpu.VMEM((k1p * DFT_PITCH, ct), F32)
    return pl.pallas_call(
        functools.partial(_hy_filter_kernel, geo=geo), name="hy_filters",
        grid=(HY_ORDER, nct),
        in_specs=[_single((l, LANE), lambda o, ci: (0, 0)), wspec(0), wspec(1), dspec(0), dspec(1)] + cspecs,
        out_specs=[ospec, ospec],
        out_shape=[jax.ShapeDtypeStruct((HY_ORDER, k1p * DFT_Q, ch), F32)] * 2,
        scratch_shapes=[pltpu.VMEM((n1c * DFT_PITCH, ct), F32), ysc, ysc, ysc, ysc],
        compiler_params=_cparams(("parallel", "parallel")),
    )(hdn_x, w4, w4, d4, d4, *consts)


def _position_features(l):
    bands = (HY_EMB - 1) // 2
    t = jnp.linspace(0.0, 1.0, l, dtype=F32)[:, None]
    f = jnp.linspace(1e-4, bands - 1, bands, dtype=F32)[None, :]
    ang = (2.0 * math.pi / l) * jnp.arange(l, dtype=F32)[:, None] * f
    feats = jnp.concatenate([t, jnp.cos(ang), -jnp.sin(ang)], axis=-1)
    return jnp.pad(feats, ((0, 0), (0, LANE - HY_EMB))), t


def _hy_prep_kernel(x_ref, xp_ref, xn_ref, g_ref, cw_ref, cb_ref, x1_ref, m2_ref, v_ref):
    prev, cur, nxt = _load_with_halo(x_ref, xp_ref, xn_ref)
    y = (_shifted(prev, cur, nxt, -1) * cw_ref[0:1, :] + cur * cw_ref[1:2, :]
         + _shifted(prev, cur, nxt, 1) * cw_ref[2:3, :] + cb_ref[...])
    c = g_ref.shape[2]
    x1_ref[0] = y[:, :c].astype(x1_ref.dtype)
    m2_ref[0] = (y[:, c:2 * c] * _silu(g_ref[0].astype(F32))).astype(m2_ref.dtype)
    v_ref[0] = y[:, 2 * c:].astype(v_ref.dtype)


def _hy_prep(xv, gate, conv_w, conv_b):
    b, l, w3 = xv.shape
    c = gate.shape[2]
    t = ROW_TILE
    prev, nxt = _halo_specs(t, w3, 0, l)
    blk = lambda width: pl.BlockSpec((1, t, width), lambda bi, ti: (bi, ti, 0))
    return pl.pallas_call(
        _hy_prep_kernel, name="hy_prep",
        grid=(b, l // t),
        in_specs=[blk(w3), prev, nxt, blk(c), _full((HY_SHORT, w3)), _full((1, w3))],
        out_specs=[blk(c)] * 3,
        out_shape=[jax.ShapeDtypeStruct((b, l, c), ACT)] * 3,
        compiler_params=_cparams(("parallel", "parallel")),
    )(xv, xv, xv, gate, conv_w.astype(F32), conv_b.astype(F32).reshape(1, w3))


def _odd_layer(h, w_in, conv_w, conv_b, f_w1, f_b1, f_w2, f_b2, f_w3, f_b3, f_freq, f_out, deltas, skip):
    b, l, d = h.shape
    c = skip.shape[1]
    xv, gate = _project(h.reshape(b * l, d), w_in.astype(BF16), (3 * c, c), (ACT, ACT))
    x1, m2, v = _hy_prep(xv.reshape(b, l, 3 * c), gate.reshape(b, l, c), conv_w, conv_b)
    tables = _dft_tables(l)
    feats, tcol = _position_features(l)
    hdn = _hy_mlp(feats, f_w1, f_b1, f_w2, f_b2, f_w3, f_b3, f_freq)
    h_re, h_im = _hy_filters(hdn, tcol, f_out, deltas, tables, l)
    skip = skip.astype(F32)
    z = _hy_conv(v, x1, skip[0:1], h_re[0], h_im[0], tables)
    return _hy_conv(z, m2, skip[1:2], h_re[1], h_im[1], tables)


def kernel(x, p, even_w_in, dn_conv, dn_a_log, dn_dt_bias, dn_norm, rk_mu, rk_w0, rk_w2, rk_a0, rk_a2, rk_k_k, rk_k_a, rk_r_k, rk_ln_w, rk_ln_b, odd_w_in, hy_conv_w, hy_conv_b, hy_ffn_w1, hy_ffn_b1, hy_ffn_w2, hy_ffn_b2, hy_ffn_w3, hy_ffn_b3, hy_ffn_freq, hy_ffn_out, hy_deltas, hy_skip, w_out, ln_g, ln_b, ple_w, ple_norm, ple_gate):
    b, l, d = x.shape
    depth = p.shape[0]
    alpha = (2.0 * depth) ** 0.25
    h = x
    for i in range(depth):
        j = i // 2
        if i % 2 == 0:
            mix = _even_layer(h, even_w_in[j], dn_conv[j], dn_a_log[j], dn_dt_bias[j], dn_norm[j], rk_mu[j],
                              rk_w0[j], rk_w2[j], rk_a0[j], rk_a2[j], rk_k_k[j].reshape(-1), rk_k_a[j].reshape(-1),
                              rk_r_k[j].reshape(-1), rk_ln_w[j], rk_ln_b[j])
        else:
            mix = _odd_layer(h, odd_w_in[j], hy_conv_w[j], hy_conv_b[j], hy_ffn_w1[j], hy_ffn_b1[j],
                             hy_ffn_w2[j], hy_ffn_b2[j], hy_ffn_w3[j], hy_ffn_b3[j], hy_ffn_freq[j],
                             hy_ffn_out[j], hy_deltas[j], hy_skip[j])
        h2 = _post_layer(h.reshape(b * l, d), mix.reshape(b * l, mix.shape[-1]), p.reshape(depth, b * l, p.shape[-1]),
                         i, w_out[i], ple_w[i], ple_gate[i], ln_g[i], ln_b[i], ple_norm[i], alpha)
        h = h2.reshape(b, l, d)
    return h
```

```python
import functools
import math

import numpy as np
import jax
import jax.numpy as jnp
from jax import lax
from jax.experimental import pallas as pl
from jax.experimental.pallas import tpu as pltpu

F32 = jnp.float32
BF16 = jnp.bfloat16

LN_EPS = 1e-5
RMS_EPS = 1e-6

DN_HEADS = 4
DN_DK = 128
DN_DV = 128
DN_WIDTH = DN_HEADS * DN_DV
DN_QKV = 2 * DN_HEADS * DN_DK + DN_WIDTH
DN_AB = 4 * DN_HEADS
DN_CONV = 5
DN_CHUNK = 64
DN_UNROLL = 4

RK_HEADS = 8
RK_HEAD = 64
RK_WIDTH = RK_HEADS * RK_HEAD
RK_LORA = 64
RK_SHIFT = 3 * RK_WIDTH + 3 * RK_LORA
RK_SHIFT_PAD = 1792
RK_GN_EPS = 64e-5
RK_CHUNK = 64
RK_UNROLL = 2

HY_ORDER = 2
HY_SHORT = 3
HY_EMB = 33
HY_FW = 64

LANE = 128
SUBLANE = 8
DFT_Q = 128
VMEM_LIMIT = 56 * 1024 * 1024

ROW_TILE = 256
SCAN_TILE = 512
HY_CT = 128

HI = lax.Precision.HIGHEST
ACT = BF16


def _cparams(sem):
    return pltpu.CompilerParams(dimension_semantics=sem, vmem_limit_bytes=VMEM_LIMIT)


_DIMS = {
    "nn": (((1,), (0,)), ((), ())),
    "nt": (((1,), (1,)), ((), ())),
    "tn": (((0,), (0,)), ((), ())),
}


def _dot16(a, b, dims="nn"):
    return lax.dot_general(a.astype(BF16), b.astype(BF16), _DIMS[dims], preferred_element_type=F32)


def _dot32(a, b, dims="nn"):
    return lax.dot_general(a.astype(F32), b.astype(F32), _DIMS[dims], precision=HI,
                           preferred_element_type=F32)


def _split2(x):
    hi = x.astype(BF16)
    lo = (x - hi.astype(F32)).astype(BF16)
    return hi, lo


def _dot_exact_rhs(x, m16):
    hi, lo = _split2(x)
    return (jnp.dot(hi, m16, preferred_element_type=F32) + jnp.dot(lo, m16, preferred_element_type=F32))


def _sigmoid(x):
    return 1.0 / (1.0 + jnp.exp(-x))


def _silu(x):
    return x * _sigmoid(x)


def _softplus(x):
    return jnp.maximum(x, 0.0) + jnp.log1p(jnp.exp(-jnp.abs(x)))


HALO = 16


SHIFT_ROWS = 128


def _tile_and_shifts(x_ref, xp_ref, xn_ref):
    assert x_ref.dtype == BF16
    t_idx = pl.program_id(1)
    t = x_ref.shape[1]
    cur16 = x_ref[0]
    zero = jnp.zeros_like(xp_ref[0])
    ext = jnp.concatenate([jnp.where(t_idx > 0, xp_ref[0], zero), cur16,
                           jnp.where(t_idx < pl.num_programs(1) - 1, xn_ref[0], zero)], axis=0)
    k = SHIFT_ROWS + 2 * HALO
    row = lax.broadcasted_iota(jnp.int32, (SHIFT_ROWS, k), 0)
    col = lax.broadcasted_iota(jnp.int32, (SHIFT_ROWS, k), 1)

    def shift(d):
        sel = jnp.where(col == row + (HALO + d), 1.0, 0.0).astype(BF16)
        return jnp.concatenate([jnp.dot(sel, ext[r0:r0 + k], preferred_element_type=F32)
                                for r0 in range(0, t, SHIFT_ROWS)], axis=0)

    return cur16.astype(F32), shift


def _halo_specs(t_rows, width, col, l_total):
    nb = t_rows // HALO
    last = l_total // HALO - 1
    prev = pl.BlockSpec((1, HALO, width), lambda b, t: (b, jnp.maximum(t * nb - 1, 0), col))
    nxt = pl.BlockSpec((1, HALO, width), lambda b, t: (b, jnp.minimum((t + 1) * nb, last), col))
    return prev, nxt


def _full(shape):
    nd = len(shape)
    return pl.BlockSpec(shape, lambda *_: (0,) * nd)


def _proj_kernel(a_ref, w_ref, *o_refs, offs):
    a = a_ref[...].astype(BF16)
    for o_ref, (lo, hi) in zip(o_refs, offs):
        o_ref[...] = jnp.dot(a, w_ref[:, lo:hi], preferred_element_type=F32).astype(o_ref.dtype)


def _project(a, w16, widths, dtypes):
    m, k = a.shape
    offs, o = [], 0
    for w in widths:
        offs.append((o, o + w))
        o += w
    n = o
    tm = ROW_TILE
    return pl.pallas_call(
        functools.partial(_proj_kernel, offs=tuple(offs)), name="project",
        grid=(m // tm,),
        in_specs=[pl.BlockSpec((tm, k), lambda i: (i, 0)), _full((k, n))],
        out_specs=[pl.BlockSpec((tm, w), lambda i: (i, 0)) for w in widths],
        out_shape=[jax.ShapeDtypeStruct((m, w), dt) for w, dt in zip(widths, dtypes)],
        compiler_params=_cparams(("parallel",)),
    )(a, w16)


def _post_kernel(h_ref, mix_ref, p_ref, wo_ref, pw_ref, pg_ref, lng_ref, lnb_ref, pn_ref, o_ref, *, alpha):
    t = alpha * h_ref[...] + jnp.dot(mix_ref[...], wo_ref[...], preferred_element_type=F32)
    mu = jnp.mean(t, axis=-1, keepdims=True)
    tc = t - mu
    var = jnp.mean(tc * tc, axis=-1, keepdims=True)
    y = tc * lax.rsqrt(var + LN_EPS) * lng_ref[...] + lnb_ref[...]
    e = jnp.dot(p_ref[...].astype(BF16), pw_ref[...], preferred_element_type=F32)
    e = e * lax.rsqrt(jnp.mean(e * e, axis=-1, keepdims=True) + RMS_EPS) * pn_ref[...]
    gate = _sigmoid(jnp.dot(y.astype(BF16), pg_ref[...], preferred_element_type=F32))
    o_ref[...] = y + gate * e


def _post_layer(h, mix, p_all, layer, w_out, ple_w, ple_gate, ln_g, ln_b, ple_norm, alpha):
    m, d = h.shape
    pd = p_all.shape[2]
    tm = ROW_TILE
    row = lambda w: pl.BlockSpec((tm, w), lambda i: (i, 0))
    p_spec = pl.BlockSpec((None, tm, pd), lambda i: (layer, i, 0))
    return pl.pallas_call(
        functools.partial(_post_kernel, alpha=alpha), name="post_layer",
        grid=(m // tm,),
        in_specs=[row(d), row(mix.shape[1]), p_spec, _full(w_out.shape), _full(ple_w.shape),
                  _full(ple_gate.shape), _full((1, d)), _full((1, d)), _full((1, d))],
        out_specs=row(d),
        out_shape=jax.ShapeDtypeStruct((m, d), F32),
        compiler_params=_cparams(("parallel",)),
    )(h, mix, p_all, w_out.astype(BF16), ple_w.astype(BF16), ple_gate.astype(BF16),
      ln_g.reshape(1, d), ln_b.reshape(1, d), ple_norm.reshape(1, d))


def _dn_prep_kernel(x_ref, xp_ref, xn_ref, ab_ref, cw_ref, ga_ref, gbias_ref, q_ref, k_ref, v_ref, gb_ref):
    cur, shift = _tile_and_shifts(x_ref, xp_ref, xn_ref)
    pad = DN_CONV // 2
    acc = cur * cw_ref[pad:pad + 1, :]
    for j in range(DN_CONV):
        if j != pad:
            acc = acc + shift(j - pad) * cw_ref[j:j + 1, :]
    y = _silu(acc)
    nqk = DN_HEADS * DN_DK
    for h in range(DN_HEADS):
        qh = y[:, h * DN_DK:(h + 1) * DN_DK]
        kh = y[:, nqk + h * DN_DK:nqk + (h + 1) * DN_DK]
        qn = lax.rsqrt(jnp.sum(qh * qh, axis=-1, keepdims=True) + RMS_EPS) * (DN_DK ** -0.5)
        kn = lax.rsqrt(jnp.sum(kh * kh, axis=-1, keepdims=True) + RMS_EPS)
        q_ref[0, :, h * DN_DK:(h + 1) * DN_DK] = (qh * qn).astype(q_ref.dtype)
        k_ref[0, :, h * DN_DK:(h + 1) * DN_DK] = (kh * kn).astype(k_ref.dtype)
    v_ref[0] = y[:, 2 * nqk:].astype(v_ref.dtype)
    ab = ab_ref[0]
    lane = lax.broadcasted_iota(jnp.int32, ab.shape, 1)
    g = ga_ref[...] * _softplus(ab + gbias_ref[...])
    gb_ref[0] = jnp.where(lane < 2 * DN_HEADS, g, _sigmoid(ab))


def _dn_prep(qkv, ab, conv_w, a_log, dt_bias):
    b, l, _ = qkv.shape
    t = SCAN_TILE
    ga = jnp.zeros((1, LANE), F32).at[0, :2 * DN_HEADS].set(-jnp.exp(a_log.astype(F32)).reshape(-1))
    gbias = jnp.zeros((1, LANE), F32).at[0, :2 * DN_HEADS].set(dt_bias.astype(F32).reshape(-1))
    prev, nxt = _halo_specs(t, DN_QKV, 0, l)
    blk = lambda w: pl.BlockSpec((1, t, w), lambda bi, ti: (bi, ti, 0))
    return pl.pallas_call(
        _dn_prep_kernel, name="dn_prep",
        grid=(b, l // t),
        in_specs=[blk(DN_QKV), prev, nxt, blk(LANE), _full((DN_CONV, DN_QKV)), _full((1, LANE)), _full((1, LANE))],
        out_specs=[blk(DN_WIDTH), blk(DN_WIDTH), blk(DN_WIDTH), blk(LANE)],
        out_shape=[jax.ShapeDtypeStruct((b, l, DN_WIDTH), ACT)] * 3 + [jax.ShapeDtypeStruct((b, l, LANE), F32)],
        compiler_params=_cparams(("parallel", "parallel")),
    )(qkv, qkv, qkv, ab, conv_w.astype(F32), ga, gbias)


def _tri_masks(c, reverse):
    r = lax.broadcasted_iota(jnp.int32, (c, c), 0)
    s = lax.broadcasted_iota(jnp.int32, (c, c), 1)
    if reverse:
        return s >= r, s > r, r >= s
    return s <= r, s < r, r <= s


HALF = LANE // 2


def _pair_diag(x):
    low = lax.broadcasted_iota(jnp.int32, x.shape, 1) < HALF
    return jnp.concatenate([jnp.where(low, x, 0.0), jnp.where(low, 0.0, x)], axis=0).astype(BF16)


def _pair_inverses(xs, eye2, c):
    rs = [eye2 + x for x in xs]
    ps = [_dot16(x, _pair_diag(x)) for x in xs]
    for _ in range(int(math.log2(c)) - 2):
        zs = [_dot16(jnp.concatenate([r, p], axis=0), _pair_diag(p)) for r, p in zip(rs, ps)]
        rs = [r + z[:c] for r, z in zip(rs, zs)]
        ps = [z[c:] for z in zs]
    return [r + _dot16(r, _pair_diag(p)) for r, p in zip(rs, ps)]


def _dn_scan_kernel(qf_ref, kf_ref, vf_ref, gf_ref, qb_ref, kb_ref, vb_ref, gb_ref, of_ref, ob_ref, s_ref):
    c = DN_CHUNK
    n_sub = qf_ref.shape[1] // c

    @pl.when(pl.program_id(1) == 0)
    def _():
        s_ref[...] = jnp.zeros_like(s_ref)

    row = lax.broadcasted_iota(jnp.int32, (c, LANE), 0)
    col = lax.broadcasted_iota(jnp.int32, (c, LANE), 1) % c
    low = lax.broadcasted_iota(jnp.int32, (c, LANE), 1) < c
    eye2_b = row == col
    eye2 = eye2_b.astype(F32)
    masks2 = ((col <= row, col < row, row <= col), (col >= row, col > row, row >= col))
    neg = jnp.float32(-1e30)
    sides =((qf_ref, kf_ref, vf_ref, gf_ref, of_ref), (qb_ref, kb_ref, vb_ref, gb_ref, ob_ref))

    def diag2(a, b):
        a16, b16 = a.astype(BF16), b.astype(BF16)
        za = jnp.zeros((a16.shape[0], b16.shape[1]), BF16)
        zb = jnp.zeros((b16.shape[0], a16.shape[1]), BF16)
        return jnp.concatenate([jnp.concatenate([a16, za], axis=1), jnp.concatenate([zb, b16], axis=1)], axis=0)

    def chunk(i, carry):
        chains, pairs = [], []
        for u, d in [(u, d) for u in range(DN_UNROLL) for d in range(2)]:
            q_ref, k_ref, v_ref, g_ref, o_ref = sides[d]
            j = i * DN_UNROLL + u
            jj = (n_sub - 1 - j) if d else j
            rows = pl.ds(pl.multiple_of(jj * c, c), c)
            incl2, strict2, incl_t2 = masks2[d]
            gb = g_ref[0, rows, :]
            for h0 in range(0, DN_HEADS, 2):
                g_cols = [gb[:, d * DN_HEADS + h:d * DN_HEADS + h + 1] for h in (h0, h0 + 1)]
                g_sel = jnp.where(low, g_cols[0], g_cols[1])
                g_row = jnp.sum(jnp.where(eye2_b, g_sel, 0.0), axis=0, keepdims=True)
                cum_row = jnp.sum(jnp.where(incl_t2, g_sel, 0.0), axis=0, keepdims=True)
                cum_cols = [jnp.sum(jnp.where(incl2 & m, g_row, 0.0), axis=1, keepdims=True)
                            for m in (low, jnp.logical_not(low))]
                decay = jnp.exp(jnp.where(incl2, jnp.where(low, cum_cols[0], cum_cols[1]) - cum_row, neg))
                first = len(chains)
                for t, h in enumerate((h0, h0 + 1)):
                    beta = gb[:, (2 + d) * DN_HEADS + h:(2 + d) * DN_HEADS + h + 1]
                    g_tot = jnp.sum(g_cols[t], axis=0, keepdims=True)
                    lanes = slice(h * DN_DK, (h + 1) * DN_DK)
                    q = q_ref[0, rows, lanes].astype(F32)
                    k = k_ref[0, rows, lanes].astype(F32)
                    v = v_ref[0, rows, lanes].astype(F32)
                    eg = jnp.exp(cum_cols[t])
                    k_beta = k * beta
                    chains.append(dict(
                        u=u, slot=d * DN_HEADS + h, rows=rows, lanes=lanes, o_ref=o_ref,
                        e_tot=jnp.exp(g_tot), q_dec=q * eg, k=k,
                        lhs=jnp.concatenate([k_beta, q], axis=0),
                        rhs=jnp.concatenate([v * beta, k_beta * eg], axis=1),
                        k_tail=k * jnp.exp(g_tot - cum_cols[t])))
                pairs.append(dict(a=first, b=first + 1, decay=decay, strict=strict2))
        gram = [_dot16(jnp.concatenate([chains[p["a"]]["lhs"], chains[p["b"]]["lhs"]], axis=1),
                       diag2(chains[p["a"]]["k"], chains[p["b"]]["k"]), "nt") for p in pairs]
        kks = [g[:c] * p["decay"] for g, p in zip(gram, pairs)]
        qks = [g[c:] * p["decay"] for g, p in zip(gram, pairs)]
        t_inv = _pair_inverses([jnp.where(p["strict"], -kk, 0.0) for kk, p in zip(kks, pairs)], eye2, c)
        uw_p = [_dot16(t, diag2(chains[p["a"]]["rhs"], chains[p["b"]]["rhs"])) for t, p in zip(t_inv, pairs)]
        split = lambda xs: [part for x in xs for part in (x[:, :2 * DN_DV], x[:, 2 * DN_DV:])]
        uw = split(uw_p)
        quw = split([_dot16(qk, diag2(x[:, :2 * DN_DV], x[:, 2 * DN_DV:])) for qk, x in zip(qks, uw_p)])
        kuw = [_dot16(ch["k_tail"], x, "tn") for ch, x in zip(chains, uw)]
        o_a = [ch["q_dec"] - x[:, DN_DV:] for ch, x in zip(chains, quw)]
        state = [s_ref[slot] for slot in range(2 * DN_HEADS)]
        for u in range(DN_UNROLL):
            mine = [n for n, ch in enumerate(chains) if ch["u"] == u]
            cur = [state[chains[n]["slot"]] for n in mine]
            outs = [_dot16(o_a[n], s) + quw[n][:, :DN_DV] for n, s in zip(mine, cur)]
            news = [s * chains[n]["e_tot"] - _dot16(kuw[n][:, DN_DV:], s) + kuw[n][:, :DN_DV] for n, s in zip(mine, cur)]
            for n, o, s_new in zip(mine, outs, news):
                ch = chains[n]
                ch["o_ref"][0, ch["rows"], ch["lanes"]] = o.astype(ch["o_ref"].dtype)
                state[ch["slot"]] = s_new
        for slot in range(2 * DN_HEADS):
            s_ref[slot] = state[slot]
        return carry

    lax.fori_loop(0, n_sub // DN_UNROLL, chunk, 0)


def _dn_scan(q, k, v, gb):
    b, l, _ = q.shape
    t = SCAN_TILE
    n = l // t
    fwd = lambda w: pl.BlockSpec((1, t, w), lambda bi, ti: (bi, ti, 0))
    bwd = lambda w: pl.BlockSpec((1, t, w), lambda bi, ti: (bi, n - 1 - ti, 0))
    w = DN_WIDTH
    return pl.pallas_call(
        _dn_scan_kernel, name="dn_scan",
        grid=(b, n),
        in_specs=[fwd(w), fwd(w), fwd(w), fwd(LANE), bwd(w), bwd(w), bwd(w), bwd(LANE)],
        out_specs=[fwd(w), bwd(w)],
        out_shape=[jax.ShapeDtypeStruct((b, l, w), ACT)] * 2,
        scratch_shapes=[pltpu.VMEM((2 * DN_HEADS, DN_DK, DN_DV), F32)],
        compiler_params=_cparams(("parallel", "arbitrary")),
    )(q, k, v, gb, q, k, v, gb)


def _rk_prep_kernel(x_ref, xp_ref, xn_ref, mu_ref, w2_ref, w0_ref, a2_ref, a0_ref, kk_w_ref, ka_ref, seg_ref,
                    r_ref, k_ref, v_ref, kk_ref, a_ref, lw_ref):
    cur, shift = _tile_and_shifts(x_ref, xp_ref, xn_ref)
    s = cur + mu_ref[0:1, :] * (shift(-1) - cur) + mu_ref[1:2, :] * (shift(1) - cur)
    w = RK_WIDTH
    r = s[:, 0:w]
    k = s[:, w:2 * w]
    v = s[:, 2 * w:3 * w]
    wd = s[:, 3 * w:3 * w + 2 * RK_LORA]
    ad = s[:, 3 * w + 2 * RK_LORA:3 * w + 4 * RK_LORA]
    lora_w = _dot16(jnp.tanh(wd), w2_ref[...])
    lw_ref[0] = -math.exp(-0.5) * _sigmoid(w0_ref[...] + lora_w)
    a = _sigmoid(a0_ref[...] + _dot16(ad, a2_ref[...]))
    kk_raw = k * kk_w_ref[...]
    ssq = _dot_exact_rhs(kk_raw * kk_raw, seg_ref[...])
    kk_ref[0] = (kk_raw * lax.rsqrt(ssq + RMS_EPS)).astype(kk_ref.dtype)
    r_ref[0] = r.astype(r_ref.dtype)
    k_ref[0] = (k * (1.0 + (a - 1.0) * ka_ref[...])).astype(k_ref.dtype)
    v_ref[0] = v.astype(v_ref.dtype)
    a_ref[0] = a.astype(a_ref.dtype)


def _seg_ones(width, group):
    i = np.arange(width) // group
    return (i[:, None] == i[None, :]).astype(np.float32)


def _rk_prep(rk, mu, w0, w2, a0, a2, k_k, k_a):
    b, l, wp = rk.shape
    t = ROW_TILE
    w = RK_WIDTH
    mu_p = jnp.zeros((2, wp), F32).at[:, :RK_SHIFT].set(mu.astype(F32))
    w2cat = jnp.zeros((2 * RK_LORA, 2 * w), F32)
    w2cat = w2cat.at[:RK_LORA, :w].set(w2[0]).at[RK_LORA:, w:].set(w2[1]).astype(BF16)
    w0cat = w0.astype(F32).reshape(1, 2 * w)
    a2p = jnp.zeros((2 * RK_LORA, w), F32).at[:RK_LORA].set(a2).astype(BF16)
    prev, nxt = _halo_specs(t, wp, 0, l)
    blk = lambda width: pl.BlockSpec((1, t, width), lambda bi, ti: (bi, ti, 0))
    return pl.pallas_call(
        _rk_prep_kernel, name="rk_prep",
        grid=(b, l // t),
        in_specs=[blk(wp), prev, nxt, _full((2, wp)), _full((2 * RK_LORA, 2 * w)), _full((1, 2 * w)),
                  _full((2 * RK_LORA, w)), _full((1, w)), _full((1, w)), _full((1, w)), _full((w, w))],
        out_specs=[blk(w)] * 5 + [blk(2 * w)],
        out_shape=[jax.ShapeDtypeStruct((b, l, w), ACT)] * 5 + [jax.ShapeDtypeStruct((b, l, 2 * w), F32)],
        compiler_params=_cparams(("parallel", "parallel")),
    )(rk, rk, rk, mu_p, w2cat, w0cat, a2p, a0.astype(F32).reshape(1, w), k_k.astype(F32).reshape(1, w),
      k_a.astype(F32).reshape(1, w), jnp.asarray(_seg_ones(w, RK_HEAD), dtype=BF16))


def _rk_scan_kernel(rf_ref, kf_ref, vf_ref, kkf_ref, af_ref, lwf_ref, rb_ref, kb_ref, vb_ref, kkb_ref, ab_ref,
                    lwb_ref, yf_ref, yb_ref, s_ref):
    c = RK_CHUNK
    n_sub = rf_ref.shape[1] // c
    n_pair = RK_WIDTH // LANE

    @pl.when(pl.program_id(1) == 0)
    def _():
        s_ref[...] = jnp.zeros_like(s_ref)

    row = lax.broadcasted_iota(jnp.int32, (c, LANE), 0)
    col = lax.broadcasted_iota(jnp.int32, (c, LANE), 1) % RK_HEAD
    eye2 = (row == col).astype(F32)
    masks2 = ((col <= row, col < row), (col >= row, col > row))
    low_half = lax.broadcasted_iota(jnp.int32, (RK_HEAD, LANE), 1) < RK_HEAD
    same_block = (lax.broadcasted_iota(jnp.int32, (LANE, LANE), 0) // RK_HEAD) == (
        lax.broadcasted_iota(jnp.int32, (LANE, LANE), 1) // RK_HEAD)
    masks = (_tri_masks(c, False), _tri_masks(c, True))
    sides = ((rf_ref, kf_ref, vf_ref, kkf_ref, af_ref, lwf_ref, yf_ref),
             (rb_ref, kb_ref, vb_ref, kkb_ref, ab_ref, lwb_ref, yb_ref))

    def chunk(i, carry):
        chains = []
        for u, d in [(u, d) for u in range(RK_UNROLL) for d in range(2)]:
            r_ref, k_ref, v_ref, kk_ref, a_ref, lw_ref, y_ref = sides[d]
            j = i * RK_UNROLL + u
            jj = (n_sub - 1 - j) if d else j
            rows = pl.ds(pl.multiple_of(jj * c, c), c)
            incl, _, _ = masks[d]
            tri16 = incl.astype(BF16)
            lw = lw_ref[0, rows, :]
            l1 = lw.astype(BF16)
            rem = lw - l1.astype(F32)
            l2 = rem.astype(BF16)
            l3 = (rem - l2.astype(F32)).astype(BF16)
            cum = (jnp.dot(tri16, l1, preferred_element_type=F32) + jnp.dot(tri16, l2, preferred_element_type=F32)
                   + jnp.dot(tri16, l3, preferred_element_type=F32))
            tot = jnp.sum(lw, axis=0, keepdims=True)
            e_neg = jnp.exp(-cum)
            e_tail = jnp.exp(tot - cum)
            e_tot = jnp.exp(tot)
            k = k_ref[0, rows, :].astype(F32)
            v = v_ref[0, rows, :].astype(F32)
            kk = kk_ref[0, rows, :].astype(F32)
            b_vec = kk * a_ref[0, rows, :].astype(F32)
            ra = r_ref[0, rows, :].astype(F32) * jnp.exp(cum)
            aa = -kk * jnp.exp(cum - lw)
            bb = b_vec * e_neg
            kb = k * e_neg
            bt = b_vec * e_tail
            kt = k * e_tail
            for g in range(n_pair):
                lanes = slice(g * LANE, (g + 1) * LANE)
                chains.append(dict(
                    u=u, slot=d * n_pair + g, rows=rows, lanes=lanes, y_ref=y_ref, incl=masks2[d][0],
                    strict=masks2[d][1],
                    aa=aa[:, lanes], ra=ra[:, lanes], bb=bb[:, lanes], kb=kb[:, lanes], v=v[:, lanes],
                    bt=bt[:, lanes], kt=kt[:, lanes], e_tot=e_tot[:, lanes]))
        for ch in chains:
            ch["lhs"] = jnp.concatenate([ch["aa"], ch["ra"]], axis=0)
        gb = [_dot16(ch["lhs"], _pair_diag(ch["bb"]), "nt") for ch in chains]
        gk = [_dot16(ch["lhs"], _pair_diag(ch["kb"]), "nt") for ch in chains]
        a_ab = [jnp.where(ch["strict"], g[:c], 0.0) for g, ch in zip(gb, chains)]
        m_rb = [jnp.where(ch["incl"], g[c:], 0.0) for g, ch in zip(gb, chains)]
        akrk = [jnp.concatenate([jnp.where(ch["strict"], g[:c], 0.0), jnp.where(ch["incl"], g[c:], 0.0)], axis=0)
                for g, ch in zip(gk, chains)]
        avyv = [_dot16(m, _pair_diag(ch["v"])) for m, ch in zip(akrk, chains)]
        t_inv = _pair_inverses(a_ab, eye2, c)
        tq = [_dot16(t, jnp.concatenate([_pair_diag(ch["aa"]), _pair_diag(x[:c])], axis=1))
              for t, ch, x in zip(t_inv, chains, avyv)]
        yy = [_dot16(m, jnp.concatenate([_pair_diag(x[:, :LANE]), _pair_diag(x[:, LANE:])], axis=1))
              for m, x in zip(m_rb, tq)]
        ya = [ch["ra"] + y[:, :LANE] for ch, y in zip(chains, yy)]
        yb = [y[:, LANE:] + x[c:] for y, x in zip(yy, avyv)]
        wm = [jnp.where(same_block, _dot16(x[:, :LANE], ch["bt"], "tn"), 0.0) for x, ch in zip(tq, chains)]
        hc_full = [_dot16(jnp.concatenate([x[:, LANE:], ch["v"]], axis=0),
                          jnp.concatenate([ch["bt"], ch["kt"]], axis=0), "tn") for x, ch in zip(tq, chains)]
        hc = [jnp.where(low_half, x[:RK_HEAD], x[RK_HEAD:]) for x in hc_full]
        state = [s_ref[slot] for slot in range(2 * n_pair)]
        for u in range(RK_UNROLL):
            mine = [n for n, ch in enumerate(chains) if ch["u"] == u]
            cur = [state[chains[n]["slot"]] for n in mine]
            outs = [_dot16(ya[n], _pair_diag(s), "nt") + yb[n] for n, s in zip(mine, cur)]
            news = [s * chains[n]["e_tot"] + _dot16(s, wm[n]) + hc[n] for n, s in zip(mine, cur)]
            for n, y, s_new in zip(mine, outs, news):
                ch = chains[n]
                ch["y_ref"][0, ch["rows"], ch["lanes"]] = y.astype(ch["y_ref"].dtype)
                state[ch["slot"]] = s_new
        for slot in range(2 * n_pair):
            s_ref[slot] = state[slot]
        return carry

    lax.fori_loop(0, n_sub // RK_UNROLL, chunk, 0)


def _rk_scan(r, k, v, kk, a, lw):
    b, l, w = r.shape
    t = ROW_TILE
    n = l // t
    fwd = pl.BlockSpec((1, t, w), lambda bi, ti: (bi, ti, 0))
    bwd = pl.BlockSpec((1, t, w), lambda bi, ti: (bi, n - 1 - ti, 0))
    bwd_lw = pl.BlockSpec((1, t, w), lambda bi, ti: (bi, n - 1 - ti, 1))
    return pl.pallas_call(
        _rk_scan_kernel, name="rk_scan",
        grid=(b, n),
        in_specs=[fwd] * 6 + [bwd] * 5 + [bwd_lw],
        out_specs=[fwd, bwd],
        out_shape=[jax.ShapeDtypeStruct((b, l, w), ACT)] * 2,
        scratch_shapes=[pltpu.VMEM((2 * RK_WIDTH // LANE, RK_HEAD, LANE), F32)],
        compiler_params=_cparams(("parallel", "arbitrary")),
    )(r, k, v, kk, a, lw, r, k, v, kk, a, lw)


def _even_mix_kernel(of_ref, ob_ref, dg_ref, dnw_ref, yf_ref, yb_ref, r_ref, k_ref, v_ref, rg_ref,
                     rk_ref, lnw_ref, lnb_ref, segm_ref, seg1_ref, o_ref):
    f32 = lambda ref: ref[0].astype(F32)
    o = f32(of_ref) + f32(ob_ref)
    gate = _silu(f32(dg_ref))
    for h in range(DN_HEADS):
        lanes = slice(h * DN_DV, (h + 1) * DN_DV)
        oh = o[:, lanes]
        ms = jnp.mean(oh * oh, axis=-1, keepdims=True)
        o_ref[0, :, lanes] = (oh * lax.rsqrt(ms + RMS_EPS) * dnw_ref[...] * gate[:, lanes]).astype(o_ref.dtype)
    wkv = f32(yf_ref) + f32(yb_ref)
    mean = _dot_exact_rhs(wkv, segm_ref[...])
    cen = wkv - mean
    var = _dot_exact_rhs(cen * cen, segm_ref[...])
    wkv = cen * lax.rsqrt(var + RK_GN_EPS) * lnw_ref[...] + lnb_ref[...]
    bonus = _dot_exact_rhs(f32(r_ref) * f32(k_ref) * rk_ref[...], seg1_ref[...]) * f32(v_ref)
    o_ref[0, :, DN_WIDTH:] = ((wkv + bonus) * _silu(f32(rg_ref))).astype(o_ref.dtype)


def _even_mix(o_f, o_b, dn_gate, dn_norm, y_f, y_b, r, k, v, rk_gate, r_k, ln_w, ln_b):
    b, l, _ = o_f.shape
    t = ROW_TILE
    w = RK_WIDTH
    blk = lambda width: pl.BlockSpec((1, t, width), lambda bi, ti: (bi, ti, 0))
    seg1 = jnp.asarray(_seg_ones(w, RK_HEAD), dtype=BF16)
    segm = jnp.asarray(_seg_ones(w, RK_HEAD) / RK_HEAD, dtype=BF16)
    return pl.pallas_call(
        _even_mix_kernel, name="even_mix",
        grid=(b, l // t),
        in_specs=[blk(DN_WIDTH), blk(DN_WIDTH), blk(DN_WIDTH), _full((1, DN_DV)),
                  blk(w), blk(w), blk(w), blk(w), blk(w), blk(w),
                  _full((1, w)), _full((1, w)), _full((1, w)), _full((w, w)), _full((w, w))],
        out_specs=blk(DN_WIDTH + w),
        out_shape=jax.ShapeDtypeStruct((b, l, DN_WIDTH + w), ACT),
        compiler_params=_cparams(("parallel", "parallel")),
    )(o_f, o_b, dn_gate, dn_norm.astype(F32).reshape(1, DN_DV), y_f, y_b, r, k, v, rk_gate,
      r_k.astype(F32).reshape(1, w), ln_w.astype(F32).reshape(1, w), ln_b.astype(F32).reshape(1, w), segm, seg1)


def _even_layer(h, w_in, dn_conv, dn_a_log, dn_dt_bias, dn_norm, rk_mu, rk_w0, rk_w2, rk_a0, rk_a2,
                rk_k_k, rk_k_a, rk_r_k, rk_ln_w, rk_ln_b):
    b, l, d = h.shape
    s0 = DN_QKV
    s1 = s0 + DN_AB
    s2 = s1 + DN_WIDTH
    s3 = s2 + RK_SHIFT
    pad = lambda m, width: jnp.pad(m, ((0, 0), (0, width - m.shape[1])))
    widths = (DN_QKV, LANE, DN_WIDTH, RK_SHIFT_PAD, RK_WIDTH)
    w16 = jnp.concatenate([w_in[:, :s0], pad(w_in[:, s0:s1], LANE), w_in[:, s1:s2],
                           pad(w_in[:, s2:s3], RK_SHIFT_PAD), w_in[:, s3:]], axis=1).astype(BF16)
    qkv, ab, dn_gate, rk, rk_gate = _project(h.reshape(b * l, d), w16, widths, (ACT, F32, ACT, ACT, ACT))
    r3 = lambda m: m.reshape(b, l, m.shape[-1])
    q, k, v, gb = _dn_prep(r3(qkv), r3(ab), dn_conv, dn_a_log, dn_dt_bias)
    o_f, o_b = _dn_scan(q, k, v, gb)
    r, kr, vr, kk, a, lw = _rk_prep(r3(rk), rk_mu, rk_w0, rk_w2, rk_a0, rk_a2, rk_k_k, rk_k_a)
    y_f, y_b = _rk_scan(r, kr, vr, kk, a, lw)
    return _even_mix(o_f, o_b, r3(dn_gate), dn_norm, y_f, y_b, r, kr, vr, r3(rk_gate),
                     rk_r_k, rk_ln_w, rk_ln_b)


def _dft_geometry(l):
    nf = 2 * l
    p = nf // DFT_Q
    n1 = p // 2
    k1 = p // 2 + 1
    k1p = -(-k1 // SUBLANE) * SUBLANE
    return nf, p, n1, k1, k1p


@functools.lru_cache(maxsize=None)
def _dft_tables(l):
    nf, p, n1c, k1c, k1p = _dft_geometry(l)
    q = DFT_Q
    n2 = np.arange(q)[:, None, None]
    k1 = np.arange(k1c)[None, :, None]
    n1 = np.arange(n1c)[None, None, :]
    ph = -2.0 * np.pi * (((n1 * k1) % p) / p + ((n2 * k1) % nf) / nf)
    fa = np.zeros((q, 2 * k1p, n1c))
    fa[:, :k1c] = np.cos(ph)
    fa[:, k1p:k1p + k1c] = np.sin(ph)
    wgt = np.full((k1c,), 2.0)
    wgt[0] = 1.0
    wgt[-1] = 1.0
    th = -ph.transpose(0, 2, 1)
    gd = np.zeros((q, n1c, 2 * k1p))
    gd[:, :, :k1c] = np.cos(th) * wgt / nf
    gd[:, :, k1p:k1p + k1c] = -np.sin(th) * wgt / nf
    a = np.arange(q)
    ang = -2.0 * np.pi * ((a[:, None] * a[None, :]) % q) / q
    cr, ci = np.cos(ang), np.sin(ang)
    fb = np.block([[cr, -ci], [ci, cr]])
    fc = np.block([[cr, ci], [-ci, cr]])

    return tuple(m.astype(np.float32).astype(BF16) for m in (fa, fb, fc, gd))


def _fdot(f, x):
    return jnp.dot(f, x.astype(BF16), preferred_element_type=F32)


DFT_GROUP_N2 = 8
DFT_GROUP_K1 = 8
DFT_PITCH = DFT_Q + SUBLANE


def _stage_a(src, y_re, y_im, fa, geo):
    nf, p, n1c, k1c, k1p = geo
    g = DFT_GROUP_N2

    def body(i, carry):
        n2s = [i * g + t for t in range(g)]
        slabs = [src[pl.ds(n2, n1c, stride=DFT_PITCH), :] for n2 in n2s]
        outs = [_fdot(fa[n2], slab) for n2, slab in zip(n2s, slabs)]
        for n2, out in zip(n2s, outs):
            y_re[pl.ds(n2, k1p, stride=DFT_PITCH), :] = out[:k1p]
            y_im[pl.ds(n2, k1p, stride=DFT_PITCH), :] = out[k1p:]
        return carry

    lax.fori_loop(0, DFT_Q // g, body, 0)


def _fdot_pairs(f, xs):
    outs = []
    for a, b in zip(xs[0::2], xs[1::2]):
        z = _fdot(f, jnp.concatenate([a, b], axis=1))
        outs += [z[:, :a.shape[1]], z[:, a.shape[1]:]]
    return outs


def _stage_b(y_re, y_im, i, fb):
    g = DFT_GROUP_K1
    k1s = [i * g + t for t in range(g)]
    rows = [pl.ds(pl.multiple_of(k1 * DFT_PITCH, SUBLANE), DFT_Q) for k1 in k1s]
    ws = [jnp.concatenate([y_re[r, :], y_im[r, :]], axis=0) for r in rows]
    return k1s, rows, _fdot_pairs(fb[...], ws)


def _hy_conv_kernel(u_ref, m_ref, skip_ref, hr_ref, hi_ref, fa, fb, fc, gd, o_ref, pad, y_re, y_im, *, geo):
    nf, p, n1c, k1c, k1p = geo
    for n1 in range(n1c):
        pad[n1 * DFT_PITCH:n1 * DFT_PITCH + DFT_Q, :] = u_ref[0, n1 * DFT_Q:(n1 + 1) * DFT_Q, :].astype(F32)
    _stage_a(pad, y_re, y_im, fa, geo)

    def mid(i, carry):
        k1s, rows, zs = _stage_b(y_re, y_im, i, fb)
        prods = []
        for k1, z in zip(k1s, zs):
            zr, zi = z[:DFT_Q], z[DFT_Q:]
            hrows = pl.ds(pl.multiple_of(k1 * DFT_Q, DFT_Q), DFT_Q)
            hr = hr_ref[hrows, :]
            hi = hi_ref[hrows, :]
            prods.append(jnp.concatenate([zr * hr - zi * hi, zr * hi + zi * hr], axis=0))
        outs = _fdot_pairs(fc[...], prods)
        for r, a in zip(rows, outs):
            y_re[r, :] = a[:DFT_Q]
            y_im[r, :] = a[DFT_Q:]
        return carry

    lax.fori_loop(0, k1p // DFT_GROUP_K1, mid, 0)

    def last(i, carry):
        n2s = [i * DFT_GROUP_N2 + t for t in range(DFT_GROUP_N2)]
        ins = [jnp.concatenate([y_re[pl.ds(n2, k1p, stride=DFT_PITCH), :], y_im[pl.ds(n2, k1p, stride=DFT_PITCH), :]],
                               axis=0) for n2 in n2s]
        outs = [_fdot(gd[n2], a) for n2, a in zip(n2s, ins)]
        for n2, out in zip(n2s, outs):
            pad[pl.ds(n2, n1c, stride=DFT_PITCH), :] = out
        return carry

    lax.fori_loop(0, DFT_Q // DFT_GROUP_N2, last, 0)

    skip = skip_ref[...]
    for n1 in range(n1c):
        rows = slice(n1 * DFT_Q, (n1 + 1) * DFT_Q)
        conv = pad[n1 * DFT_PITCH:n1 * DFT_PITCH + DFT_Q, :]
        o_ref[0, rows, :] = (m_ref[0, rows, :].astype(F32)
                             * (conv + skip * u_ref[0, rows, :].astype(F32))).astype(o_ref.dtype)


def _single(shape, index_map):
    return pl.BlockSpec(shape, index_map, pipeline_mode=pl.Buffered(1))


def _hy_conv(u, mult, skip, h_re, h_im, order, tables):
    b, l, ch = u.shape
    geo = _dft_geometry(l)
    nf, p, n1c, k1c, k1p = geo
    ct = HY_CT
    consts = [jnp.asarray(t) for t in tables]
    seq = pl.BlockSpec((1, l, ct), lambda ci, bi: (bi, 0, ci))
    spec = _single((None, k1p * DFT_Q, ct), lambda ci, bi: (order, 0, ci))
    cspecs = [_single(c.shape, (lambda ci, bi, nd=c.ndim: (0,) * nd)) for c in consts]
    ysc = pltpu.VMEM((k1p * DFT_PITCH, ct), F32)
    return pl.pallas_call(
        functools.partial(_hy_conv_kernel, geo=geo), name="hy_conv",
        grid=(ch // ct, b),
        in_specs=[seq, seq, pl.BlockSpec((1, ct), lambda ci, bi: (0, ci)), spec, spec] + cspecs,
        out_specs=seq,
        out_shape=jax.ShapeDtypeStruct((b, l, ch), ACT),
        scratch_shapes=[pltpu.VMEM((n1c * DFT_PITCH, ct), F32), ysc, ysc],
        compiler_params=_cparams(("parallel", "parallel")),
    )(u, mult, skip, h_re, h_im, *consts)


def _hy_mlp_kernel(f_ref, w1_ref, b1_ref, w2_ref, b2_ref, w3_ref, b3_ref, fr_ref, o_ref):
    fr = fr_ref[...]
    hdn = jnp.sin(fr * (_dot32(f_ref[...], w1_ref[...]) + b1_ref[...]))
    hdn = jnp.sin(fr * (_dot32(hdn, w2_ref[...]) + b2_ref[...]))
    o_ref[...] = jnp.sin(fr * (_dot32(hdn, w3_ref[...]) + b3_ref[...]))


def _hy_mlp(feats, w1, b1, w2, b2, w3, b3, freq):
    l = feats.shape[0]
    t = min(l, 1024)
    fw = HY_FW
    row = lambda a: a.astype(F32).reshape(1, fw)
    w1p = jnp.zeros((LANE, fw), F32).at[:HY_EMB].set(w1.astype(F32))
    return pl.pallas_call(
        _hy_mlp_kernel, name="hy_mlp",
        grid=(l // t,),
        in_specs=[pl.BlockSpec((t, LANE), lambda i: (i, 0)), _full((LANE, fw)), _full((1, fw)), _full((fw, fw)),
                  _full((1, fw)), _full((fw, fw)), _full((1, fw)), _full((1, fw))],
        out_specs=pl.BlockSpec((t, fw), lambda i: (i, 0)),
        out_shape=jax.ShapeDtypeStruct((l, fw), F32),
        compiler_params=_cparams(("parallel",)),
    )(feats, w1p, row(b1), w2.astype(F32), row(b2), w3.astype(F32), row(b3), row(freq))


def _hy_filter_kernel(hdn_ref, wf_ref, wb_ref, df_ref, db_ref, fa, fb,
                      hr_ref, hi_ref, filt, yr_f, yi_f, yr_b, yi_b, *, geo):
    nf, p, n1c, k1c, k1p = geo

    def build(w_ref, d_ref, drop_first):
        grp = 4

        def body(i, acc):
            n1s = [i * grp + t for t in range(grp)]
            hxs = [hdn_ref[pl.ds(pl.multiple_of(n1 * DFT_Q, DFT_Q), DFT_Q), :] for n1 in n1s]
            raw = [_dot32(hx, w_ref[0]) for hx in hxs]
            for n1, hx, hv in zip(n1s, hxs, raw):
                hv = hv * jnp.exp(-hx[:, HY_FW:HY_FW + 1] * jnp.abs(d_ref[0]))
                if drop_first:
                    pos = lax.broadcasted_iota(jnp.int32, hv.shape, 0) + n1 * DFT_Q
                    hv = jnp.where(pos == 0, 0.0, hv)
                filt[pl.ds(pl.multiple_of(n1 * DFT_PITCH, SUBLANE), DFT_Q), :] = hv
                acc = acc + jnp.sum(jnp.abs(hv), axis=0, keepdims=True)
            return acc

        return lax.fori_loop(0, n1c // grp, body, jnp.zeros((1, filt.shape[1]), F32))

    l1 = build(wf_ref, df_ref, False)
    _stage_a(filt, yr_f, yi_f, fa, geo)
    l1 = l1 + build(wb_ref, db_ref, True)
    _stage_a(filt, yr_b, yi_b, fa, geo)
    inv = 1.0 / (l1 + RMS_EPS)

    def mid(i, carry):
        k1s, _, zf = _stage_b(yr_f, yi_f, i, fb)
        _, _, zb = _stage_b(yr_b, yi_b, i, fb)
        for k1, f, b in zip(k1s, zf, zb):
            r = pl.ds(pl.multiple_of(k1 * DFT_Q, DFT_Q), DFT_Q)
            hr_ref[0, r, :] = (f[:DFT_Q] + b[:DFT_Q]) * inv
            hi_ref[0, r, :] = (f[DFT_Q:] - b[DFT_Q:]) * inv
        return carry

    lax.fori_loop(0, k1p // DFT_GROUP_K1, mid, 0)


def _hy_filters(hdn, tcol, w_out, deltas, tables, l):
    geo = _dft_geometry(l)
    nf, p, n1c, k1c, k1p = geo
    ch = w_out.shape[1] // (2 * HY_ORDER)
    ct = HY_CT
    nct = ch // ct
    hdn_x = jnp.concatenate([hdn, tcol, jnp.zeros((l, LANE - HY_FW - 1), F32)], axis=1)
    w4 = w_out.astype(F32).reshape(HY_FW, 2 * HY_ORDER, ch).transpose(1, 0, 2)
    w4 = jnp.pad(w4, ((0, 0), (0, LANE - HY_FW), (0, 0)))
    d4 = deltas.astype(F32).reshape(2 * HY_ORDER, 1, ch)
    consts = [jnp.asarray(t) for t in tables[:2]]
    cspecs = [_single(c.shape, (lambda o, ci, nd=c.ndim: (0,) * nd)) for c in consts]
    wspec = lambda d: pl.BlockSpec((1, LANE, ct), lambda o, ci: (2 * o + d, 0, ci))
    dspec = lambda d: pl.BlockSpec((1, 1, ct), lambda o, ci: (2 * o + d, 0, ci))
    ospec = pl.BlockSpec((1, k1p * DFT_Q, ct), lambda o, ci: (o, 0, ci))
    ysc = pltpu.VMEM((k1p * DFT_PITCH, ct), F32)
    return pl.pallas_call(
        functools.partial(_hy_filter_kernel, geo=geo), name="hy_filters",
        grid=(HY_ORDER, nct),
        in_specs=[_single((l, LANE), lambda o, ci: (0, 0)), wspec(0), wspec(1), dspec(0), dspec(1)] + cspecs,
        out_specs=[ospec, ospec],
        out_shape=[jax.ShapeDtypeStruct((HY_ORDER, k1p * DFT_Q, ch), F32)] * 2,
        scratch_shapes=[pltpu.VMEM((n1c * DFT_PITCH, ct), F32), ysc, ysc, ysc, ysc],
        compiler_params=_cparams(("parallel", "parallel")),
    )(hdn_x, w4, w4, d4, d4, *consts)


def _position_features(l):
    bands = (HY_EMB - 1) // 2
    t = jnp.linspace(0.0, 1.0, l, dtype=F32)[:, None]
    f = jnp.linspace(1e-4, bands - 1, bands, dtype=F32)[None, :]
    ang = (2.0 * math.pi / l) * jnp.arange(l, dtype=F32)[:, None] * f
    feats = jnp.concatenate([t, jnp.cos(ang), -jnp.sin(ang)], axis=-1)
    return jnp.pad(feats, ((0, 0), (0, LANE - HY_EMB))), t


def _hy_prep_kernel(x_ref, xp_ref, xn_ref, g_ref, cw_ref, cb_ref, x1_ref, m2_ref, v_ref):
    cur, shift = _tile_and_shifts(x_ref, xp_ref, xn_ref)
    y = shift(-1) * cw_ref[0:1, :] + cur * cw_ref[1:2, :] + shift(1) * cw_ref[2:3, :] + cb_ref[...]
    c = g_ref.shape[2]
    x1_ref[0] = y[:, :c].astype(x1_ref.dtype)
    m2_ref[0] = (y[:, c:2 * c] * _silu(g_ref[0].astype(F32))).astype(m2_ref.dtype)
    v_ref[0] = y[:, 2 * c:].astype(v_ref.dtype)


def _hy_prep(xv, gate, conv_w, conv_b):
    b, l, w3 = xv.shape
    c = gate.shape[2]
    t = ROW_TILE
    prev, nxt = _halo_specs(t, w3, 0, l)
    blk = lambda width: pl.BlockSpec((1, t, width), lambda bi, ti: (bi, ti, 0))
    return pl.pallas_call(
        _hy_prep_kernel, name="hy_prep",
        grid=(b, l // t),
        in_specs=[blk(w3), prev, nxt, blk(c), _full((HY_SHORT, w3)), _full((1, w3))],
        out_specs=[blk(c)] * 3,
        out_shape=[jax.ShapeDtypeStruct((b, l, c), ACT)] * 3,
        compiler_params=_cparams(("parallel", "parallel")),
    )(xv, xv, xv, gate, conv_w.astype(F32), conv_b.astype(F32).reshape(1, w3))


def _odd_layer(h, w_in, conv_w, conv_b, f_w1, f_b1, f_w2, f_b2, f_w3, f_b3, f_freq, f_out, deltas, skip):
    b, l, d = h.shape
    c = skip.shape[1]
    xv, gate = _project(h.reshape(b * l, d), w_in.astype(BF16), (3 * c, c), (ACT, ACT))
    x1, m2, v = _hy_prep(xv.reshape(b, l, 3 * c), gate.reshape(b, l, c), conv_w, conv_b)
    tables = _dft_tables(l)
    feats, tcol = _position_features(l)
    hdn = _hy_mlp(feats, f_w1, f_b1, f_w2, f_b2, f_w3, f_b3, f_freq)
    h_re, h_im = _hy_filters(hdn, tcol, f_out, deltas, tables, l)
    skip = skip.astype(F32)
    z = _hy_conv(v, x1, skip[0:1], h_re, h_im, 0, tables)
    return _hy_conv(z, m2, skip[1:2], h_re, h_im, 1, tables)


def kernel(x, p, even_w_in, dn_conv, dn_a_log, dn_dt_bias, dn_norm, rk_mu, rk_w0, rk_w2, rk_a0, rk_a2, rk_k_k, rk_k_a, rk_r_k, rk_ln_w, rk_ln_b, odd_w_in, hy_conv_w, hy_conv_b, hy_ffn_w1, hy_ffn_b1, hy_ffn_w2, hy_ffn_b2, hy_ffn_w3, hy_ffn_b3, hy_ffn_freq, hy_ffn_out, hy_deltas, hy_skip, w_out, ln_g, ln_b, ple_w, ple_norm, ple_gate):
    b, l, d = x.shape
    depth = p.shape[0]
    alpha = (2.0 * depth) ** 0.25
    h = x
    for i in range(depth):
        j = i // 2
        if i % 2 == 0:
            mix = _even_layer(h, even_w_in[j], dn_conv[j], dn_a_log[j], dn_dt_bias[j], dn_norm[j], rk_mu[j],
                              rk_w0[j], rk_w2[j], rk_a0[j], rk_a2[j], rk_k_k[j].reshape(-1), rk_k_a[j].reshape(-1),
                              rk_r_k[j].reshape(-1), rk_ln_w[j], rk_ln_b[j])
        else:
            mix = _odd_layer(h, odd_w_in[j], hy_conv_w[j], hy_conv_b[j], hy_ffn_w1[j], hy_ffn_b1[j],
                             hy_ffn_w2[j], hy_ffn_b2[j], hy_ffn_w3[j], hy_ffn_b3[j], hy_ffn_freq[j],
                             hy_ffn_out[j], hy_deltas[j], hy_skip[j])
        h2 = _post_layer(h.reshape(b * l, d), mix.reshape(b * l, mix.shape[-1]), p.reshape(depth, b * l, p.shape[-1]),
                         i, w_out[i], ple_w[i], ple_gate[i], ln_g[i], ln_b[i], ple_norm[i], alpha)
        h = h2.reshape(b, l, d)
    return h
```

```python
import functools
import math

import numpy as np
import jax
import jax.numpy as jnp
from jax import lax
from jax.experimental import pallas as pl
from jax.experimental.pallas import tpu as pltpu

F32 = jnp.float32
BF16 = jnp.bfloat16

LN_EPS = 1e-5
RMS_EPS = 1e-6

DN_HEADS = 4
DN_DK = 128
DN_DV = 128
DN_WIDTH = DN_HEADS * DN_DV
DN_QKV = 2 * DN_HEADS * DN_DK + DN_WIDTH
DN_AB = 4 * DN_HEADS
DN_CONV = 5
DN_CHUNK = 64
DN_UNROLL = 4

RK_HEADS = 8
RK_HEAD = 64
RK_WIDTH = RK_HEADS * RK_HEAD
RK_LORA = 64
RK_SHIFT = 3 * RK_WIDTH + 3 * RK_LORA
RK_SHIFT_PAD = 1792
RK_GN_EPS = 64e-5
RK_CHUNK = 64
RK_UNROLL = 2

HY_ORDER = 2
HY_SHORT = 3
HY_EMB = 33
HY_FW = 64

LANE = 128
SUBLANE = 8
DFT_Q = 128
VMEM_LIMIT = 56 * 1024 * 1024

ROW_TILE = 256
SCAN_TILE = 512
HY_CT = 128

HI = lax.Precision.HIGHEST
ACT = BF16


def _cparams(sem):
    return pltpu.CompilerParams(dimension_semantics=sem, vmem_limit_bytes=VMEM_LIMIT)


_DIMS = {
    "nn": (((1,), (0,)), ((), ())),
    "nt": (((1,), (1,)), ((), ())),
    "tn": (((0,), (0,)), ((), ())),
}


def _dot16(a, b, dims="nn"):
    return lax.dot_general(a.astype(BF16), b.astype(BF16), _DIMS[dims], preferred_element_type=F32)


def _dot32(a, b, dims="nn"):
    return lax.dot_general(a.astype(F32), b.astype(F32), _DIMS[dims], precision=HI,
                           preferred_element_type=F32)


def _split2(x):
    hi = x.astype(BF16)
    lo = (x - hi.astype(F32)).astype(BF16)
    return hi, lo


def _dot_exact_rhs(x, m16):
    hi, lo = _split2(x)
    return (jnp.dot(hi, m16, preferred_element_type=F32) + jnp.dot(lo, m16, preferred_element_type=F32))


def _sigmoid(x):
    return 1.0 / (1.0 + jnp.exp(-x))


def _silu(x):
    return x * _sigmoid(x)


def _softplus(x):
    return jnp.maximum(x, 0.0) + jnp.log1p(jnp.exp(-jnp.abs(x)))


HALO = 16


SHIFT_ROWS = 128


def _tile_and_shifts(x_ref, xp_ref, xn_ref):
    assert x_ref.dtype == BF16
    t_idx = pl.program_id(1)
    t = x_ref.shape[1]
    cur16 = x_ref[0]
    zero = jnp.zeros_like(xp_ref[0])
    ext = jnp.concatenate([jnp.where(t_idx > 0, xp_ref[0], zero), cur16,
                           jnp.where(t_idx < pl.num_programs(1) - 1, xn_ref[0], zero)], axis=0)
    k = SHIFT_ROWS + 2 * HALO
    row = lax.broadcasted_iota(jnp.int32, (SHIFT_ROWS, k), 0)
    col = lax.broadcasted_iota(jnp.int32, (SHIFT_ROWS, k), 1)

    def shift(d):
        sel = jnp.where(col == row + (HALO + d), 1.0, 0.0).astype(BF16)
        return jnp.concatenate([jnp.dot(sel, ext[r0:r0 + k], preferred_element_type=F32)
                                for r0 in range(0, t, SHIFT_ROWS)], axis=0)

    return cur16.astype(F32), shift


def _halo_specs(t_rows, width, col, l_total):
    nb = t_rows // HALO
    last = l_total // HALO - 1
    prev = pl.BlockSpec((1, HALO, width), lambda b, t: (b, jnp.maximum(t * nb - 1, 0), col))
    nxt = pl.BlockSpec((1, HALO, width), lambda b, t: (b, jnp.minimum((t + 1) * nb, last), col))
    return prev, nxt


def _full(shape):
    nd = len(shape)
    return pl.BlockSpec(shape, lambda *_: (0,) * nd)


def _proj_kernel(a_ref, w_ref, *o_refs, offs):
    a = a_ref[...].astype(BF16)
    for o_ref, (lo, hi) in zip(o_refs, offs):
        o_ref[...] = jnp.dot(a, w_ref[:, lo:hi], preferred_element_type=F32).astype(o_ref.dtype)


def _project(a, w16, widths, dtypes):
    m, k = a.shape
    offs, o = [], 0
    for w in widths:
        offs.append((o, o + w))
        o += w
    n = o
    tm = ROW_TILE
    return pl.pallas_call(
        functools.partial(_proj_kernel, offs=tuple(offs)), name="project",
        grid=(m // tm,),
        in_specs=[pl.BlockSpec((tm, k), lambda i: (i, 0)), _full((k, n))],
        out_specs=[pl.BlockSpec((tm, w), lambda i: (i, 0)) for w in widths],
        out_shape=[jax.ShapeDtypeStruct((m, w), dt) for w, dt in zip(widths, dtypes)],
        compiler_params=_cparams(("parallel",)),
    )(a, w16)


def _post_kernel(h_ref, mix_ref, p_ref, wo_ref, pw_ref, pg_ref, lng_ref, lnb_ref, pn_ref, o_ref, *, alpha):
    t = alpha * h_ref[...] + jnp.dot(mix_ref[...], wo_ref[...], preferred_element_type=F32)
    mu = jnp.mean(t, axis=-1, keepdims=True)
    tc = t - mu
    var = jnp.mean(tc * tc, axis=-1, keepdims=True)
    y = tc * lax.rsqrt(var + LN_EPS) * lng_ref[...] + lnb_ref[...]
    e = jnp.dot(p_ref[...].astype(BF16), pw_ref[...], preferred_element_type=F32)
    e = e * lax.rsqrt(jnp.mean(e * e, axis=-1, keepdims=True) + RMS_EPS) * pn_ref[...]
    gate = _sigmoid(jnp.dot(y.astype(BF16), pg_ref[...], preferred_element_type=F32))
    o_ref[...] = y + gate * e


def _post_layer(h, mix, p_all, layer, w_out, ple_w, ple_gate, ln_g, ln_b, ple_norm, alpha):
    m, d = h.shape
    pd = p_all.shape[2]
    tm = ROW_TILE
    row = lambda w: pl.BlockSpec((tm, w), lambda i: (i, 0))
    p_spec = pl.BlockSpec((None, tm, pd), lambda i: (layer, i, 0))
    return pl.pallas_call(
        functools.partial(_post_kernel, alpha=alpha), name="post_layer",
        grid=(m // tm,),
        in_specs=[row(d), row(mix.shape[1]), p_spec, _full(w_out.shape), _full(ple_w.shape),
                  _full(ple_gate.shape), _full((1, d)), _full((1, d)), _full((1, d))],
        out_specs=row(d),
        out_shape=jax.ShapeDtypeStruct((m, d), F32),
        compiler_params=_cparams(("parallel",)),
    )(h, mix, p_all, w_out.astype(BF16), ple_w.astype(BF16), ple_gate.astype(BF16),
      ln_g.reshape(1, d), ln_b.reshape(1, d), ple_norm.reshape(1, d))


def _split3(x):
    t1 = x.astype(BF16)
    rem = x - t1.astype(F32)
    t2 = rem.astype(BF16)
    return t1, t2, (rem - t2.astype(F32)).astype(BF16)


def _dn_prep_kernel(x_ref, xp_ref, xn_ref, ab_ref, cw_ref, ga_ref, gbias_ref, trif_ref, trib_ref,
                    q_ref, k_ref, v_ref, gb_ref):
    cur, shift = _tile_and_shifts(x_ref, xp_ref, xn_ref)
    pad = DN_CONV // 2
    acc = cur * cw_ref[pad:pad + 1, :]
    for j in range(DN_CONV):
        if j != pad:
            acc = acc + shift(j - pad) * cw_ref[j:j + 1, :]
    y = _silu(acc)
    nqk = DN_HEADS * DN_DK
    for h in range(DN_HEADS):
        qh = y[:, h * DN_DK:(h + 1) * DN_DK]
        kh = y[:, nqk + h * DN_DK:nqk + (h + 1) * DN_DK]
        qn = lax.rsqrt(jnp.sum(qh * qh, axis=-1, keepdims=True) + RMS_EPS) * (DN_DK ** -0.5)
        kn = lax.rsqrt(jnp.sum(kh * kh, axis=-1, keepdims=True) + RMS_EPS)
        q_ref[0, :, h * DN_DK:(h + 1) * DN_DK] = (qh * qn).astype(q_ref.dtype)
        k_ref[0, :, h * DN_DK:(h + 1) * DN_DK] = (kh * kn).astype(k_ref.dtype)
    v_ref[0] = y[:, 2 * nqk:].astype(v_ref.dtype)
    ab = ab_ref[0]
    lane = lax.broadcasted_iota(jnp.int32, ab.shape, 1)
    g = ga_ref[...] * _softplus(ab + gbias_ref[...])
    parts = _split3(g)
    c = DN_CHUNK
    tri_f, tri_b = trif_ref[...], trib_ref[...]
    chunks = range(0, g.shape[0], c)
    cum_f = jnp.concatenate([sum(jnp.dot(tri_f, p[r:r + c], preferred_element_type=F32) for p in parts)
                             for r in chunks], axis=0)
    cum_b = jnp.concatenate([sum(jnp.dot(tri_b, p[r:r + c], preferred_element_type=F32) for p in parts)
                             for r in chunks], axis=0)
    gb_ref[0] = jnp.where(lane < DN_HEADS, cum_f, jnp.where(lane < 2 * DN_HEADS, cum_b, _sigmoid(ab)))


def _dn_prep(qkv, ab, conv_w, a_log, dt_bias):
    b, l, _ = qkv.shape
    t = SCAN_TILE
    ga = jnp.zeros((1, LANE), F32).at[0, :2 * DN_HEADS].set(-jnp.exp(a_log.astype(F32)).reshape(-1))
    gbias = jnp.zeros((1, LANE), F32).at[0, :2 * DN_HEADS].set(dt_bias.astype(F32).reshape(-1))
    prev, nxt = _halo_specs(t, DN_QKV, 0, l)
    blk = lambda w: pl.BlockSpec((1, t, w), lambda bi, ti: (bi, ti, 0))
    i = np.arange(DN_CHUNK)
    tri_f = jnp.asarray(i[None, :] <= i[:, None], dtype=BF16)
    tri_b = jnp.asarray(i[None, :] >= i[:, None], dtype=BF16)
    return pl.pallas_call(
        _dn_prep_kernel, name="dn_prep",
        grid=(b, l // t),
        in_specs=[blk(DN_QKV), prev, nxt, blk(LANE), _full((DN_CONV, DN_QKV)), _full((1, LANE)), _full((1, LANE)),
                  _full((DN_CHUNK, DN_CHUNK)), _full((DN_CHUNK, DN_CHUNK))],
        out_specs=[blk(DN_WIDTH), blk(DN_WIDTH), blk(DN_WIDTH), blk(LANE)],
        out_shape=[jax.ShapeDtypeStruct((b, l, DN_WIDTH), ACT)] * 3 + [jax.ShapeDtypeStruct((b, l, LANE), F32)],
        compiler_params=_cparams(("parallel", "parallel")),
    )(qkv, qkv, qkv, ab, conv_w.astype(F32), ga, gbias, tri_f, tri_b)


HALF = LANE // 2


def _pair_diag(x):
    low = lax.broadcasted_iota(jnp.int32, x.shape, 1) < HALF
    return jnp.concatenate([jnp.where(low, x, 0.0), jnp.where(low, 0.0, x)], axis=0).astype(BF16)


def _pair_inverses(xs, eye2, c):
    rs = [eye2 + x for x in xs]
    ps = [_dot16(x, _pair_diag(x)) for x in xs]
    for _ in range(int(math.log2(c)) - 2):
        zs = [_dot16(jnp.concatenate([r, p], axis=0), _pair_diag(p)) for r, p in zip(rs, ps)]
        rs = [r + z[:c] for r, z in zip(rs, zs)]
        ps = [z[c:] for z in zs]
    return [r + _dot16(r, _pair_diag(p)) for r, p in zip(rs, ps)]


def _dn_scan_kernel(qf_ref, kf_ref, vf_ref, gf_ref, qb_ref, kb_ref, vb_ref, gb_ref, of_ref, ob_ref, s_ref):
    c = DN_CHUNK
    n_sub = qf_ref.shape[1] // c

    @pl.when(pl.program_id(1) == 0)
    def _():
        s_ref[...] = jnp.zeros_like(s_ref)

    row = lax.broadcasted_iota(jnp.int32, (c, LANE), 0)
    col = lax.broadcasted_iota(jnp.int32, (c, LANE), 1) % c
    low = lax.broadcasted_iota(jnp.int32, (c, LANE), 1) < c
    eye2_b = row == col
    eye2 = eye2_b.astype(F32)
    masks2 = ((col <= row, col < row, row <= col), (col >= row, col > row, row >= col))
    neg = jnp.float32(-1e30)
    sides =((qf_ref, kf_ref, vf_ref, gf_ref, of_ref), (qb_ref, kb_ref, vb_ref, gb_ref, ob_ref))

    def diag2(a, b):
        a16, b16 = a.astype(BF16), b.astype(BF16)
        za = jnp.zeros((a16.shape[0], b16.shape[1]), BF16)
        zb = jnp.zeros((b16.shape[0], a16.shape[1]), BF16)
        return jnp.concatenate([jnp.concatenate([a16, za], axis=1), jnp.concatenate([zb, b16], axis=1)], axis=0)

    def chunk(i, carry):
        groups = []
        for u, d in [(u, d) for u in range(DN_UNROLL) for d in range(2)]:
            j = i * DN_UNROLL + u
            jj = (n_sub - 1 - j) if d else j
            rows = pl.ds(pl.multiple_of(jj * c, c), c)
            groups.append(dict(u=u, d=d, rows=rows, gb=sides[d][3][0, rows, :]))
        chains, pairs = [], []
        for grp in groups:
            d, gb = grp["d"], grp["gb"]
            for h0 in range(0, DN_HEADS, 2):
                cums = [gb[:, d * DN_HEADS + h:d * DN_HEADS + h + 1] for h in (h0, h0 + 1)]
                pairs.append(dict(a=len(chains), b=len(chains) + 1, strict=masks2[d][1], incl=masks2[d][0],
                                  cum=jnp.where(low, cums[0], cums[1])))
                for t, h in enumerate((h0, h0 + 1)):
                    last = cums[t][0:1] if d else cums[t][c - 1:c]
                    chains.append(dict(u=grp["u"], slot=d * DN_HEADS + h, rows=grp["rows"], o_ref=sides[d][4],
                                       lanes=slice(h * DN_DK, (h + 1) * DN_DK), refs=sides[d], cum=cums[t], g_tot=last,
                                       beta=gb[:, (2 + d) * DN_HEADS + h:(2 + d) * DN_HEADS + h + 1]))
        for p in pairs:
            p["cum_row"] = jnp.sum(jnp.where(eye2_b, p["cum"], 0.0), axis=0, keepdims=True)
        for p in pairs:
            p["decay"] = jnp.exp(jnp.where(p["incl"], p["cum"] - p["cum_row"], neg))
        for ch in chains:
            ch["eg"] = jnp.exp(ch["cum"])
            ch["e_tot"] = jnp.exp(ch["g_tot"])
            ch["e_tail"] = jnp.exp(ch["g_tot"] - ch["cum"])
        for ch in chains:
            q_ref, k_ref, v_ref = ch["refs"][:3]
            q = q_ref[0, ch["rows"], ch["lanes"]].astype(F32)
            k = k_ref[0, ch["rows"], ch["lanes"]].astype(F32)
            v = v_ref[0, ch["rows"], ch["lanes"]].astype(F32)
            k_beta = k * ch["beta"]
            ch.update(q_dec=q * ch["eg"], k=k, lhs=jnp.concatenate([k_beta, q], axis=0),
                      rhs=jnp.concatenate([v * ch["beta"], k_beta * ch["eg"]], axis=1), k_tail=k * ch["e_tail"])
        gram = [_dot16(jnp.concatenate([chains[p["a"]]["lhs"], chains[p["b"]]["lhs"]], axis=1),
                       diag2(chains[p["a"]]["k"], chains[p["b"]]["k"]), "nt") for p in pairs]
        kks = [g[:c] * p["decay"] for g, p in zip(gram, pairs)]
        qks = [g[c:] * p["decay"] for g, p in zip(gram, pairs)]
        t_inv = _pair_inverses([jnp.where(p["strict"], -kk, 0.0) for kk, p in zip(kks, pairs)], eye2, c)
        uw_p = [_dot16(t, diag2(chains[p["a"]]["rhs"], chains[p["b"]]["rhs"])) for t, p in zip(t_inv, pairs)]
        split = lambda xs: [part for x in xs for part in (x[:, :2 * DN_DV], x[:, 2 * DN_DV:])]
        uw = split(uw_p)
        quw = split([_dot16(qk, diag2(x[:, :2 * DN_DV], x[:, 2 * DN_DV:])) for qk, x in zip(qks, uw_p)])
        kuw = [_dot16(ch["k_tail"], x, "tn") for ch, x in zip(chains, uw)]
        o_a = [ch["q_dec"] - x[:, DN_DV:] for ch, x in zip(chains, quw)]
        state = [s_ref[slot] for slot in range(2 * DN_HEADS)]
        for u in range(DN_UNROLL):
            mine = [n for n, ch in enumerate(chains) if ch["u"] == u]
            cur = [state[chains[n]["slot"]] for n in mine]
            outs = [_dot16(o_a[n], s) + quw[n][:, :DN_DV] for n, s in zip(mine, cur)]
            news = [s * chains[n]["e_tot"] - _dot16(kuw[n][:, DN_DV:], s) + kuw[n][:, :DN_DV] for n, s in zip(mine, cur)]
            for n, o, s_new in zip(mine, outs, news):
                ch = chains[n]
                ch["o_ref"][0, ch["rows"], ch["lanes"]] = o.astype(ch["o_ref"].dtype)
                state[ch["slot"]] = s_new
        for slot in range(2 * DN_HEADS):
            s_ref[slot] = state[slot]
        return carry

    lax.fori_loop(0, n_sub // DN_UNROLL, chunk, 0)


def _dn_scan(q, k, v, gb):
    b, l, _ = q.shape
    t = SCAN_TILE
    n = l // t
    fwd = lambda w: pl.BlockSpec((1, t, w), lambda bi, ti: (bi, ti, 0))
    bwd = lambda w: pl.BlockSpec((1, t, w), lambda bi, ti: (bi, n - 1 - ti, 0))
    w = DN_WIDTH
    return pl.pallas_call(
        _dn_scan_kernel, name="dn_scan",
        grid=(b, n),
        in_specs=[fwd(w), fwd(w), fwd(w), fwd(LANE), bwd(w), bwd(w), bwd(w), bwd(LANE)],
        out_specs=[fwd(w), bwd(w)],
        out_shape=[jax.ShapeDtypeStruct((b, l, w), ACT)] * 2,
        scratch_shapes=[pltpu.VMEM((2 * DN_HEADS, DN_DK, DN_DV), F32)],
        compiler_params=_cparams(("parallel", "arbitrary")),
    )(q, k, v, gb, q, k, v, gb)


def _rk_prep_kernel(x_ref, xp_ref, xn_ref, mu_ref, w2_ref, w0_ref, a2_ref, a0_ref, kk_w_ref, ka_ref, seg_ref,
                    trif_ref, trib_ref, r_ref, k_ref, v_ref, kk_ref, a_ref, lw_ref, cum_ref):
    cur, shift = _tile_and_shifts(x_ref, xp_ref, xn_ref)
    s = cur + mu_ref[0:1, :] * (shift(-1) - cur) + mu_ref[1:2, :] * (shift(1) - cur)
    w = RK_WIDTH
    r = s[:, 0:w]
    k = s[:, w:2 * w]
    v = s[:, 2 * w:3 * w]
    wd = s[:, 3 * w:3 * w + 2 * RK_LORA]
    ad = s[:, 3 * w + 2 * RK_LORA:3 * w + 4 * RK_LORA]
    lora_w = _dot16(jnp.tanh(wd), w2_ref[...])
    lw = -math.exp(-0.5) * _sigmoid(w0_ref[...] + lora_w)
    lw_ref[0] = lw
    c = RK_CHUNK
    parts = _split3(lw)
    halves = ((trif_ref[...], slice(0, w)), (trib_ref[...], slice(w, 2 * w)))
    cum_ref[0] = jnp.concatenate(
        [jnp.concatenate([sum(jnp.dot(tri, p[r0:r0 + c, cols], preferred_element_type=F32) for p in parts)
                          for tri, cols in halves], axis=1) for r0 in range(0, lw.shape[0], c)], axis=0)
    a = _sigmoid(a0_ref[...] + _dot16(ad, a2_ref[...]))
    kk_raw = k * kk_w_ref[...]
    ssq = _dot_exact_rhs(kk_raw * kk_raw, seg_ref[...])
    kk_ref[0] = (kk_raw * lax.rsqrt(ssq + RMS_EPS)).astype(kk_ref.dtype)
    r_ref[0] = r.astype(r_ref.dtype)
    k_ref[0] = (k * (1.0 + (a - 1.0) * ka_ref[...])).astype(k_ref.dtype)
    v_ref[0] = v.astype(v_ref.dtype)
    a_ref[0] = a.astype(a_ref.dtype)


def _seg_ones(width, group):
    i = np.arange(width) // group
    return (i[:, None] == i[None, :]).astype(np.float32)


def _rk_prep(rk, mu, w0, w2, a0, a2, k_k, k_a):
    b, l, wp = rk.shape
    t = ROW_TILE
    w = RK_WIDTH
    mu_p = jnp.zeros((2, wp), F32).at[:, :RK_SHIFT].set(mu.astype(F32))
    w2cat = jnp.zeros((2 * RK_LORA, 2 * w), F32)
    w2cat = w2cat.at[:RK_LORA, :w].set(w2[0]).at[RK_LORA:, w:].set(w2[1]).astype(BF16)
    w0cat = w0.astype(F32).reshape(1, 2 * w)
    a2p = jnp.zeros((2 * RK_LORA, w), F32).at[:RK_LORA].set(a2).astype(BF16)
    idx = np.arange(RK_CHUNK)
    prev, nxt = _halo_specs(t, wp, 0, l)
    blk = lambda width: pl.BlockSpec((1, t, width), lambda bi, ti: (bi, ti, 0))
    return pl.pallas_call(
        _rk_prep_kernel, name="rk_prep",
        grid=(b, l // t),
        in_specs=[blk(wp), prev, nxt, _full((2, wp)), _full((2 * RK_LORA, 2 * w)), _full((1, 2 * w)),
                  _full((2 * RK_LORA, w)), _full((1, w)), _full((1, w)), _full((1, w)), _full((w, w)),
                  _full((RK_CHUNK, RK_CHUNK)), _full((RK_CHUNK, RK_CHUNK))],
        out_specs=[blk(w)] * 5 + [blk(2 * w)] * 2,
        out_shape=[jax.ShapeDtypeStruct((b, l, w), ACT)] * 5 + [jax.ShapeDtypeStruct((b, l, 2 * w), F32)] * 2,
        compiler_params=_cparams(("parallel", "parallel")),
    )(rk, rk, rk, mu_p, w2cat, w0cat, a2p, a0.astype(F32).reshape(1, w), k_k.astype(F32).reshape(1, w),
      k_a.astype(F32).reshape(1, w), jnp.asarray(_seg_ones(w, RK_HEAD), dtype=BF16),
      jnp.asarray(idx[None, :] <= idx[:, None], dtype=BF16), jnp.asarray(idx[None, :] >= idx[:, None], dtype=BF16))


def _rk_scan_kernel(rf_ref, kf_ref, vf_ref, kkf_ref, af_ref, lwf_ref, cumf_ref, rb_ref, kb_ref, vb_ref, kkb_ref,
                    ab_ref, lwb_ref, cumb_ref, yf_ref, yb_ref, s_ref):
    c = RK_CHUNK
    n_sub = rf_ref.shape[1] // c
    n_pair = RK_WIDTH // LANE

    @pl.when(pl.program_id(1) == 0)
    def _():
        s_ref[...] = jnp.zeros_like(s_ref)

    row = lax.broadcasted_iota(jnp.int32, (c, LANE), 0)
    col = lax.broadcasted_iota(jnp.int32, (c, LANE), 1) % RK_HEAD
    eye2 = (row == col).astype(F32)
    masks2 = ((col <= row, col < row), (col >= row, col > row))
    low_half = lax.broadcasted_iota(jnp.int32, (RK_HEAD, LANE), 1) < RK_HEAD
    same_block = (lax.broadcasted_iota(jnp.int32, (LANE, LANE), 0) // RK_HEAD) == (
        lax.broadcasted_iota(jnp.int32, (LANE, LANE), 1) // RK_HEAD)
    sides = ((rf_ref, kf_ref, vf_ref, kkf_ref, af_ref, lwf_ref, cumf_ref, yf_ref),
             (rb_ref, kb_ref, vb_ref, kkb_ref, ab_ref, lwb_ref, cumb_ref, yb_ref))

    def chunk(i, carry):
        chains = []
        for u, d in [(u, d) for u in range(RK_UNROLL) for d in range(2)]:
            r_ref, k_ref, v_ref, kk_ref, a_ref, lw_ref, cum_ref, y_ref = sides[d]
            j = i * RK_UNROLL + u
            jj = (n_sub - 1 - j) if d else j
            rows = pl.ds(pl.multiple_of(jj * c, c), c)
            lw = lw_ref[0, rows, :]
            cum = cum_ref[0, rows, :]
            tot = cum[0:1] if d else cum[c - 1:c]
            e_neg = jnp.exp(-cum)
            e_tail = jnp.exp(tot - cum)
            e_tot = jnp.exp(tot)
            k = k_ref[0, rows, :].astype(F32)
            v = v_ref[0, rows, :].astype(F32)
            kk = kk_ref[0, rows, :].astype(F32)
            b_vec = kk * a_ref[0, rows, :].astype(F32)
            ra = r_ref[0, rows, :].astype(F32) * jnp.exp(cum)
            aa = -kk * jnp.exp(cum - lw)
            bb = b_vec * e_neg
            kb = k * e_neg
            bt = b_vec * e_tail
            kt = k * e_tail
            for g in range(n_pair):
                lanes = slice(g * LANE, (g + 1) * LANE)
                chains.append(dict(
                    u=u, slot=d * n_pair + g, rows=rows, lanes=lanes, y_ref=y_ref, incl=masks2[d][0],
                    strict=masks2[d][1],
                    aa=aa[:, lanes], ra=ra[:, lanes], bb=bb[:, lanes], kb=kb[:, lanes], v=v[:, lanes],
                    bt=bt[:, lanes], kt=kt[:, lanes], e_tot=e_tot[:, lanes]))
        for ch in chains:
            ch["lhs"] = jnp.concatenate([ch["aa"], ch["ra"]], axis=0)
        gb = [_dot16(ch["lhs"], _pair_diag(ch["bb"]), "nt") for ch in chains]
        gk = [_dot16(ch["lhs"], _pair_diag(ch["kb"]), "nt") for ch in chains]
        a_ab = [jnp.where(ch["strict"], g[:c], 0.0) for g, ch in zip(gb, chains)]
        m_rb = [jnp.where(ch["incl"], g[c:], 0.0) for g, ch in zip(gb, chains)]
        akrk = [jnp.concatenate([jnp.where(ch["strict"], g[:c], 0.0), jnp.where(ch["incl"], g[c:], 0.0)], axis=0)
                for g, ch in zip(gk, chains)]
        avyv = [_dot16(m, _pair_diag(ch["v"])) for m, ch in zip(akrk, chains)]
        t_inv = _pair_inverses(a_ab, eye2, c)
        tq = [_dot16(t, jnp.concatenate([_pair_diag(ch["aa"]), _pair_diag(x[:c])], axis=1))
              for t, ch, x in zip(t_inv, chains, avyv)]
        yy = [_dot16(m, jnp.concatenate([_pair_diag(x[:, :LANE]), _pair_diag(x[:, LANE:])], axis=1))
              for m, x in zip(m_rb, tq)]
        ya = [ch["ra"] + y[:, :LANE] for ch, y in zip(chains, yy)]
        yb = [y[:, LANE:] + x[c:] for y, x in zip(yy, avyv)]
        wm = [jnp.where(same_block, _dot16(x[:, :LANE], ch["bt"], "tn"), 0.0) for x, ch in zip(tq, chains)]
        hc_full = [_dot16(jnp.concatenate([x[:, LANE:], ch["v"]], axis=0),
                          jnp.concatenate([ch["bt"], ch["kt"]], axis=0), "tn") for x, ch in zip(tq, chains)]
        hc = [jnp.where(low_half, x[:RK_HEAD], x[RK_HEAD:]) for x in hc_full]
        state = [s_ref[slot] for slot in range(2 * n_pair)]
        for u in range(RK_UNROLL):
            mine = [n for n, ch in enumerate(chains) if ch["u"] == u]
            cur = [state[chains[n]["slot"]] for n in mine]
            outs = [_dot16(ya[n], _pair_diag(s), "nt") + yb[n] for n, s in zip(mine, cur)]
            news = [s * chains[n]["e_tot"] + _dot16(s, wm[n]) + hc[n] for n, s in zip(mine, cur)]
            for n, y, s_new in zip(mine, outs, news):
                ch = chains[n]
                ch["y_ref"][0, ch["rows"], ch["lanes"]] = y.astype(ch["y_ref"].dtype)
                state[ch["slot"]] = s_new
        for slot in range(2 * n_pair):
            s_ref[slot] = state[slot]
        return carry

    lax.fori_loop(0, n_sub // RK_UNROLL, chunk, 0)


def _rk_scan(r, k, v, kk, a, lw, cum):
    b, l, w = r.shape
    t = ROW_TILE
    n = l // t
    fwd = pl.BlockSpec((1, t, w), lambda bi, ti: (bi, ti, 0))
    bwd = pl.BlockSpec((1, t, w), lambda bi, ti: (bi, n - 1 - ti, 0))
    bwd_lw = pl.BlockSpec((1, t, w), lambda bi, ti: (bi, n - 1 - ti, 1))
    return pl.pallas_call(
        _rk_scan_kernel, name="rk_scan",
        grid=(b, n),
        in_specs=[fwd] * 7 + [bwd] * 5 + [bwd_lw] * 2,
        out_specs=[fwd, bwd],
        out_shape=[jax.ShapeDtypeStruct((b, l, w), ACT)] * 2,
        scratch_shapes=[pltpu.VMEM((2 * RK_WIDTH // LANE, RK_HEAD, LANE), F32)],
        compiler_params=_cparams(("parallel", "arbitrary")),
    )(r, k, v, kk, a, lw, cum, r, k, v, kk, a, lw, cum)


def _even_mix_kernel(of_ref, ob_ref, dg_ref, dnw_ref, yf_ref, yb_ref, r_ref, k_ref, v_ref, rg_ref,
                     rk_ref, lnw_ref, lnb_ref, segm_ref, seg1_ref, o_ref):
    f32 = lambda ref: ref[0].astype(F32)
    o = f32(of_ref) + f32(ob_ref)
    gate = _silu(f32(dg_ref))
    for h in range(DN_HEADS):
        lanes = slice(h * DN_DV, (h + 1) * DN_DV)
        oh = o[:, lanes]
        ms = jnp.mean(oh * oh, axis=-1, keepdims=True)
        o_ref[0, :, lanes] = (oh * lax.rsqrt(ms + RMS_EPS) * dnw_ref[...] * gate[:, lanes]).astype(o_ref.dtype)
    wkv = f32(yf_ref) + f32(yb_ref)
    mean = _dot_exact_rhs(wkv, segm_ref[...])
    cen = wkv - mean
    var = _dot_exact_rhs(cen * cen, segm_ref[...])
    wkv = cen * lax.rsqrt(var + RK_GN_EPS) * lnw_ref[...] + lnb_ref[...]
    bonus = _dot_exact_rhs(f32(r_ref) * f32(k_ref) * rk_ref[...], seg1_ref[...]) * f32(v_ref)
    o_ref[0, :, DN_WIDTH:] = ((wkv + bonus) * _silu(f32(rg_ref))).astype(o_ref.dtype)


def _even_mix(o_f, o_b, dn_gate, dn_norm, y_f, y_b, r, k, v, rk_gate, r_k, ln_w, ln_b):
    b, l, _ = o_f.shape
    t = ROW_TILE
    w = RK_WIDTH
    blk = lambda width: pl.BlockSpec((1, t, width), lambda bi, ti: (bi, ti, 0))
    seg1 = jnp.asarray(_seg_ones(w, RK_HEAD), dtype=BF16)
    segm = jnp.asarray(_seg_ones(w, RK_HEAD) / RK_HEAD, dtype=BF16)
    return pl.pallas_call(
        _even_mix_kernel, name="even_mix",
        grid=(b, l // t),
        in_specs=[blk(DN_WIDTH), blk(DN_WIDTH), blk(DN_WIDTH), _full((1, DN_DV)),
                  blk(w), blk(w), blk(w), blk(w), blk(w), blk(w),
                  _full((1, w)), _full((1, w)), _full((1, w)), _full((w, w)), _full((w, w))],
        out_specs=blk(DN_WIDTH + w),
        out_shape=jax.ShapeDtypeStruct((b, l, DN_WIDTH + w), ACT),
        compiler_params=_cparams(("parallel", "parallel")),
    )(o_f, o_b, dn_gate, dn_norm.astype(F32).reshape(1, DN_DV), y_f, y_b, r, k, v, rk_gate,
      r_k.astype(F32).reshape(1, w), ln_w.astype(F32).reshape(1, w), ln_b.astype(F32).reshape(1, w), segm, seg1)


def _even_layer(h, w_in, dn_conv, dn_a_log, dn_dt_bias, dn_norm, rk_mu, rk_w0, rk_w2, rk_a0, rk_a2,
                rk_k_k, rk_k_a, rk_r_k, rk_ln_w, rk_ln_b):
    b, l, d = h.shape
    s0 = DN_QKV
    s1 = s0 + DN_AB
    s2 = s1 + DN_WIDTH
    s3 = s2 + RK_SHIFT
    pad = lambda m, width: jnp.pad(m, ((0, 0), (0, width - m.shape[1])))
    widths = (DN_QKV, LANE, DN_WIDTH, RK_SHIFT_PAD, RK_WIDTH)
    w16 = jnp.concatenate([w_in[:, :s0], pad(w_in[:, s0:s1], LANE), w_in[:, s1:s2],
                           pad(w_in[:, s2:s3], RK_SHIFT_PAD), w_in[:, s3:]], axis=1).astype(BF16)
    qkv, ab, dn_gate, rk, rk_gate = _project(h.reshape(b * l, d), w16, widths, (ACT, F32, ACT, ACT, ACT))
    r3 = lambda m: m.reshape(b, l, m.shape[-1])
    q, k, v, gb = _dn_prep(r3(qkv), r3(ab), dn_conv, dn_a_log, dn_dt_bias)
    o_f, o_b = _dn_scan(q, k, v, gb)
    r, kr, vr, kk, a, lw, cum = _rk_prep(r3(rk), rk_mu, rk_w0, rk_w2, rk_a0, rk_a2, rk_k_k, rk_k_a)
    y_f, y_b = _rk_scan(r, kr, vr, kk, a, lw, cum)
    return _even_mix(o_f, o_b, r3(dn_gate), dn_norm, y_f, y_b, r, kr, vr, r3(rk_gate),
                     rk_r_k, rk_ln_w, rk_ln_b)


def _dft_geometry(l):
    nf = 2 * l
    p = nf // DFT_Q
    n1 = p // 2
    k1 = p // 2 + 1
    k1p = -(-k1 // SUBLANE) * SUBLANE
    return nf, p, n1, k1, k1p


@functools.lru_cache(maxsize=None)
def _dft_tables(l):
    nf, p, n1c, k1c, k1p = _dft_geometry(l)
    q = DFT_Q
    n2 = np.arange(q)[:, None, None]
    k1 = np.arange(k1c)[None, :, None]
    n1 = np.arange(n1c)[None, None, :]
    ph = -2.0 * np.pi * (((n1 * k1) % p) / p + ((n2 * k1) % nf) / nf)
    fa = np.zeros((q, 2 * k1p, n1c))
    fa[:, :k1c] = np.cos(ph)
    fa[:, k1p:k1p + k1c] = np.sin(ph)
    wgt = np.full((k1c,), 2.0)
    wgt[0] = 1.0
    wgt[-1] = 1.0
    th = -ph.transpose(0, 2, 1)
    gd = np.zeros((q, n1c, 2 * k1p))
    gd[:, :, :k1c] = np.cos(th) * wgt / nf
    gd[:, :, k1p:k1p + k1c] = -np.sin(th) * wgt / nf
    a = np.arange(q)
    ang = -2.0 * np.pi * ((a[:, None] * a[None, :]) % q) / q
    cr, ci = np.cos(ang), np.sin(ang)
    fb = np.block([[cr, -ci], [ci, cr]])
    fc = np.block([[cr, ci], [-ci, cr]])

    return tuple(m.astype(np.float32).astype(BF16) for m in (fa, fb, fc, gd))


def _fdot(f, x):
    return jnp.dot(f, x.astype(BF16), preferred_element_type=F32)


DFT_GROUP_N2 = 8
DFT_GROUP_K1 = 8
DFT_PITCH = DFT_Q + SUBLANE


def _stage_a(src, y_re, y_im, fa, geo):
    nf, p, n1c, k1c, k1p = geo
    g = DFT_GROUP_N2

    def body(i, carry):
        n2s = [i * g + t for t in range(g)]
        slabs = [src[pl.ds(n2, n1c, stride=DFT_PITCH), :] for n2 in n2s]
        outs = [_fdot(fa[n2], slab) for n2, slab in zip(n2s, slabs)]
        for n2, out in zip(n2s, outs):
            y_re[pl.ds(n2, k1p, stride=DFT_PITCH), :] = out[:k1p]
            y_im[pl.ds(n2, k1p, stride=DFT_PITCH), :] = out[k1p:]
        return carry

    lax.fori_loop(0, DFT_Q // g, body, 0)


def _fdot_pairs(f, xs):
    outs = []
    for a, b in zip(xs[0::2], xs[1::2]):
        z = _fdot(f, jnp.concatenate([a, b], axis=1))
        outs += [z[:, :a.shape[1]], z[:, a.shape[1]:]]
    return outs


def _stage_b(y_re, y_im, i, fb):
    g = DFT_GROUP_K1
    k1s = [i * g + t for t in range(g)]
    rows = [pl.ds(pl.multiple_of(k1 * DFT_PITCH, SUBLANE), DFT_Q) for k1 in k1s]
    ws = [jnp.concatenate([y_re[r, :], y_im[r, :]], axis=0) for r in rows]
    return k1s, rows, _fdot_pairs(fb[...], ws)


def _hy_conv_kernel(u_ref, m_ref, skip_ref, hr_ref, hi_ref, fa, fb, fc, gd, o_ref, pad, y_re, y_im, *, geo):
    nf, p, n1c, k1c, k1p = geo
    for n1 in range(n1c):
        pad[n1 * DFT_PITCH:n1 * DFT_PITCH + DFT_Q, :] = u_ref[0, n1 * DFT_Q:(n1 + 1) * DFT_Q, :].astype(F32)
    _stage_a(pad, y_re, y_im, fa, geo)

    def mid(i, carry):
        k1s, rows, zs = _stage_b(y_re, y_im, i, fb)
        prods = []
        for k1, z in zip(k1s, zs):
            zr, zi = z[:DFT_Q], z[DFT_Q:]
            hrows = pl.ds(pl.multiple_of(k1 * DFT_Q, DFT_Q), DFT_Q)
            hr = hr_ref[hrows, :]
            hi = hi_ref[hrows, :]
            prods.append(jnp.concatenate([zr * hr - zi * hi, zr * hi + zi * hr], axis=0))
        outs = _fdot_pairs(fc[...], prods)
        for r, a in zip(rows, outs):
            y_re[r, :] = a[:DFT_Q]
            y_im[r, :] = a[DFT_Q:]
        return carry

    lax.fori_loop(0, k1p // DFT_GROUP_K1, mid, 0)

    def last(i, carry):
        n2s = [i * DFT_GROUP_N2 + t for t in range(DFT_GROUP_N2)]
        ins = [jnp.concatenate([y_re[pl.ds(n2, k1p, stride=DFT_PITCH), :], y_im[pl.ds(n2, k1p, stride=DFT_PITCH), :]],
                               axis=0) for n2 in n2s]
        outs = [_fdot(gd[n2], a) for n2, a in zip(n2s, ins)]
        for n2, out in zip(n2s, outs):
            pad[pl.ds(n2, n1c, stride=DFT_PITCH), :] = out
        return carry

    lax.fori_loop(0, DFT_Q // DFT_GROUP_N2, last, 0)

    skip = skip_ref[...]
    for n1 in range(n1c):
        rows = slice(n1 * DFT_Q, (n1 + 1) * DFT_Q)
        conv = pad[n1 * DFT_PITCH:n1 * DFT_PITCH + DFT_Q, :]
        o_ref[0, rows, :] = (m_ref[0, rows, :].astype(F32)
                             * (conv + skip * u_ref[0, rows, :].astype(F32))).astype(o_ref.dtype)


def _single(shape, index_map):
    return pl.BlockSpec(shape, index_map, pipeline_mode=pl.Buffered(1))


def _hy_conv(u, mult, skip, h_re, h_im, order, tables):
    b, l, ch = u.shape
    geo = _dft_geometry(l)
    nf, p, n1c, k1c, k1p = geo
    ct = HY_CT
    consts = [jnp.asarray(t) for t in tables]
    seq = pl.BlockSpec((1, l, ct), lambda ci, bi: (bi, 0, ci))
    spec = _single((None, k1p * DFT_Q, ct), lambda ci, bi: (order, 0, ci))
    cspecs = [_single(c.shape, (lambda ci, bi, nd=c.ndim: (0,) * nd)) for c in consts]
    ysc = pltpu.VMEM((k1p * DFT_PITCH, ct), F32)
    return pl.pallas_call(
        functools.partial(_hy_conv_kernel, geo=geo), name="hy_conv",
        grid=(ch // ct, b),
        in_specs=[seq, seq, pl.BlockSpec((1, ct), lambda ci, bi: (0, ci)), spec, spec] + cspecs,
        out_specs=seq,
        out_shape=jax.ShapeDtypeStruct((b, l, ch), ACT),
        scratch_shapes=[pltpu.VMEM((n1c * DFT_PITCH, ct), F32), ysc, ysc],
        compiler_params=_cparams(("parallel", "parallel")),
    )(u, mult, skip, h_re, h_im, *consts)


def _hy_mlp_kernel(f_ref, w1_ref, b1_ref, w2_ref, b2_ref, w3_ref, b3_ref, fr_ref, o_ref):
    fr = fr_ref[...]
    hdn = jnp.sin(fr * (_dot32(f_ref[...], w1_ref[...]) + b1_ref[...]))
    hdn = jnp.sin(fr * (_dot32(hdn, w2_ref[...]) + b2_ref[...]))
    o_ref[...] = jnp.sin(fr * (_dot32(hdn, w3_ref[...]) + b3_ref[...]))


def _hy_mlp(feats, w1, b1, w2, b2, w3, b3, freq):
    l = feats.shape[0]
    t = min(l, 1024)
    fw = HY_FW
    row = lambda a: a.astype(F32).reshape(1, fw)
    w1p = jnp.zeros((LANE, fw), F32).at[:HY_EMB].set(w1.astype(F32))
    return pl.pallas_call(
        _hy_mlp_kernel, name="hy_mlp",
        grid=(l // t,),
        in_specs=[pl.BlockSpec((t, LANE), lambda i: (i, 0)), _full((LANE, fw)), _full((1, fw)), _full((fw, fw)),
                  _full((1, fw)), _full((fw, fw)), _full((1, fw)), _full((1, fw))],
        out_specs=pl.BlockSpec((t, fw), lambda i: (i, 0)),
        out_shape=jax.ShapeDtypeStruct((l, fw), F32),
        compiler_params=_cparams(("parallel",)),
    )(feats, w1p, row(b1), w2.astype(F32), row(b2), w3.astype(F32), row(b3), row(freq))


def _hy_filter_kernel(hdn_ref, wf_ref, wb_ref, df_ref, db_ref, fa, fb,
                      hr_ref, hi_ref, filt, yr_f, yi_f, yr_b, yi_b, *, geo):
    nf, p, n1c, k1c, k1p = geo

    def build(w_ref, d_ref, drop_first):
        grp = 4

        def body(i, acc):
            n1s = [i * grp + t for t in range(grp)]
            hxs = [hdn_ref[pl.ds(pl.multiple_of(n1 * DFT_Q, DFT_Q), DFT_Q), :] for n1 in n1s]
            raw = [_dot32(hx, w_ref[0]) for hx in hxs]
            for n1, hx, hv in zip(n1s, hxs, raw):
                hv = hv * jnp.exp(-hx[:, HY_FW:HY_FW + 1] * jnp.abs(d_ref[0]))
                if drop_first:
                    pos = lax.broadcasted_iota(jnp.int32, hv.shape, 0) + n1 * DFT_Q
                    hv = jnp.where(pos == 0, 0.0, hv)
                filt[pl.ds(pl.multiple_of(n1 * DFT_PITCH, SUBLANE), DFT_Q), :] = hv
                acc = acc + jnp.sum(jnp.abs(hv), axis=0, keepdims=True)
            return acc

        return lax.fori_loop(0, n1c // grp, body, jnp.zeros((1, filt.shape[1]), F32))

    l1 = build(wf_ref, df_ref, False)
    _stage_a(filt, yr_f, yi_f, fa, geo)
    l1 = l1 + build(wb_ref, db_ref, True)
    _stage_a(filt, yr_b, yi_b, fa, geo)
    inv = 1.0 / (l1 + RMS_EPS)

    def mid(i, carry):
        k1s, _, zf = _stage_b(yr_f, yi_f, i, fb)
        _, _, zb = _stage_b(yr_b, yi_b, i, fb)
        for k1, f, b in zip(k1s, zf, zb):
            r = pl.ds(pl.multiple_of(k1 * DFT_Q, DFT_Q), DFT_Q)
            hr_ref[0, r, :] = (f[:DFT_Q] + b[:DFT_Q]) * inv
            hi_ref[0, r, :] = (f[DFT_Q:] - b[DFT_Q:]) * inv
        return carry

    lax.fori_loop(0, k1p // DFT_GROUP_K1, mid, 0)


def _hy_filters(hdn, tcol, w_out, deltas, tables, l):
    geo = _dft_geometry(l)
    nf, p, n1c, k1c, k1p = geo
    ch = w_out.shape[1] // (2 * HY_ORDER)
    ct = HY_CT
    nct = ch // ct
    hdn_x = jnp.concatenate([hdn, tcol, jnp.zeros((l, LANE - HY_FW - 1), F32)], axis=1)
    w4 = w_out.astype(F32).reshape(HY_FW, 2 * HY_ORDER, ch).transpose(1, 0, 2)
    w4 = jnp.pad(w4, ((0, 0), (0, LANE - HY_FW), (0, 0)))
    d4 = deltas.astype(F32).reshape(2 * HY_ORDER, 1, ch)
    consts = [jnp.asarray(t) for t in tables[:2]]
    cspecs = [_single(c.shape, (lambda o, ci, nd=c.ndim: (0,) * nd)) for c in consts]
    wspec = lambda d: pl.BlockSpec((1, LANE, ct), lambda o, ci: (2 * o + d, 0, ci))
    dspec = lambda d: pl.BlockSpec((1, 1, ct), lambda o, ci: (2 * o + d, 0, ci))
    ospec = pl.BlockSpec((1, k1p * DFT_Q, ct), lambda o, ci: (o, 0, ci))
    ysc = pltpu.VMEM((k1p * DFT_PITCH, ct), F32)
    return pl.pallas_call(
        functools.partial(_hy_filter_kernel, geo=geo), name="hy_filters",
        grid=(HY_ORDER, nct),
        in_specs=[_single((l, LANE), lambda o, ci: (0, 0)), wspec(0), wspec(1), dspec(0), dspec(1)] + cspecs,
        out_specs=[ospec, ospec],
        out_shape=[jax.ShapeDtypeStruct((HY_ORDER, k1p * DFT_Q, ch), F32)] * 2,
        scratch_shapes=[pltpu.VMEM((n1c * DFT_PITCH, ct), F32), ysc, ysc, ysc, ysc],
        compiler_params=_cparams(("parallel", "parallel")),
    )(hdn_x, w4, w4, d4, d4, *consts)


def _position_features(l):
    bands = (HY_EMB - 1) // 2
    t = jnp.linspace(0.0, 1.0, l, dtype=F32)[:, None]
    f = jnp.linspace(1e-4, bands - 1, bands, dtype=F32)[None, :]
    ang = (2.0 * math.pi / l) * jnp.arange(l, dtype=F32)[:, None] * f
    feats = jnp.concatenate([t, jnp.cos(ang), -jnp.sin(ang)], axis=-1)
    return jnp.pad(feats, ((0, 0), (0, LANE - HY_EMB))), t


def _hy_prep_kernel(x_ref, xp_ref, xn_ref, g_ref, cw_ref, cb_ref, x1_ref, m2_ref, v_ref):
    cur, shift = _tile_and_shifts(x_ref, xp_ref, xn_ref)
    y = shift(-1) * cw_ref[0:1, :] + cur * cw_ref[1:2, :] + shift(1) * cw_ref[2:3, :] + cb_ref[...]
    c = g_ref.shape[2]
    x1_ref[0] = y[:, :c].astype(x1_ref.dtype)
    m2_ref[0] = (y[:, c:2 * c] * _silu(g_ref[0].astype(F32))).astype(m2_ref.dtype)
    v_ref[0] = y[:, 2 * c:].astype(v_ref.dtype)


def _hy_prep(xv, gate, conv_w, conv_b):
    b, l, w3 = xv.shape
    c = gate.shape[2]
    t = ROW_TILE
    prev, nxt = _halo_specs(t, w3, 0, l)
    blk = lambda width: pl.BlockSpec((1, t, width), lambda bi, ti: (bi, ti, 0))
    return pl.pallas_call(
        _hy_prep_kernel, name="hy_prep",
        grid=(b, l // t),
        in_specs=[blk(w3), prev, nxt, blk(c), _full((HY_SHORT, w3)), _full((1, w3))],
        out_specs=[blk(c)] * 3,
        out_shape=[jax.ShapeDtypeStruct((b, l, c), ACT)] * 3,
        compiler_params=_cparams(("parallel", "parallel")),
    )(xv, xv, xv, gate, conv_w.astype(F32), conv_b.astype(F32).reshape(1, w3))


def _odd_layer(h, w_in, conv_w, conv_b, f_w1, f_b1, f_w2, f_b2, f_w3, f_b3, f_freq, f_out, deltas, skip):
    b, l, d = h.shape
    c = skip.shape[1]
    xv, gate = _project(h.reshape(b * l, d), w_in.astype(BF16), (3 * c, c), (ACT, ACT))
    x1, m2, v = _hy_prep(xv.reshape(b, l, 3 * c), gate.reshape(b, l, c), conv_w, conv_b)
    tables = _dft_tables(l)
    feats, tcol = _position_features(l)
    hdn = _hy_mlp(feats, f_w1, f_b1, f_w2, f_b2, f_w3, f_b3, f_freq)
    h_re, h_im = _hy_filters(hdn, tcol, f_out, deltas, tables, l)
    skip = skip.astype(F32)
    z = _hy_conv(v, x1, skip[0:1], h_re, h_im, 0, tables)
    return _hy_conv(z, m2, skip[1:2], h_re, h_im, 1, tables)


def kernel(x, p, even_w_in, dn_conv, dn_a_log, dn_dt_bias, dn_norm, rk_mu, rk_w0, rk_w2, rk_a0, rk_a2, rk_k_k, rk_k_a, rk_r_k, rk_ln_w, rk_ln_b, odd_w_in, hy_conv_w, hy_conv_b, hy_ffn_w1, hy_ffn_b1, hy_ffn_w2, hy_ffn_b2, hy_ffn_w3, hy_ffn_b3, hy_ffn_freq, hy_ffn_out, hy_deltas, hy_skip, w_out, ln_g, ln_b, ple_w, ple_norm, ple_gate):
    b, l, d = x.shape
    depth = p.shape[0]
    alpha = (2.0 * depth) ** 0.25
    h = x
    for i in range(depth):
        j = i // 2
        if i % 2 == 0:
            mix = _even_layer(h, even_w_in[j], dn_conv[j], dn_a_log[j], dn_dt_bias[j], dn_norm[j], rk_mu[j],
                              rk_w0[j], rk_w2[j], rk_a0[j], rk_a2[j], rk_k_k[j].reshape(-1), rk_k_a[j].reshape(-1),
                              rk_r_k[j].reshape(-1), rk_ln_w[j], rk_ln_b[j])
        else:
            mix = _odd_layer(h, odd_w_in[j], hy_conv_w[j], hy_conv_b[j], hy_ffn_w1[j], hy_ffn_b1[j],
                             hy_ffn_w2[j], hy_ffn_b2[j], hy_ffn_w3[j], hy_ffn_b3[j], hy_ffn_freq[j],
                             hy_ffn_out[j], hy_deltas[j], hy_skip[j])
        h2 = _post_layer(h.reshape(b * l, d), mix.reshape(b * l, mix.shape[-1]), p.reshape(depth, b * l, p.shape[-1]),
                         i, w_out[i], ple_w[i], ple_gate[i], ln_g[i], ln_b[i], ple_norm[i], alpha)
        h = h2.reshape(b, l, d)
    return h
```

```python
import functools
import math

import numpy as np
import jax
import jax.numpy as jnp
from jax import lax
from jax.experimental import pallas as pl
from jax.experimental.pallas import tpu as pltpu

F32 = jnp.float32
BF16 = jnp.bfloat16

LN_EPS = 1e-5
RMS_EPS = 1e-6

DN_HEADS = 4
DN_DK = 128
DN_DV = 128
DN_WIDTH = DN_HEADS * DN_DV
DN_QKV = 2 * DN_HEADS * DN_DK + DN_WIDTH
DN_AB = 4 * DN_HEADS
DN_CONV = 5
DN_CHUNK = 64
DN_UNROLL = 4

RK_HEADS = 8
RK_HEAD = 64
RK_WIDTH = RK_HEADS * RK_HEAD
RK_LORA = 64
RK_SHIFT = 3 * RK_WIDTH + 3 * RK_LORA
RK_SHIFT_PAD = 1792
RK_GN_EPS = 64e-5
RK_CHUNK = 64
RK_UNROLL = 2

HY_ORDER = 2
HY_SHORT = 3
HY_EMB = 33
HY_FW = 64

LANE = 128
SUBLANE = 8
DFT_Q = 128
VMEM_LIMIT = 56 * 1024 * 1024

ROW_TILE = 512
SCAN_TILE = 512
HY_CT = 128

HI = lax.Precision.HIGHEST
ACT = BF16


def _cparams(sem):
    return pltpu.CompilerParams(dimension_semantics=sem, vmem_limit_bytes=VMEM_LIMIT)


_DIMS = {
    "nn": (((1,), (0,)), ((), ())),
    "nt": (((1,), (1,)), ((), ())),
    "tn": (((0,), (0,)), ((), ())),
}


def _dot16(a, b, dims="nn"):
    return lax.dot_general(a.astype(BF16), b.astype(BF16), _DIMS[dims], preferred_element_type=F32)


def _dot32(a, b, dims="nn"):
    return lax.dot_general(a.astype(F32), b.astype(F32), _DIMS[dims], precision=HI,
                           preferred_element_type=F32)


def _split2(x):
    hi = x.astype(BF16)
    lo = (x - hi.astype(F32)).astype(BF16)
    return hi, lo


def _dot_exact_rhs(x, m16):
    hi, lo = _split2(x)
    return (jnp.dot(hi, m16, preferred_element_type=F32) + jnp.dot(lo, m16, preferred_element_type=F32))


def _sigmoid(x):
    return 1.0 / (1.0 + jnp.exp(-x))


def _silu(x):
    return x * _sigmoid(x)


def _softplus(x):
    return jnp.maximum(x, 0.0) + jnp.log1p(jnp.exp(-jnp.abs(x)))


HALO = 16


SHIFT_ROWS = 128


def _tile_and_shifts(x_ref, xp_ref, xn_ref):
    assert x_ref.dtype == BF16
    t_idx = pl.program_id(1)
    t = x_ref.shape[1]
    cur16 = x_ref[0]
    zero = jnp.zeros_like(xp_ref[0])
    ext = jnp.concatenate([jnp.where(t_idx > 0, xp_ref[0], zero), cur16,
                           jnp.where(t_idx < pl.num_programs(1) - 1, xn_ref[0], zero)], axis=0)
    k = SHIFT_ROWS + 2 * HALO
    row = lax.broadcasted_iota(jnp.int32, (SHIFT_ROWS, k), 0)
    col = lax.broadcasted_iota(jnp.int32, (SHIFT_ROWS, k), 1)

    def shift(d):
        sel = jnp.where(col == row + (HALO + d), 1.0, 0.0).astype(BF16)
        return jnp.concatenate([jnp.dot(sel, ext[r0:r0 + k], preferred_element_type=F32)
                                for r0 in range(0, t, SHIFT_ROWS)], axis=0)

    return cur16.astype(F32), shift


def _halo_specs(t_rows, width, col, l_total):
    nb = t_rows // HALO
    last = l_total // HALO - 1
    prev = pl.BlockSpec((1, HALO, width), lambda b, t: (b, jnp.maximum(t * nb - 1, 0), col))
    nxt = pl.BlockSpec((1, HALO, width), lambda b, t: (b, jnp.minimum((t + 1) * nb, last), col))
    return prev, nxt


def _full(shape):
    nd = len(shape)
    return pl.BlockSpec(shape, lambda *_: (0,) * nd)


def _proj_kernel(a_ref, w_ref, *o_refs, offs):
    a = a_ref[...].astype(BF16)
    for o_ref, (lo, hi) in zip(o_refs, offs):
        o_ref[...] = jnp.dot(a, w_ref[:, lo:hi], preferred_element_type=F32).astype(o_ref.dtype)


def _project(a, w16, widths, dtypes):
    m, k = a.shape
    offs, o = [], 0
    for w in widths:
        offs.append((o, o + w))
        o += w
    n = o
    tm = ROW_TILE
    return pl.pallas_call(
        functools.partial(_proj_kernel, offs=tuple(offs)), name="project",
        grid=(m // tm,),
        in_specs=[pl.BlockSpec((tm, k), lambda i: (i, 0)), _full((k, n))],
        out_specs=[pl.BlockSpec((tm, w), lambda i: (i, 0)) for w in widths],
        out_shape=[jax.ShapeDtypeStruct((m, w), dt) for w, dt in zip(widths, dtypes)],
        compiler_params=_cparams(("parallel",)),
    )(a, w16)


def _post_kernel(h_ref, mix_ref, p_ref, wo_ref, pw_ref, pg_ref, lng_ref, lnb_ref, pn_ref, o_ref, *, alpha):
    t = alpha * h_ref[...] + jnp.dot(mix_ref[...], wo_ref[...], preferred_element_type=F32)
    mu = jnp.mean(t, axis=-1, keepdims=True)
    tc = t - mu
    var = jnp.mean(tc * tc, axis=-1, keepdims=True)
    y = tc * lax.rsqrt(var + LN_EPS) * lng_ref[...] + lnb_ref[...]
    e = jnp.dot(p_ref[...].astype(BF16), pw_ref[...], preferred_element_type=F32)
    e = e * lax.rsqrt(jnp.mean(e * e, axis=-1, keepdims=True) + RMS_EPS) * pn_ref[...]
    gate = _sigmoid(jnp.dot(y.astype(BF16), pg_ref[...], preferred_element_type=F32))
    o_ref[...] = y + gate * e


def _post_layer(h, mix, p_all, layer, w_out, ple_w, ple_gate, ln_g, ln_b, ple_norm, alpha):
    m, d = h.shape
    pd = p_all.shape[2]
    tm = ROW_TILE
    row = lambda w: pl.BlockSpec((tm, w), lambda i: (i, 0))
    p_spec = pl.BlockSpec((None, tm, pd), lambda i: (layer, i, 0))
    return pl.pallas_call(
        functools.partial(_post_kernel, alpha=alpha), name="post_layer",
        grid=(m // tm,),
        in_specs=[row(d), row(mix.shape[1]), p_spec, _full(w_out.shape), _full(ple_w.shape),
                  _full(ple_gate.shape), _full((1, d)), _full((1, d)), _full((1, d))],
        out_specs=row(d),
        out_shape=jax.ShapeDtypeStruct((m, d), F32),
        compiler_params=_cparams(("parallel",)),
    )(h, mix, p_all, w_out.astype(BF16), ple_w.astype(BF16), ple_gate.astype(BF16),
      ln_g.reshape(1, d), ln_b.reshape(1, d), ple_norm.reshape(1, d))


def _split3(x):
    t1 = x.astype(BF16)
    rem = x - t1.astype(F32)
    t2 = rem.astype(BF16)
    return t1, t2, (rem - t2.astype(F32)).astype(BF16)


def _dn_prep_kernel(x_ref, xp_ref, xn_ref, ab_ref, cw_ref, ga_ref, gbias_ref, trif_ref, trib_ref,
                    q_ref, k_ref, v_ref, gb_ref):
    cur, shift = _tile_and_shifts(x_ref, xp_ref, xn_ref)
    pad = DN_CONV // 2
    acc = cur * cw_ref[pad:pad + 1, :]
    for j in range(DN_CONV):
        if j != pad:
            acc = acc + shift(j - pad) * cw_ref[j:j + 1, :]
    y = _silu(acc)
    nqk = DN_HEADS * DN_DK
    for h in range(DN_HEADS):
        qh = y[:, h * DN_DK:(h + 1) * DN_DK]
        kh = y[:, nqk + h * DN_DK:nqk + (h + 1) * DN_DK]
        qn = lax.rsqrt(jnp.sum(qh * qh, axis=-1, keepdims=True) + RMS_EPS) * (DN_DK ** -0.5)
        kn = lax.rsqrt(jnp.sum(kh * kh, axis=-1, keepdims=True) + RMS_EPS)
        q_ref[0, :, h * DN_DK:(h + 1) * DN_DK] = (qh * qn).astype(q_ref.dtype)
        k_ref[0, :, h * DN_DK:(h + 1) * DN_DK] = (kh * kn).astype(k_ref.dtype)
    v_ref[0] = y[:, 2 * nqk:].astype(v_ref.dtype)
    ab = ab_ref[0]
    lane = lax.broadcasted_iota(jnp.int32, ab.shape, 1)
    g = ga_ref[...] * _softplus(ab + gbias_ref[...])
    parts = _split3(g)
    c = DN_CHUNK
    tri_f, tri_b = trif_ref[...], trib_ref[...]
    chunks = range(0, g.shape[0], c)
    cum_f = jnp.concatenate([sum(jnp.dot(tri_f, p[r:r + c], preferred_element_type=F32) for p in parts)
                             for r in chunks], axis=0)
    cum_b = jnp.concatenate([sum(jnp.dot(tri_b, p[r:r + c], preferred_element_type=F32) for p in parts)
                             for r in chunks], axis=0)
    gb_ref[0] = jnp.where(lane < DN_HEADS, cum_f, jnp.where(lane < 2 * DN_HEADS, cum_b, _sigmoid(ab)))


def _dn_prep(qkv, ab, conv_w, a_log, dt_bias):
    b, l, _ = qkv.shape
    t = SCAN_TILE
    ga = jnp.zeros((1, LANE), F32).at[0, :2 * DN_HEADS].set(-jnp.exp(a_log.astype(F32)).reshape(-1))
    gbias = jnp.zeros((1, LANE), F32).at[0, :2 * DN_HEADS].set(dt_bias.astype(F32).reshape(-1))
    prev, nxt = _halo_specs(t, DN_QKV, 0, l)
    blk = lambda w: pl.BlockSpec((1, t, w), lambda bi, ti: (bi, ti, 0))
    i = np.arange(DN_CHUNK)
    tri_f = jnp.asarray(i[None, :] <= i[:, None], dtype=BF16)
    tri_b = jnp.asarray(i[None, :] >= i[:, None], dtype=BF16)
    return pl.pallas_call(
        _dn_prep_kernel, name="dn_prep",
        grid=(b, l // t),
        in_specs=[blk(DN_QKV), prev, nxt, blk(LANE), _full((DN_CONV, DN_QKV)), _full((1, LANE)), _full((1, LANE)),
                  _full((DN_CHUNK, DN_CHUNK)), _full((DN_CHUNK, DN_CHUNK))],
        out_specs=[blk(DN_WIDTH), blk(DN_WIDTH), blk(DN_WIDTH), blk(LANE)],
        out_shape=[jax.ShapeDtypeStruct((b, l, DN_WIDTH), ACT)] * 3 + [jax.ShapeDtypeStruct((b, l, LANE), F32)],
        compiler_params=_cparams(("parallel", "parallel")),
    )(qkv, qkv, qkv, ab, conv_w.astype(F32), ga, gbias, tri_f, tri_b)


HALF = LANE // 2


def _pair_diag(x):
    low = lax.broadcasted_iota(jnp.int32, x.shape, 1) < HALF
    return jnp.concatenate([jnp.where(low, x, 0.0), jnp.where(low, 0.0, x)], axis=0).astype(BF16)


def _pair_inverses(xs, eye2, c):
    rs = [eye2 + x for x in xs]
    ps = [_dot16(x, _pair_diag(x)) for x in xs]
    for _ in range(int(math.log2(c)) - 2):
        zs = [_dot16(jnp.concatenate([r, p], axis=0), _pair_diag(p)) for r, p in zip(rs, ps)]
        rs = [r + z[:c] for r, z in zip(rs, zs)]
        ps = [z[c:] for z in zs]
    return [r + _dot16(r, _pair_diag(p)) for r, p in zip(rs, ps)]


def _dn_scan_kernel(qf_ref, kf_ref, vf_ref, gf_ref, qb_ref, kb_ref, vb_ref, gb_ref, of_ref, ob_ref, s_ref):
    c = DN_CHUNK
    n_sub = qf_ref.shape[1] // c

    @pl.when(pl.program_id(1) == 0)
    def _():
        s_ref[...] = jnp.zeros_like(s_ref)

    row = lax.broadcasted_iota(jnp.int32, (c, LANE), 0)
    col = lax.broadcasted_iota(jnp.int32, (c, LANE), 1) % c
    low = lax.broadcasted_iota(jnp.int32, (c, LANE), 1) < c
    eye2_b = row == col
    eye2 = eye2_b.astype(F32)
    masks2 = ((col <= row, col < row, row <= col), (col >= row, col > row, row >= col))
    neg = jnp.float32(-1e30)
    sides =((qf_ref, kf_ref, vf_ref, gf_ref, of_ref), (qb_ref, kb_ref, vb_ref, gb_ref, ob_ref))

    def diag2(a, b):
        a16, b16 = a.astype(BF16), b.astype(BF16)
        za = jnp.zeros((a16.shape[0], b16.shape[1]), BF16)
        zb = jnp.zeros((b16.shape[0], a16.shape[1]), BF16)
        return jnp.concatenate([jnp.concatenate([a16, za], axis=1), jnp.concatenate([zb, b16], axis=1)], axis=0)

    def chunk(i, carry):
        groups = []
        for u, d in [(u, d) for u in range(DN_UNROLL) for d in range(2)]:
            j = i * DN_UNROLL + u
            jj = (n_sub - 1 - j) if d else j
            rows = pl.ds(pl.multiple_of(jj * c, c), c)
            groups.append(dict(u=u, d=d, rows=rows, gb=sides[d][3][0, rows, :]))
        chains, pairs = [], []
        for grp in groups:
            d, gb = grp["d"], grp["gb"]
            for h0 in range(0, DN_HEADS, 2):
                cums = [gb[:, d * DN_HEADS + h:d * DN_HEADS + h + 1] for h in (h0, h0 + 1)]
                pairs.append(dict(a=len(chains), b=len(chains) + 1, strict=masks2[d][1], incl=masks2[d][0],
                                  cum=jnp.where(low, cums[0], cums[1])))
                for t, h in enumerate((h0, h0 + 1)):
                    last = cums[t][0:1] if d else cums[t][c - 1:c]
                    chains.append(dict(u=grp["u"], slot=d * DN_HEADS + h, rows=grp["rows"], o_ref=sides[d][4],
                                       lanes=slice(h * DN_DK, (h + 1) * DN_DK), refs=sides[d], cum=cums[t], g_tot=last,
                                       beta=gb[:, (2 + d) * DN_HEADS + h:(2 + d) * DN_HEADS + h + 1]))
        for p in pairs:
            p["cum_row"] = jnp.sum(jnp.where(eye2_b, p["cum"], 0.0), axis=0, keepdims=True)
        for p in pairs:
            p["decay"] = jnp.exp(jnp.where(p["incl"], p["cum"] - p["cum_row"], neg))
        for ch in chains:
            ch["eg"] = jnp.exp(ch["cum"])
            ch["e_tot"] = jnp.exp(ch["g_tot"])
            ch["e_tail"] = jnp.exp(ch["g_tot"] - ch["cum"])
        for ch in chains:
            q_ref, k_ref, v_ref = ch["refs"][:3]
            q = q_ref[0, ch["rows"], ch["lanes"]].astype(F32)
            k = k_ref[0, ch["rows"], ch["lanes"]].astype(F32)
            v = v_ref[0, ch["rows"], ch["lanes"]].astype(F32)
            k_beta = k * ch["beta"]
            ch.update(q_dec=q * ch["eg"], k=k, lhs=jnp.concatenate([k_beta, q], axis=0),
                      rhs=jnp.concatenate([v * ch["beta"], k_beta * ch["eg"]], axis=1), k_tail=k * ch["e_tail"])
        gram = [_dot16(jnp.concatenate([chains[p["a"]]["lhs"], chains[p["b"]]["lhs"]], axis=1),
                       diag2(chains[p["a"]]["k"], chains[p["b"]]["k"]), "nt") for p in pairs]
        kks = [g[:c] * p["decay"] for g, p in zip(gram, pairs)]
        qks = [g[c:] * p["decay"] for g, p in zip(gram, pairs)]
        t_inv = _pair_inverses([jnp.where(p["strict"], -kk, 0.0) for kk, p in zip(kks, pairs)], eye2, c)
        uw_p = [_dot16(t, diag2(chains[p["a"]]["rhs"], chains[p["b"]]["rhs"])) for t, p in zip(t_inv, pairs)]
        split = lambda xs: [part for x in xs for part in (x[:, :2 * DN_DV], x[:, 2 * DN_DV:])]
        uw = split(uw_p)
        quw = split([_dot16(qk, diag2(x[:, :2 * DN_DV], x[:, 2 * DN_DV:])) for qk, x in zip(qks, uw_p)])
        kuw = [_dot16(ch["k_tail"], x, "tn") for ch, x in zip(chains, uw)]
        o_a = [ch["q_dec"] - x[:, DN_DV:] for ch, x in zip(chains, quw)]
        state = [s_ref[slot] for slot in range(2 * DN_HEADS)]
        for u in range(DN_UNROLL):
            mine = [n for n, ch in enumerate(chains) if ch["u"] == u]
            cur = [state[chains[n]["slot"]] for n in mine]
            outs = [_dot16(o_a[n], s) + quw[n][:, :DN_DV] for n, s in zip(mine, cur)]
            news = [s * chains[n]["e_tot"] - _dot16(kuw[n][:, DN_DV:], s) + kuw[n][:, :DN_DV] for n, s in zip(mine, cur)]
            for n, o, s_new in zip(mine, outs, news):
                ch = chains[n]
                ch["o_ref"][0, ch["rows"], ch["lanes"]] = o.astype(ch["o_ref"].dtype)
                state[ch["slot"]] = s_new
        for slot in range(2 * DN_HEADS):
            s_ref[slot] = state[slot]
        return carry

    lax.fori_loop(0, n_sub // DN_UNROLL, chunk, 0)


def _dn_scan(q, k, v, gb):
    b, l, _ = q.shape
    t = SCAN_TILE
    n = l // t
    fwd = lambda w: pl.BlockSpec((1, t, w), lambda bi, ti: (bi, ti, 0))
    bwd = lambda w: pl.BlockSpec((1, t, w), lambda bi, ti: (bi, n - 1 - ti, 0))
    w = DN_WIDTH
    return pl.pallas_call(
        _dn_scan_kernel, name="dn_scan",
        grid=(b, n),
        in_specs=[fwd(w), fwd(w), fwd(w), fwd(LANE), bwd(w), bwd(w), bwd(w), bwd(LANE)],
        out_specs=[fwd(w), bwd(w)],
        out_shape=[jax.ShapeDtypeStruct((b, l, w), ACT)] * 2,
        scratch_shapes=[pltpu.VMEM((2 * DN_HEADS, DN_DK, DN_DV), F32)],
        compiler_params=_cparams(("parallel", "arbitrary")),
    )(q, k, v, gb, q, k, v, gb)


def _rk_prep_kernel(x_ref, xp_ref, xn_ref, mu_ref, w2_ref, w0_ref, a2_ref, a0_ref, kk_w_ref, ka_ref, seg_ref,
                    trif_ref, trib_ref, r_ref, k_ref, v_ref, kk_ref, a_ref, lw_ref, cum_ref):
    cur, shift = _tile_and_shifts(x_ref, xp_ref, xn_ref)
    s = cur + mu_ref[0:1, :] * (shift(-1) - cur) + mu_ref[1:2, :] * (shift(1) - cur)
    w = RK_WIDTH
    r = s[:, 0:w]
    k = s[:, w:2 * w]
    v = s[:, 2 * w:3 * w]
    wd = s[:, 3 * w:3 * w + 2 * RK_LORA]
    ad = s[:, 3 * w + 2 * RK_LORA:3 * w + 4 * RK_LORA]
    lora_w = _dot16(jnp.tanh(wd), w2_ref[...])
    lw = -math.exp(-0.5) * _sigmoid(w0_ref[...] + lora_w)
    lw_ref[0] = lw
    c = RK_CHUNK
    parts = _split3(lw)
    halves = ((trif_ref[...], slice(0, w)), (trib_ref[...], slice(w, 2 * w)))
    cum_ref[0] = jnp.concatenate(
        [jnp.concatenate([sum(jnp.dot(tri, p[r0:r0 + c, cols], preferred_element_type=F32) for p in parts)
                          for tri, cols in halves], axis=1) for r0 in range(0, lw.shape[0], c)], axis=0)
    a = _sigmoid(a0_ref[...] + _dot16(ad, a2_ref[...]))
    kk_raw = k * kk_w_ref[...]
    ssq = _dot_exact_rhs(kk_raw * kk_raw, seg_ref[...])
    kk_ref[0] = (kk_raw * lax.rsqrt(ssq + RMS_EPS)).astype(kk_ref.dtype)
    r_ref[0] = r.astype(r_ref.dtype)
    k_ref[0] = (k * (1.0 + (a - 1.0) * ka_ref[...])).astype(k_ref.dtype)
    v_ref[0] = v.astype(v_ref.dtype)
    a_ref[0] = a.astype(a_ref.dtype)


def _seg_ones(width, group):
    i = np.arange(width) // group
    return (i[:, None] == i[None, :]).astype(np.float32)


def _rk_prep(rk, mu, w0, w2, a0, a2, k_k, k_a):
    b, l, wp = rk.shape
    t = ROW_TILE
    w = RK_WIDTH
    mu_p = jnp.zeros((2, wp), F32).at[:, :RK_SHIFT].set(mu.astype(F32))
    w2cat = jnp.zeros((2 * RK_LORA, 2 * w), F32)
    w2cat = w2cat.at[:RK_LORA, :w].set(w2[0]).at[RK_LORA:, w:].set(w2[1]).astype(BF16)
    w0cat = w0.astype(F32).reshape(1, 2 * w)
    a2p = jnp.zeros((2 * RK_LORA, w), F32).at[:RK_LORA].set(a2).astype(BF16)
    idx = np.arange(RK_CHUNK)
    prev, nxt = _halo_specs(t, wp, 0, l)
    blk = lambda width: pl.BlockSpec((1, t, width), lambda bi, ti: (bi, ti, 0))
    return pl.pallas_call(
        _rk_prep_kernel, name="rk_prep",
        grid=(b, l // t),
        in_specs=[blk(wp), prev, nxt, _full((2, wp)), _full((2 * RK_LORA, 2 * w)), _full((1, 2 * w)),
                  _full((2 * RK_LORA, w)), _full((1, w)), _full((1, w)), _full((1, w)), _full((w, w)),
                  _full((RK_CHUNK, RK_CHUNK)), _full((RK_CHUNK, RK_CHUNK))],
        out_specs=[blk(w)] * 5 + [blk(2 * w)] * 2,
        out_shape=[jax.ShapeDtypeStruct((b, l, w), ACT)] * 5 + [jax.ShapeDtypeStruct((b, l, 2 * w), F32)] * 2,
        compiler_params=_cparams(("parallel", "parallel")),
    )(rk, rk, rk, mu_p, w2cat, w0cat, a2p, a0.astype(F32).reshape(1, w), k_k.astype(F32).reshape(1, w),
      k_a.astype(F32).reshape(1, w), jnp.asarray(_seg_ones(w, RK_HEAD), dtype=BF16),
      jnp.asarray(idx[None, :] <= idx[:, None], dtype=BF16), jnp.asarray(idx[None, :] >= idx[:, None], dtype=BF16))


def _rk_scan_kernel(rf_ref, kf_ref, vf_ref, kkf_ref, af_ref, lwf_ref, cumf_ref, rb_ref, kb_ref, vb_ref, kkb_ref,
                    ab_ref, lwb_ref, cumb_ref, yf_ref, yb_ref, s_ref):
    c = RK_CHUNK
    n_sub = rf_ref.shape[1] // c
    n_pair = RK_WIDTH // LANE

    @pl.when(pl.program_id(1) == 0)
    def _():
        s_ref[...] = jnp.zeros_like(s_ref)

    row = lax.broadcasted_iota(jnp.int32, (c, LANE), 0)
    col = lax.broadcasted_iota(jnp.int32, (c, LANE), 1) % RK_HEAD
    eye2 = (row == col).astype(F32)
    masks2 = ((col <= row, col < row), (col >= row, col > row))
    low_half = lax.broadcasted_iota(jnp.int32, (RK_HEAD, LANE), 1) < RK_HEAD
    same_block = (lax.broadcasted_iota(jnp.int32, (LANE, LANE), 0) // RK_HEAD) == (
        lax.broadcasted_iota(jnp.int32, (LANE, LANE), 1) // RK_HEAD)
    sides = ((rf_ref, kf_ref, vf_ref, kkf_ref, af_ref, lwf_ref, cumf_ref, yf_ref),
             (rb_ref, kb_ref, vb_ref, kkb_ref, ab_ref, lwb_ref, cumb_ref, yb_ref))

    def chunk(i, carry):
        chains = []
        for u, d in [(u, d) for u in range(RK_UNROLL) for d in range(2)]:
            r_ref, k_ref, v_ref, kk_ref, a_ref, lw_ref, cum_ref, y_ref = sides[d]
            j = i * RK_UNROLL + u
            jj = (n_sub - 1 - j) if d else j
            rows = pl.ds(pl.multiple_of(jj * c, c), c)
            lw = lw_ref[0, rows, :]
            cum = cum_ref[0, rows, :]
            tot = cum[0:1] if d else cum[c - 1:c]
            e_neg = jnp.exp(-cum)
            e_tail = jnp.exp(tot - cum)
            e_tot = jnp.exp(tot)
            k = k_ref[0, rows, :].astype(F32)
            v = v_ref[0, rows, :].astype(F32)
            kk = kk_ref[0, rows, :].astype(F32)
            b_vec = kk * a_ref[0, rows, :].astype(F32)
            ra = r_ref[0, rows, :].astype(F32) * jnp.exp(cum)
            aa = -kk * jnp.exp(cum - lw)
            bb = b_vec * e_neg
            kb = k * e_neg
            bt = b_vec * e_tail
            kt = k * e_tail
            for g in range(n_pair):
                lanes = slice(g * LANE, (g + 1) * LANE)
                chains.append(dict(
                    u=u, slot=d * n_pair + g, rows=rows, lanes=lanes, y_ref=y_ref, incl=masks2[d][0],
                    strict=masks2[d][1],
                    aa=aa[:, lanes], ra=ra[:, lanes], bb=bb[:, lanes], kb=kb[:, lanes], v=v[:, lanes],
                    bt=bt[:, lanes], kt=kt[:, lanes], e_tot=e_tot[:, lanes]))
        for ch in chains:
            ch["lhs"] = jnp.concatenate([ch["aa"], ch["ra"]], axis=0)
        gb = [_dot16(ch["lhs"], _pair_diag(ch["bb"]), "nt") for ch in chains]
        gk = [_dot16(ch["lhs"], _pair_diag(ch["kb"]), "nt") for ch in chains]
        a_ab = [jnp.where(ch["strict"], g[:c], 0.0) for g, ch in zip(gb, chains)]
        m_rb = [jnp.where(ch["incl"], g[c:], 0.0) for g, ch in zip(gb, chains)]
        akrk = [jnp.concatenate([jnp.where(ch["strict"], g[:c], 0.0), jnp.where(ch["incl"], g[c:], 0.0)], axis=0)
                for g, ch in zip(gk, chains)]
        avyv = [_dot16(m, _pair_diag(ch["v"])) for m, ch in zip(akrk, chains)]
        t_inv = _pair_inverses(a_ab, eye2, c)
        tq = [_dot16(t, jnp.concatenate([_pair_diag(ch["aa"]), _pair_diag(x[:c])], axis=1))
              for t, ch, x in zip(t_inv, chains, avyv)]
        yy = [_dot16(m, jnp.concatenate([_pair_diag(x[:, :LANE]), _pair_diag(x[:, LANE:])], axis=1))
              for m, x in zip(m_rb, tq)]
        ya = [ch["ra"] + y[:, :LANE] for ch, y in zip(chains, yy)]
        yb = [y[:, LANE:] + x[c:] for y, x in zip(yy, avyv)]
        wm = [jnp.where(same_block, _dot16(x[:, :LANE], ch["bt"], "tn"), 0.0) for x, ch in zip(tq, chains)]
        hc_full = [_dot16(jnp.concatenate([x[:, LANE:], ch["v"]], axis=0),
                          jnp.concatenate([ch["bt"], ch["kt"]], axis=0), "tn") for x, ch in zip(tq, chains)]
        hc = [jnp.where(low_half, x[:RK_HEAD], x[RK_HEAD:]) for x in hc_full]
        state = [s_ref[slot] for slot in range(2 * n_pair)]
        for u in range(RK_UNROLL):
            mine = [n for n, ch in enumerate(chains) if ch["u"] == u]
            cur = [state[chains[n]["slot"]] for n in mine]
            outs = [_dot16(ya[n], _pair_diag(s), "nt") + yb[n] for n, s in zip(mine, cur)]
            news = [s * chains[n]["e_tot"] + _dot16(s, wm[n]) + hc[n] for n, s in zip(mine, cur)]
            for n, y, s_new in zip(mine, outs, news):
                ch = chains[n]
                ch["y_ref"][0, ch["rows"], ch["lanes"]] = y.astype(ch["y_ref"].dtype)
                state[ch["slot"]] = s_new
        for slot in range(2 * n_pair):
            s_ref[slot] = state[slot]
        return carry

    lax.fori_loop(0, n_sub // RK_UNROLL, chunk, 0)


def _rk_scan(r, k, v, kk, a, lw, cum):
    b, l, w = r.shape
    t = RK_CHUNK * RK_UNROLL * 2
    n = l // t
    fwd = pl.BlockSpec((1, t, w), lambda bi, ti: (bi, ti, 0))
    bwd = pl.BlockSpec((1, t, w), lambda bi, ti: (bi, n - 1 - ti, 0))
    bwd_lw = pl.BlockSpec((1, t, w), lambda bi, ti: (bi, n - 1 - ti, 1))
    return pl.pallas_call(
        _rk_scan_kernel, name="rk_scan",
        grid=(b, n),
        in_specs=[fwd] * 7 + [bwd] * 5 + [bwd_lw] * 2,
        out_specs=[fwd, bwd],
        out_shape=[jax.ShapeDtypeStruct((b, l, w), ACT)] * 2,
        scratch_shapes=[pltpu.VMEM((2 * RK_WIDTH // LANE, RK_HEAD, LANE), F32)],
        compiler_params=_cparams(("parallel", "arbitrary")),
    )(r, k, v, kk, a, lw, cum, r, k, v, kk, a, lw, cum)


def _even_mix_kernel(of_ref, ob_ref, dg_ref, dnw_ref, yf_ref, yb_ref, r_ref, k_ref, v_ref, rg_ref,
                     rk_ref, lnw_ref, lnb_ref, segm_ref, seg1_ref, o_ref):
    f32 = lambda ref: ref[0].astype(F32)
    o = f32(of_ref) + f32(ob_ref)
    gate = _silu(f32(dg_ref))
    for h in range(DN_HEADS):
        lanes = slice(h * DN_DV, (h + 1) * DN_DV)
        oh = o[:, lanes]
        ms = jnp.mean(oh * oh, axis=-1, keepdims=True)
        o_ref[0, :, lanes] = (oh * lax.rsqrt(ms + RMS_EPS) * dnw_ref[...] * gate[:, lanes]).astype(o_ref.dtype)
    wkv = f32(yf_ref) + f32(yb_ref)
    mean = _dot_exact_rhs(wkv, segm_ref[...])
    cen = wkv - mean
    var = _dot_exact_rhs(cen * cen, segm_ref[...])
    wkv = cen * lax.rsqrt(var + RK_GN_EPS) * lnw_ref[...] + lnb_ref[...]
    bonus = _dot_exact_rhs(f32(r_ref) * f32(k_ref) * rk_ref[...], seg1_ref[...]) * f32(v_ref)
    o_ref[0, :, DN_WIDTH:] = ((wkv + bonus) * _silu(f32(rg_ref))).astype(o_ref.dtype)


def _even_mix(o_f, o_b, dn_gate, dn_norm, y_f, y_b, r, k, v, rk_gate, r_k, ln_w, ln_b):
    b, l, _ = o_f.shape
    t = ROW_TILE
    w = RK_WIDTH
    blk = lambda width: pl.BlockSpec((1, t, width), lambda bi, ti: (bi, ti, 0))
    seg1 = jnp.asarray(_seg_ones(w, RK_HEAD), dtype=BF16)
    segm = jnp.asarray(_seg_ones(w, RK_HEAD) / RK_HEAD, dtype=BF16)
    return pl.pallas_call(
        _even_mix_kernel, name="even_mix",
        grid=(b, l // t),
        in_specs=[blk(DN_WIDTH), blk(DN_WIDTH), blk(DN_WIDTH), _full((1, DN_DV)),
                  blk(w), blk(w), blk(w), blk(w), blk(w), blk(w),
                  _full((1, w)), _full((1, w)), _full((1, w)), _full((w, w)), _full((w, w))],
        out_specs=blk(DN_WIDTH + w),
        out_shape=jax.ShapeDtypeStruct((b, l, DN_WIDTH + w), ACT),
        compiler_params=_cparams(("parallel", "parallel")),
    )(o_f, o_b, dn_gate, dn_norm.astype(F32).reshape(1, DN_DV), y_f, y_b, r, k, v, rk_gate,
      r_k.astype(F32).reshape(1, w), ln_w.astype(F32).reshape(1, w), ln_b.astype(F32).reshape(1, w), segm, seg1)


def _even_layer(h, w_in, dn_conv, dn_a_log, dn_dt_bias, dn_norm, rk_mu, rk_w0, rk_w2, rk_a0, rk_a2,
                rk_k_k, rk_k_a, rk_r_k, rk_ln_w, rk_ln_b):
    b, l, d = h.shape
    s0 = DN_QKV
    s1 = s0 + DN_AB
    s2 = s1 + DN_WIDTH
    s3 = s2 + RK_SHIFT
    pad = lambda m, width: jnp.pad(m, ((0, 0), (0, width - m.shape[1])))
    widths = (DN_QKV, LANE, DN_WIDTH, RK_SHIFT_PAD, RK_WIDTH)
    w16 = jnp.concatenate([w_in[:, :s0], pad(w_in[:, s0:s1], LANE), w_in[:, s1:s2],
                           pad(w_in[:, s2:s3], RK_SHIFT_PAD), w_in[:, s3:]], axis=1).astype(BF16)
    qkv, ab, dn_gate, rk, rk_gate = _project(h.reshape(b * l, d), w16, widths, (ACT, F32, ACT, ACT, ACT))
    r3 = lambda m: m.reshape(b, l, m.shape[-1])
    q, k, v, gb = _dn_prep(r3(qkv), r3(ab), dn_conv, dn_a_log, dn_dt_bias)
    o_f, o_b = _dn_scan(q, k, v, gb)
    r, kr, vr, kk, a, lw, cum = _rk_prep(r3(rk), rk_mu, rk_w0, rk_w2, rk_a0, rk_a2, rk_k_k, rk_k_a)
    y_f, y_b = _rk_scan(r, kr, vr, kk, a, lw, cum)
    return _even_mix(o_f, o_b, r3(dn_gate), dn_norm, y_f, y_b, r, kr, vr, r3(rk_gate),
                     rk_r_k, rk_ln_w, rk_ln_b)


def _dft_geometry(l):
    nf = 2 * l
    p = nf // DFT_Q
    n1 = p // 2
    k1 = p // 2 + 1
    k1p = -(-k1 // SUBLANE) * SUBLANE
    return nf, p, n1, k1, k1p


@functools.lru_cache(maxsize=None)
def _dft_tables(l):
    nf, p, n1c, k1c, k1p = _dft_geometry(l)
    q = DFT_Q
    n2 = np.arange(q)[:, None, None]
    k1 = np.arange(k1c)[None, :, None]
    n1 = np.arange(n1c)[None, None, :]
    ph = -2.0 * np.pi * (((n1 * k1) % p) / p + ((n2 * k1) % nf) / nf)
    fa = np.zeros((q, 2 * k1p, n1c))
    fa[:, :k1c] = np.cos(ph)
    fa[:, k1p:k1p + k1c] = np.sin(ph)
    wgt = np.full((k1c,), 2.0)
    wgt[0] = 1.0
    wgt[-1] = 1.0
    th = -ph.transpose(0, 2, 1)
    gd = np.zeros((q, n1c, 2 * k1p))
    gd[:, :, :k1c] = np.cos(th) * wgt / nf
    gd[:, :, k1p:k1p + k1c] = -np.sin(th) * wgt / nf
    a = np.arange(q)
    ang = -2.0 * np.pi * ((a[:, None] * a[None, :]) % q) / q
    cr, ci = np.cos(ang), np.sin(ang)
    fb = np.block([[cr, -ci], [ci, cr]])
    fc = np.block([[cr, ci], [-ci, cr]])

    return tuple(m.astype(np.float32).astype(BF16) for m in (fa, fb, fc, gd))


def _fdot(f, x):
    return jnp.dot(f, x.astype(BF16), preferred_element_type=F32)


DFT_GROUP_N2 = 8
DFT_GROUP_K1 = 24


def _k1_group(k1p, most=DFT_GROUP_K1):
    return max(g for g in range(2, most + 1, 2) if k1p % g == 0)
DFT_PITCH = DFT_Q + SUBLANE


def _stage_a(src, y_re, y_im, fa, geo):
    nf, p, n1c, k1c, k1p = geo
    g = DFT_GROUP_N2

    def body(i, carry):
        n2s = [i * g + t for t in range(g)]
        slabs = [src[pl.ds(n2, n1c, stride=DFT_PITCH), :] for n2 in n2s]
        outs = [_fdot(fa[n2], slab) for n2, slab in zip(n2s, slabs)]
        for n2, out in zip(n2s, outs):
            y_re[pl.ds(n2, k1p, stride=DFT_PITCH), :] = out[:k1p]
            y_im[pl.ds(n2, k1p, stride=DFT_PITCH), :] = out[k1p:]
        return carry

    lax.fori_loop(0, DFT_Q // g, body, 0)


def _fdot_pairs(f, xs):
    outs = []
    for a, b in zip(xs[0::2], xs[1::2]):
        z = _fdot(f, jnp.concatenate([a, b], axis=1))
        outs += [z[:, :a.shape[1]], z[:, a.shape[1]:]]
    return outs


def _stage_b(y_re, y_im, i, fb, g):
    k1s = [i * g + t for t in range(g)]
    rows = [pl.ds(pl.multiple_of(k1 * DFT_PITCH, SUBLANE), DFT_Q) for k1 in k1s]
    ws = [jnp.concatenate([y_re[r, :], y_im[r, :]], axis=0) for r in rows]
    return k1s, rows, _fdot_pairs(fb[...], ws)


def _hy_conv_kernel(u_ref, m_ref, skip_ref, hr_ref, hi_ref, fa, fb, fc, gd, o_ref, pad, y_re, y_im, *, geo):
    nf, p, n1c, k1c, k1p = geo
    group = _k1_group(k1p)
    for n1 in range(n1c):
        pad[n1 * DFT_PITCH:n1 * DFT_PITCH + DFT_Q, :] = u_ref[0, n1 * DFT_Q:(n1 + 1) * DFT_Q, :].astype(F32)
    _stage_a(pad, y_re, y_im, fa, geo)

    def mid(i, carry):
        k1s, rows, zs = _stage_b(y_re, y_im, i, fb, group)
        prods = []
        for k1, z in zip(k1s, zs):
            zr, zi = z[:DFT_Q], z[DFT_Q:]
            hrows = pl.ds(pl.multiple_of(k1 * DFT_Q, DFT_Q), DFT_Q)
            hr = hr_ref[hrows, :]
            hi = hi_ref[hrows, :]
            prods.append(jnp.concatenate([zr * hr - zi * hi, zr * hi + zi * hr], axis=0))
        outs = _fdot_pairs(fc[...], prods)
        for r, a in zip(rows, outs):
            y_re[r, :] = a[:DFT_Q]
            y_im[r, :] = a[DFT_Q:]
        return carry

    lax.fori_loop(0, k1p // group, mid, 0)

    def last(i, carry):
        n2s = [i * DFT_GROUP_N2 + t for t in range(DFT_GROUP_N2)]
        ins = [jnp.concatenate([y_re[pl.ds(n2, k1p, stride=DFT_PITCH), :], y_im[pl.ds(n2, k1p, stride=DFT_PITCH), :]],
                               axis=0) for n2 in n2s]
        outs = [_fdot(gd[n2], a) for n2, a in zip(n2s, ins)]
        for n2, out in zip(n2s, outs):
            pad[pl.ds(n2, n1c, stride=DFT_PITCH), :] = out
        return carry

    lax.fori_loop(0, DFT_Q // DFT_GROUP_N2, last, 0)

    skip = skip_ref[...]
    for n1 in range(n1c):
        rows = slice(n1 * DFT_Q, (n1 + 1) * DFT_Q)
        conv = pad[n1 * DFT_PITCH:n1 * DFT_PITCH + DFT_Q, :]
        o_ref[0, rows, :] = (m_ref[0, rows, :].astype(F32)
                             * (conv + skip * u_ref[0, rows, :].astype(F32))).astype(o_ref.dtype)


def _single(shape, index_map):
    return pl.BlockSpec(shape, index_map, pipeline_mode=pl.Buffered(1))


def _hy_conv(u, mult, skip, h_re, h_im, order, tables):
    b, l, ch = u.shape
    geo = _dft_geometry(l)
    nf, p, n1c, k1c, k1p = geo
    ct = HY_CT
    consts = [jnp.asarray(t) for t in tables]
    seq = pl.BlockSpec((1, l, ct), lambda ci, bi: (bi, 0, ci))
    spec = _single((None, k1p * DFT_Q, ct), lambda ci, bi: (order, 0, ci))
    cspecs = [_single(c.shape, (lambda ci, bi, nd=c.ndim: (0,) * nd)) for c in consts]
    ysc = pltpu.VMEM((k1p * DFT_PITCH, ct), F32)
    return pl.pallas_call(
        functools.partial(_hy_conv_kernel, geo=geo), name="hy_conv",
        grid=(ch // ct, b),
        in_specs=[seq, seq, pl.BlockSpec((1, ct), lambda ci, bi: (0, ci)), spec, spec] + cspecs,
        out_specs=seq,
        out_shape=jax.ShapeDtypeStruct((b, l, ch), ACT),
        scratch_shapes=[pltpu.VMEM((n1c * DFT_PITCH, ct), F32), ysc, ysc],
        compiler_params=_cparams(("parallel", "parallel")),
    )(u, mult, skip, h_re, h_im, *consts)


def _hy_mlp_kernel(f_ref, w1_ref, b1_ref, w2_ref, b2_ref, w3_ref, b3_ref, fr_ref, o_ref):
    fr = fr_ref[...]
    hdn = jnp.sin(fr * (_dot32(f_ref[...], w1_ref[...]) + b1_ref[...]))
    hdn = jnp.sin(fr * (_dot32(hdn, w2_ref[...]) + b2_ref[...]))
    o_ref[...] = jnp.sin(fr * (_dot32(hdn, w3_ref[...]) + b3_ref[...]))


def _hy_mlp(feats, w1, b1, w2, b2, w3, b3, freq):
    l = feats.shape[0]
    t = min(l, 1024)
    fw = HY_FW
    row = lambda a: a.astype(F32).reshape(1, fw)
    w1p = jnp.zeros((LANE, fw), F32).at[:HY_EMB].set(w1.astype(F32))
    return pl.pallas_call(
        _hy_mlp_kernel, name="hy_mlp",
        grid=(l // t,),
        in_specs=[pl.BlockSpec((t, LANE), lambda i: (i, 0)), _full((LANE, fw)), _full((1, fw)), _full((fw, fw)),
                  _full((1, fw)), _full((fw, fw)), _full((1, fw)), _full((1, fw))],
        out_specs=pl.BlockSpec((t, fw), lambda i: (i, 0)),
        out_shape=jax.ShapeDtypeStruct((l, fw), F32),
        compiler_params=_cparams(("parallel",)),
    )(feats, w1p, row(b1), w2.astype(F32), row(b2), w3.astype(F32), row(b3), row(freq))


def _hy_filter_kernel(hdn_ref, wf_ref, wb_ref, df_ref, db_ref, fa, fb,
                      hr_ref, hi_ref, filt, yr_f, yi_f, yr_b, yi_b, *, geo):
    nf, p, n1c, k1c, k1p = geo
    group = _k1_group(k1p, SUBLANE)

    def build(w_ref, d_ref, drop_first):
        grp = 4

        def body(i, acc):
            n1s = [i * grp + t for t in range(grp)]
            hxs = [hdn_ref[pl.ds(pl.multiple_of(n1 * DFT_Q, DFT_Q), DFT_Q), :] for n1 in n1s]
            raw = [_dot32(hx, w_ref[0]) for hx in hxs]
            for n1, hx, hv in zip(n1s, hxs, raw):
                hv = hv * jnp.exp(-hx[:, HY_FW:HY_FW + 1] * jnp.abs(d_ref[0]))
                if drop_first:
                    pos = lax.broadcasted_iota(jnp.int32, hv.shape, 0) + n1 * DFT_Q
                    hv = jnp.where(pos == 0, 0.0, hv)
                filt[pl.ds(pl.multiple_of(n1 * DFT_PITCH, SUBLANE), DFT_Q), :] = hv
                acc = acc + jnp.sum(jnp.abs(hv), axis=0, keepdims=True)
            return acc

        return lax.fori_loop(0, n1c // grp, body, jnp.zeros((1, filt.shape[1]), F32))

    l1 = build(wf_ref, df_ref, False)
    _stage_a(filt, yr_f, yi_f, fa, geo)
    l1 = l1 + build(wb_ref, db_ref, True)
    _stage_a(filt, yr_b, yi_b, fa, geo)
    inv = 1.0 / (l1 + RMS_EPS)

    def mid(i, carry):
        k1s, _, zf = _stage_b(yr_f, yi_f, i, fb, group)
        _, _, zb = _stage_b(yr_b, yi_b, i, fb, group)
        for k1, f, b in zip(k1s, zf, zb):
            r = pl.ds(pl.multiple_of(k1 * DFT_Q, DFT_Q), DFT_Q)
            hr_ref[0, r, :] = (f[:DFT_Q] + b[:DFT_Q]) * inv
            hi_ref[0, r, :] = (f[DFT_Q:] - b[DFT_Q:]) * inv
        return carry

    lax.fori_loop(0, k1p // group, mid, 0)


def _hy_filters(hdn, tcol, w_out, deltas, tables, l):
    geo = _dft_geometry(l)
    nf, p, n1c, k1c, k1p = geo
    ch = w_out.shape[1] // (2 * HY_ORDER)
    ct = HY_CT
    nct = ch // ct
    hdn_x = jnp.concatenate([hdn, tcol, jnp.zeros((l, LANE - HY_FW - 1), F32)], axis=1)
    w4 = w_out.astype(F32).reshape(HY_FW, 2 * HY_ORDER, ch).transpose(1, 0, 2)
    w4 = jnp.pad(w4, ((0, 0), (0, LANE - HY_FW), (0, 0)))
    d4 = deltas.astype(F32).reshape(2 * HY_ORDER, 1, ch)
    consts = [jnp.asarray(t) for t in tables[:2]]
    cspecs = [_single(c.shape, (lambda o, ci, nd=c.ndim: (0,) * nd)) for c in consts]
    wspec = lambda d: pl.BlockSpec((1, LANE, ct), lambda o, ci: (2 * o + d, 0, ci))
    dspec = lambda d: pl.BlockSpec((1, 1, ct), lambda o, ci: (2 * o + d, 0, ci))
    ospec = pl.BlockSpec((1, k1p * DFT_Q, ct), lambda o, ci: (o, 0, ci))
    ysc = pltpu.VMEM((k1p * DFT_PITCH, ct), F32)
    return pl.pallas_call(
        functools.partial(_hy_filter_kernel, geo=geo), name="hy_filters",
        grid=(HY_ORDER, nct),
        in_specs=[_single((l, LANE), lambda o, ci: (0, 0)), wspec(0), wspec(1), dspec(0), dspec(1)] + cspecs,
        out_specs=[ospec, ospec],
        out_shape=[jax.ShapeDtypeStruct((HY_ORDER, k1p * DFT_Q, ch), F32)] * 2,
        scratch_shapes=[pltpu.VMEM((n1c * DFT_PITCH, ct), F32), ysc, ysc, ysc, ysc],
        compiler_params=_cparams(("parallel", "parallel")),
    )(hdn_x, w4, w4, d4, d4, *consts)


def _position_features(l):
    bands = (HY_EMB - 1) // 2
    t = jnp.linspace(0.0, 1.0, l, dtype=F32)[:, None]
    f = jnp.linspace(1e-4, bands - 1, bands, dtype=F32)[None, :]
    ang = (2.0 * math.pi / l) * jnp.arange(l, dtype=F32)[:, None] * f
    feats = jnp.concatenate([t, jnp.cos(ang), -jnp.sin(ang)], axis=-1)
    return jnp.pad(feats, ((0, 0), (0, LANE - HY_EMB))), t


def _hy_prep_kernel(x_ref, xp_ref, xn_ref, g_ref, cw_ref, cb_ref, x1_ref, m2_ref, v_ref):
    cur, shift = _tile_and_shifts(x_ref, xp_ref, xn_ref)
    y = shift(-1) * cw_ref[0:1, :] + cur * cw_ref[1:2, :] + shift(1) * cw_ref[2:3, :] + cb_ref[...]
    c = g_ref.shape[2]
    x1_ref[0] = y[:, :c].astype(x1_ref.dtype)
    m2_ref[0] = (y[:, c:2 * c] * _silu(g_ref[0].astype(F32))).astype(m2_ref.dtype)
    v_ref[0] = y[:, 2 * c:].astype(v_ref.dtype)


def _hy_prep(xv, gate, conv_w, conv_b):
    b, l, w3 = xv.shape
    c = gate.shape[2]
    t = ROW_TILE
    prev, nxt = _halo_specs(t, w3, 0, l)
    blk = lambda width: pl.BlockSpec((1, t, width), lambda bi, ti: (bi, ti, 0))
    return pl.pallas_call(
        _hy_prep_kernel, name="hy_prep",
        grid=(b, l // t),
        in_specs=[blk(w3), prev, nxt, blk(c), _full((HY_SHORT, w3)), _full((1, w3))],
        out_specs=[blk(c)] * 3,
        out_shape=[jax.ShapeDtypeStruct((b, l, c), ACT)] * 3,
        compiler_params=_cparams(("parallel", "parallel")),
    )(xv, xv, xv, gate, conv_w.astype(F32), conv_b.astype(F32).reshape(1, w3))


def _odd_layer(h, w_in, conv_w, conv_b, f_w1, f_b1, f_w2, f_b2, f_w3, f_b3, f_freq, f_out, deltas, skip):
    b, l, d = h.shape
    c = skip.shape[1]
    xv, gate = _project(h.reshape(b * l, d), w_in.astype(BF16), (3 * c, c), (ACT, ACT))
    x1, m2, v = _hy_prep(xv.reshape(b, l, 3 * c), gate.reshape(b, l, c), conv_w, conv_b)
    tables = _dft_tables(l)
    feats, tcol = _position_features(l)
    hdn = _hy_mlp(feats, f_w1, f_b1, f_w2, f_b2, f_w3, f_b3, f_freq)
    h_re, h_im = _hy_filters(hdn, tcol, f_out, deltas, tables, l)
    skip = skip.astype(F32)
    z = _hy_conv(v, x1, skip[0:1], h_re, h_im, 0, tables)
    return _hy_conv(z, m2, skip[1:2], h_re, h_im, 1, tables)


def kernel(x, p, even_w_in, dn_conv, dn_a_log, dn_dt_bias, dn_norm, rk_mu, rk_w0, rk_w2, rk_a0, rk_a2, rk_k_k, rk_k_a, rk_r_k, rk_ln_w, rk_ln_b, odd_w_in, hy_conv_w, hy_conv_b, hy_ffn_w1, hy_ffn_b1, hy_ffn_w2, hy_ffn_b2, hy_ffn_w3, hy_ffn_b3, hy_ffn_freq, hy_ffn_out, hy_deltas, hy_skip, w_out, ln_g, ln_b, ple_w, ple_norm, ple_gate):
    b, l, d = x.shape
    depth = p.shape[0]
    alpha = (2.0 * depth) ** 0.25
    h = x
    for i in range(depth):
        j = i // 2
        if i % 2 == 0:
            mix = _even_layer(h, even_w_in[j], dn_conv[j], dn_a_log[j], dn_dt_bias[j], dn_norm[j], rk_mu[j],
                              rk_w0[j], rk_w2[j], rk_a0[j], rk_a2[j], rk_k_k[j].reshape(-1), rk_k_a[j].reshape(-1),
                              rk_r_k[j].reshape(-1), rk_ln_w[j], rk_ln_b[j])
        else:
            mix = _odd_layer(h, odd_w_in[j], hy_conv_w[j], hy_conv_b[j], hy_ffn_w1[j], hy_ffn_b1[j],
                             hy_ffn_w2[j], hy_ffn_b2[j], hy_ffn_w3[j], hy_ffn_b3[j], hy_ffn_freq[j],
                             hy_ffn_out[j], hy_deltas[j], hy_skip[j])
        h2 = _post_layer(h.reshape(b * l, d), mix.reshape(b * l, mix.shape[-1]), p.reshape(depth, b * l, p.shape[-1]),
                         i, w_out[i], ple_w[i], ple_gate[i], ln_g[i], ln_b[i], ple_norm[i], alpha)
        h = h2.reshape(b, l, d)
    return h
```

```python
import functools
import math

import numpy as np
import jax
import jax.numpy as jnp
from jax import lax
from jax.experimental import pallas as pl
from jax.experimental.pallas import tpu as pltpu

F32 = jnp.float32
BF16 = jnp.bfloat16

LN_EPS = 1e-5
RMS_EPS = 1e-6

DN_HEADS = 4
DN_DK = 128
DN_DV = 128
DN_WIDTH = DN_HEADS * DN_DV
DN_QKV = 2 * DN_HEADS * DN_DK + DN_WIDTH
DN_AB = 4 * DN_HEADS
DN_CONV = 5
DN_CHUNK = 64
DN_UNROLL = 4

RK_HEADS = 8
RK_HEAD = 64
RK_WIDTH = RK_HEADS * RK_HEAD
RK_LORA = 64
RK_SHIFT = 3 * RK_WIDTH + 3 * RK_LORA
RK_SHIFT_PAD = 1792
RK_GN_EPS = 64e-5
RK_CHUNK = 64
RK_UNROLL = 2

HY_ORDER = 2
HY_SHORT = 3
HY_EMB = 33
HY_FW = 64

LANE = 128
SUBLANE = 8
DFT_Q = 128
VMEM_LIMIT = 56 * 1024 * 1024

ROW_TILE = 1024
SCAN_TILE = 1024
RK_SCAN_TILE = 512
HY_CT = 128

HI = lax.Precision.HIGHEST
ACT = BF16


def _cparams(sem):
    return pltpu.CompilerParams(dimension_semantics=sem, vmem_limit_bytes=VMEM_LIMIT)


_DIMS = {
    "nn": (((1,), (0,)), ((), ())),
    "nt": (((1,), (1,)), ((), ())),
    "tn": (((0,), (0,)), ((), ())),
}


def _dot16(a, b, dims="nn"):
    return lax.dot_general(a.astype(BF16), b.astype(BF16), _DIMS[dims], preferred_element_type=F32)


def _dot32(a, b, dims="nn"):
    return lax.dot_general(a.astype(F32), b.astype(F32), _DIMS[dims], precision=HI,
                           preferred_element_type=F32)


def _split2(x):
    hi = x.astype(BF16)
    lo = (x - hi.astype(F32)).astype(BF16)
    return hi, lo


def _dot_exact_rhs(x, m16):
    hi, lo = _split2(x)
    return (jnp.dot(hi, m16, preferred_element_type=F32) + jnp.dot(lo, m16, preferred_element_type=F32))


def _sigmoid(x):
    return 1.0 / (1.0 + jnp.exp(-x))


def _silu(x):
    return x * _sigmoid(x)


def _softplus(x):
    return jnp.maximum(x, 0.0) + jnp.log1p(jnp.exp(-jnp.abs(x)))


HALO = 16


SHIFT_ROWS = 128


def _tile_and_shifts(x_ref, xp_ref, xn_ref):
    assert x_ref.dtype == BF16
    t_idx = pl.program_id(1)
    t = x_ref.shape[1]
    cur16 = x_ref[0]
    zero = jnp.zeros_like(xp_ref[0])
    ext = jnp.concatenate([jnp.where(t_idx > 0, xp_ref[0], zero), cur16,
                           jnp.where(t_idx < pl.num_programs(1) - 1, xn_ref[0], zero)], axis=0)
    k = SHIFT_ROWS + 2 * HALO
    row = lax.broadcasted_iota(jnp.int32, (SHIFT_ROWS, k), 0)
    col = lax.broadcasted_iota(jnp.int32, (SHIFT_ROWS, k), 1)

    def shift(d):
        sel = jnp.where(col == row + (HALO + d), 1.0, 0.0).astype(BF16)
        return jnp.concatenate([jnp.dot(sel, ext[r0:r0 + k], preferred_element_type=F32)
                                for r0 in range(0, t, SHIFT_ROWS)], axis=0)

    return cur16.astype(F32), shift


def _halo_specs(t_rows, width, col, l_total):
    nb = t_rows // HALO
    last = l_total // HALO - 1
    prev = pl.BlockSpec((1, HALO, width), lambda b, t: (b, jnp.maximum(t * nb - 1, 0), col))
    nxt = pl.BlockSpec((1, HALO, width), lambda b, t: (b, jnp.minimum((t + 1) * nb, last), col))
    return prev, nxt


def _full(shape):
    nd = len(shape)
    return pl.BlockSpec(shape, lambda *_: (0,) * nd)


def _proj_kernel(a_ref, w_ref, *o_refs, offs):
    a = a_ref[...].astype(BF16)
    for o_ref, (lo, hi) in zip(o_refs, offs):
        o_ref[...] = jnp.dot(a, w_ref[:, lo:hi], preferred_element_type=F32).astype(o_ref.dtype)


def _project(a, w16, widths, dtypes):
    m, k = a.shape
    offs, o = [], 0
    for w in widths:
        offs.append((o, o + w))
        o += w
    n = o
    tm = ROW_TILE
    return pl.pallas_call(
        functools.partial(_proj_kernel, offs=tuple(offs)), name="project",
        grid=(m // tm,),
        in_specs=[pl.BlockSpec((tm, k), lambda i: (i, 0)), _full((k, n))],
        out_specs=[pl.BlockSpec((tm, w), lambda i: (i, 0)) for w in widths],
        out_shape=[jax.ShapeDtypeStruct((m, w), dt) for w, dt in zip(widths, dtypes)],
        compiler_params=_cparams(("parallel",)),
    )(a, w16)


def _post_kernel(h_ref, mix_ref, p_ref, wo_ref, pw_ref, pg_ref, lng_ref, lnb_ref, pn_ref, o_ref, *, alpha):
    t = alpha * h_ref[...] + jnp.dot(mix_ref[...], wo_ref[...], preferred_element_type=F32)
    mu = jnp.mean(t, axis=-1, keepdims=True)
    tc = t - mu
    var = jnp.mean(tc * tc, axis=-1, keepdims=True)
    y = tc * lax.rsqrt(var + LN_EPS) * lng_ref[...] + lnb_ref[...]
    e = jnp.dot(p_ref[...].astype(BF16), pw_ref[...], preferred_element_type=F32)
    e = e * lax.rsqrt(jnp.mean(e * e, axis=-1, keepdims=True) + RMS_EPS) * pn_ref[...]
    gate = _sigmoid(jnp.dot(y.astype(BF16), pg_ref[...], preferred_element_type=F32))
    o_ref[...] = y + gate * e


def _post_layer(h, mix, p_all, layer, w_out, ple_w, ple_gate, ln_g, ln_b, ple_norm, alpha):
    m, d = h.shape
    pd = p_all.shape[2]
    tm = ROW_TILE
    row = lambda w: pl.BlockSpec((tm, w), lambda i: (i, 0))
    p_spec = pl.BlockSpec((None, tm, pd), lambda i: (layer, i, 0))
    return pl.pallas_call(
        functools.partial(_post_kernel, alpha=alpha), name="post_layer",
        grid=(m // tm,),
        in_specs=[row(d), row(mix.shape[1]), p_spec, _full(w_out.shape), _full(ple_w.shape),
                  _full(ple_gate.shape), _full((1, d)), _full((1, d)), _full((1, d))],
        out_specs=row(d),
        out_shape=jax.ShapeDtypeStruct((m, d), F32),
        compiler_params=_cparams(("parallel",)),
    )(h, mix, p_all, w_out.astype(BF16), ple_w.astype(BF16), ple_gate.astype(BF16),
      ln_g.reshape(1, d), ln_b.reshape(1, d), ple_norm.reshape(1, d))


def _split3(x):
    t1 = x.astype(BF16)
    rem = x - t1.astype(F32)
    t2 = rem.astype(BF16)
    return t1, t2, (rem - t2.astype(F32)).astype(BF16)


def _dn_prep_kernel(x_ref, xp_ref, xn_ref, ab_ref, cw_ref, ga_ref, gbias_ref, trif_ref, trib_ref,
                    q_ref, k_ref, v_ref, gb_ref):
    cur, shift = _tile_and_shifts(x_ref, xp_ref, xn_ref)
    pad = DN_CONV // 2
    acc = cur * cw_ref[pad:pad + 1, :]
    for j in range(DN_CONV):
        if j != pad:
            acc = acc + shift(j - pad) * cw_ref[j:j + 1, :]
    y = _silu(acc)
    nqk = DN_HEADS * DN_DK
    for h in range(DN_HEADS):
        qh = y[:, h * DN_DK:(h + 1) * DN_DK]
        kh = y[:, nqk + h * DN_DK:nqk + (h + 1) * DN_DK]
        qn = lax.rsqrt(jnp.sum(qh * qh, axis=-1, keepdims=True) + RMS_EPS) * (DN_DK ** -0.5)
        kn = lax.rsqrt(jnp.sum(kh * kh, axis=-1, keepdims=True) + RMS_EPS)
        q_ref[0, :, h * DN_DK:(h + 1) * DN_DK] = (qh * qn).astype(q_ref.dtype)
        k_ref[0, :, h * DN_DK:(h + 1) * DN_DK] = (kh * kn).astype(k_ref.dtype)
    v_ref[0] = y[:, 2 * nqk:].astype(v_ref.dtype)
    ab = ab_ref[0]
    lane = lax.broadcasted_iota(jnp.int32, ab.shape, 1)
    g = ga_ref[...] * _softplus(ab + gbias_ref[...])
    parts = _split3(g)
    c = DN_CHUNK
    tri_f, tri_b = trif_ref[...], trib_ref[...]
    chunks = range(0, g.shape[0], c)
    cum_f = jnp.concatenate([sum(jnp.dot(tri_f, p[r:r + c], preferred_element_type=F32) for p in parts)
                             for r in chunks], axis=0)
    cum_b = jnp.concatenate([sum(jnp.dot(tri_b, p[r:r + c], preferred_element_type=F32) for p in parts)
                             for r in chunks], axis=0)
    gb_ref[0] = jnp.where(lane < DN_HEADS, cum_f, jnp.where(lane < 2 * DN_HEADS, cum_b, _sigmoid(ab)))


def _dn_prep(qkv, ab, conv_w, a_log, dt_bias):
    b, l, _ = qkv.shape
    t = SCAN_TILE
    ga = jnp.zeros((1, LANE), F32).at[0, :2 * DN_HEADS].set(-jnp.exp(a_log.astype(F32)).reshape(-1))
    gbias = jnp.zeros((1, LANE), F32).at[0, :2 * DN_HEADS].set(dt_bias.astype(F32).reshape(-1))
    prev, nxt = _halo_specs(t, DN_QKV, 0, l)
    blk = lambda w: pl.BlockSpec((1, t, w), lambda bi, ti: (bi, ti, 0))
    i = np.arange(DN_CHUNK)
    tri_f = jnp.asarray(i[None, :] <= i[:, None], dtype=BF16)
    tri_b = jnp.asarray(i[None, :] >= i[:, None], dtype=BF16)
    return pl.pallas_call(
        _dn_prep_kernel, name="dn_prep",
        grid=(b, l // t),
        in_specs=[blk(DN_QKV), prev, nxt, blk(LANE), _full((DN_CONV, DN_QKV)), _full((1, LANE)), _full((1, LANE)),
                  _full((DN_CHUNK, DN_CHUNK)), _full((DN_CHUNK, DN_CHUNK))],
        out_specs=[blk(DN_WIDTH), blk(DN_WIDTH), blk(DN_WIDTH), blk(LANE)],
        out_shape=[jax.ShapeDtypeStruct((b, l, DN_WIDTH), ACT)] * 3 + [jax.ShapeDtypeStruct((b, l, LANE), F32)],
        compiler_params=_cparams(("parallel", "parallel")),
    )(qkv, qkv, qkv, ab, conv_w.astype(F32), ga, gbias, tri_f, tri_b)


HALF = LANE // 2


def _pair_diag(x):
    low = lax.broadcasted_iota(jnp.int32, x.shape, 1) < HALF
    return jnp.concatenate([jnp.where(low, x, 0.0), jnp.where(low, 0.0, x)], axis=0).astype(BF16)


def _pair_inverses(xs, eye2, c):
    rs = [eye2 + x for x in xs]
    ps = [_dot16(x, _pair_diag(x)) for x in xs]
    for _ in range(int(math.log2(c)) - 2):
        zs = [_dot16(jnp.concatenate([r, p], axis=0), _pair_diag(p)) for r, p in zip(rs, ps)]
        rs = [r + z[:c] for r, z in zip(rs, zs)]
        ps = [z[c:] for z in zs]
    return [r + _dot16(r, _pair_diag(p)) for r, p in zip(rs, ps)]


def _dn_scan_kernel(qf_ref, kf_ref, vf_ref, gf_ref, qb_ref, kb_ref, vb_ref, gb_ref, of_ref, ob_ref, s_ref):
    c = DN_CHUNK
    n_sub = qf_ref.shape[1] // c

    @pl.when(pl.program_id(1) == 0)
    def _():
        s_ref[...] = jnp.zeros_like(s_ref)

    row = lax.broadcasted_iota(jnp.int32, (c, LANE), 0)
    col = lax.broadcasted_iota(jnp.int32, (c, LANE), 1) % c
    low = lax.broadcasted_iota(jnp.int32, (c, LANE), 1) < c
    eye2_b = row == col
    eye2 = eye2_b.astype(F32)
    masks2 = ((col <= row, col < row, row <= col), (col >= row, col > row, row >= col))
    neg = jnp.float32(-1e30)
    sides =((qf_ref, kf_ref, vf_ref, gf_ref, of_ref), (qb_ref, kb_ref, vb_ref, gb_ref, ob_ref))

    def diag2(a, b):
        a16, b16 = a.astype(BF16), b.astype(BF16)
        za = jnp.zeros((a16.shape[0], b16.shape[1]), BF16)
        zb = jnp.zeros((b16.shape[0], a16.shape[1]), BF16)
        return jnp.concatenate([jnp.concatenate([a16, za], axis=1), jnp.concatenate([zb, b16], axis=1)], axis=0)

    def chunk(i, carry):
        groups = []
        for u, d in [(u, d) for u in range(DN_UNROLL) for d in range(2)]:
            j = i * DN_UNROLL + u
            jj = (n_sub - 1 - j) if d else j
            rows = pl.ds(pl.multiple_of(jj * c, c), c)
            groups.append(dict(u=u, d=d, rows=rows, gb=sides[d][3][0, rows, :]))
        chains, pairs = [], []
        for grp in groups:
            d, gb = grp["d"], grp["gb"]
            for h0 in range(0, DN_HEADS, 2):
                cums = [gb[:, d * DN_HEADS + h:d * DN_HEADS + h + 1] for h in (h0, h0 + 1)]
                pairs.append(dict(a=len(chains), b=len(chains) + 1, strict=masks2[d][1], incl=masks2[d][0],
                                  cum=jnp.where(low, cums[0], cums[1])))
                for t, h in enumerate((h0, h0 + 1)):
                    last = cums[t][0:1] if d else cums[t][c - 1:c]
                    chains.append(dict(u=grp["u"], slot=d * DN_HEADS + h, rows=grp["rows"], o_ref=sides[d][4],
                                       lanes=slice(h * DN_DK, (h + 1) * DN_DK), refs=sides[d], cum=cums[t], g_tot=last,
                                       beta=gb[:, (2 + d) * DN_HEADS + h:(2 + d) * DN_HEADS + h + 1]))
        for p in pairs:
            p["cum_row"] = jnp.sum(jnp.where(eye2_b, p["cum"], 0.0), axis=0, keepdims=True)
        for p in pairs:
            p["decay"] = jnp.exp(jnp.where(p["incl"], p["cum"] - p["cum_row"], neg))
        for ch in chains:
            ch["eg"] = jnp.exp(ch["cum"])
            ch["e_tot"] = jnp.exp(ch["g_tot"])
            ch["e_tail"] = jnp.exp(ch["g_tot"] - ch["cum"])
        for ch in chains:
            q_ref, k_ref, v_ref = ch["refs"][:3]
            q = q_ref[0, ch["rows"], ch["lanes"]].astype(F32)
            k = k_ref[0, ch["rows"], ch["lanes"]].astype(F32)
            v = v_ref[0, ch["rows"], ch["lanes"]].astype(F32)
            k_beta = k * ch["beta"]
            ch.update(q_dec=q * ch["eg"], k=k, lhs=jnp.concatenate([k_beta, q], axis=0),
                      rhs=jnp.concatenate([v * ch["beta"], k_beta * ch["eg"]], axis=1), k_tail=k * ch["e_tail"])
        gram = [_dot16(jnp.concatenate([chains[p["a"]]["lhs"], chains[p["b"]]["lhs"]], axis=1),
                       diag2(chains[p["a"]]["k"], chains[p["b"]]["k"]), "nt") for p in pairs]
        kks = [g[:c] * p["decay"] for g, p in zip(gram, pairs)]
        qks = [g[c:] * p["decay"] for g, p in zip(gram, pairs)]
        t_inv = _pair_inverses([jnp.where(p["strict"], -kk, 0.0) for kk, p in zip(kks, pairs)], eye2, c)
        uw_p = [_dot16(t, diag2(chains[p["a"]]["rhs"], chains[p["b"]]["rhs"])) for t, p in zip(t_inv, pairs)]
        split = lambda xs: [part for x in xs for part in (x[:, :2 * DN_DV], x[:, 2 * DN_DV:])]
        uw = split(uw_p)
        quw = split([_dot16(qk, diag2(x[:, :2 * DN_DV], x[:, 2 * DN_DV:])) for qk, x in zip(qks, uw_p)])
        kuw = [_dot16(ch["k_tail"], x, "tn") for ch, x in zip(chains, uw)]
        o_a = [ch["q_dec"] - x[:, DN_DV:] for ch, x in zip(chains, quw)]
        state = [s_ref[slot] for slot in range(2 * DN_HEADS)]
        for u in range(DN_UNROLL):
            mine = [n for n, ch in enumerate(chains) if ch["u"] == u]
            cur = [state[chains[n]["slot"]] for n in mine]
            outs = [_dot16(o_a[n], s) + quw[n][:, :DN_DV] for n, s in zip(mine, cur)]
            news = [s * chains[n]["e_tot"] - _dot16(kuw[n][:, DN_DV:], s) + kuw[n][:, :DN_DV] for n, s in zip(mine, cur)]
            for n, o, s_new in zip(mine, outs, news):
                ch = chains[n]
                ch["o_ref"][0, ch["rows"], ch["lanes"]] = o.astype(ch["o_ref"].dtype)
                state[ch["slot"]] = s_new
        for slot in range(2 * DN_HEADS):
            s_ref[slot] = state[slot]
        return carry

    lax.fori_loop(0, n_sub // DN_UNROLL, chunk, 0)


def _dn_scan(q, k, v, gb):
    b, l, _ = q.shape
    t = SCAN_TILE
    n = l // t
    fwd = lambda w: pl.BlockSpec((1, t, w), lambda bi, ti: (bi, ti, 0))
    bwd = lambda w: pl.BlockSpec((1, t, w), lambda bi, ti: (bi, n - 1 - ti, 0))
    w = DN_WIDTH
    return pl.pallas_call(
        _dn_scan_kernel, name="dn_scan",
        grid=(b, n),
        in_specs=[fwd(w), fwd(w), fwd(w), fwd(LANE), bwd(w), bwd(w), bwd(w), bwd(LANE)],
        out_specs=[fwd(w), bwd(w)],
        out_shape=[jax.ShapeDtypeStruct((b, l, w), ACT)] * 2,
        scratch_shapes=[pltpu.VMEM((2 * DN_HEADS, DN_DK, DN_DV), F32)],
        compiler_params=_cparams(("parallel", "arbitrary")),
    )(q, k, v, gb, q, k, v, gb)


def _rk_prep_kernel(x_ref, xp_ref, xn_ref, mu_ref, w2_ref, w0_ref, a2_ref, a0_ref, kk_w_ref, ka_ref, seg_ref,
                    trif_ref, trib_ref, r_ref, k_ref, v_ref, kk_ref, a_ref, lw_ref, cum_ref):
    cur, shift = _tile_and_shifts(x_ref, xp_ref, xn_ref)
    s = cur + mu_ref[0:1, :] * (shift(-1) - cur) + mu_ref[1:2, :] * (shift(1) - cur)
    w = RK_WIDTH
    r = s[:, 0:w]
    k = s[:, w:2 * w]
    v = s[:, 2 * w:3 * w]
    wd = s[:, 3 * w:3 * w + 2 * RK_LORA]
    ad = s[:, 3 * w + 2 * RK_LORA:3 * w + 4 * RK_LORA]
    lora_w = _dot16(jnp.tanh(wd), w2_ref[...])
    lw = -math.exp(-0.5) * _sigmoid(w0_ref[...] + lora_w)
    lw_ref[0] = lw
    c = RK_CHUNK
    parts = _split3(lw)
    halves = ((trif_ref[...], slice(0, w)), (trib_ref[...], slice(w, 2 * w)))
    cum_ref[0] = jnp.concatenate(
        [jnp.concatenate([sum(jnp.dot(tri, p[r0:r0 + c, cols], preferred_element_type=F32) for p in parts)
                          for tri, cols in halves], axis=1) for r0 in range(0, lw.shape[0], c)], axis=0)
    a = _sigmoid(a0_ref[...] + _dot16(ad, a2_ref[...]))
    kk_raw = k * kk_w_ref[...]
    ssq = _dot_exact_rhs(kk_raw * kk_raw, seg_ref[...])
    kk_ref[0] = (kk_raw * lax.rsqrt(ssq + RMS_EPS)).astype(kk_ref.dtype)
    r_ref[0] = r.astype(r_ref.dtype)
    k_ref[0] = (k * (1.0 + (a - 1.0) * ka_ref[...])).astype(k_ref.dtype)
    v_ref[0] = v.astype(v_ref.dtype)
    a_ref[0] = a.astype(a_ref.dtype)


def _seg_ones(width, group):
    i = np.arange(width) // group
    return (i[:, None] == i[None, :]).astype(np.float32)


def _rk_prep(rk, mu, w0, w2, a0, a2, k_k, k_a):
    b, l, wp = rk.shape
    t = ROW_TILE
    w = RK_WIDTH
    mu_p = jnp.zeros((2, wp), F32).at[:, :RK_SHIFT].set(mu.astype(F32))
    w2cat = jnp.zeros((2 * RK_LORA, 2 * w), F32)
    w2cat = w2cat.at[:RK_LORA, :w].set(w2[0]).at[RK_LORA:, w:].set(w2[1]).astype(BF16)
    w0cat = w0.astype(F32).reshape(1, 2 * w)
    a2p = jnp.zeros((2 * RK_LORA, w), F32).at[:RK_LORA].set(a2).astype(BF16)
    idx = np.arange(RK_CHUNK)
    prev, nxt = _halo_specs(t, wp, 0, l)
    blk = lambda width: pl.BlockSpec((1, t, width), lambda bi, ti: (bi, ti, 0))
    return pl.pallas_call(
        _rk_prep_kernel, name="rk_prep",
        grid=(b, l // t),
        in_specs=[blk(wp), prev, nxt, _full((2, wp)), _full((2 * RK_LORA, 2 * w)), _full((1, 2 * w)),
                  _full((2 * RK_LORA, w)), _full((1, w)), _full((1, w)), _full((1, w)), _full((w, w)),
                  _full((RK_CHUNK, RK_CHUNK)), _full((RK_CHUNK, RK_CHUNK))],
        out_specs=[blk(w)] * 5 + [blk(2 * w)] * 2,
        out_shape=[jax.ShapeDtypeStruct((b, l, w), ACT)] * 5 + [jax.ShapeDtypeStruct((b, l, 2 * w), F32)] * 2,
        compiler_params=_cparams(("parallel", "parallel")),
    )(rk, rk, rk, mu_p, w2cat, w0cat, a2p, a0.astype(F32).reshape(1, w), k_k.astype(F32).reshape(1, w),
      k_a.astype(F32).reshape(1, w), jnp.asarray(_seg_ones(w, RK_HEAD), dtype=BF16),
      jnp.asarray(idx[None, :] <= idx[:, None], dtype=BF16), jnp.asarray(idx[None, :] >= idx[:, None], dtype=BF16))


def _rk_scan_kernel(rf_ref, kf_ref, vf_ref, kkf_ref, af_ref, lwf_ref, cumf_ref, rb_ref, kb_ref, vb_ref, kkb_ref,
                    ab_ref, lwb_ref, cumb_ref, yf_ref, yb_ref, s_ref):
    c = RK_CHUNK
    n_sub = rf_ref.shape[1] // c
    n_pair = RK_WIDTH // LANE

    @pl.when(pl.program_id(1) == 0)
    def _():
        s_ref[...] = jnp.zeros_like(s_ref)

    row = lax.broadcasted_iota(jnp.int32, (c, LANE), 0)
    col = lax.broadcasted_iota(jnp.int32, (c, LANE), 1) % RK_HEAD
    eye2 = (row == col).astype(F32)
    masks2 = ((col <= row, col < row), (col >= row, col > row))
    low_half = lax.broadcasted_iota(jnp.int32, (RK_HEAD, LANE), 1) < RK_HEAD
    same_block = (lax.broadcasted_iota(jnp.int32, (LANE, LANE), 0) // RK_HEAD) == (
        lax.broadcasted_iota(jnp.int32, (LANE, LANE), 1) // RK_HEAD)
    sides = ((rf_ref, kf_ref, vf_ref, kkf_ref, af_ref, lwf_ref, cumf_ref, yf_ref),
             (rb_ref, kb_ref, vb_ref, kkb_ref, ab_ref, lwb_ref, cumb_ref, yb_ref))

    def chunk(i, carry):
        chains = []
        for u, d in [(u, d) for u in range(RK_UNROLL) for d in range(2)]:
            r_ref, k_ref, v_ref, kk_ref, a_ref, lw_ref, cum_ref, y_ref = sides[d]
            j = i * RK_UNROLL + u
            jj = (n_sub - 1 - j) if d else j
            rows = pl.ds(pl.multiple_of(jj * c, c), c)
            lw = lw_ref[0, rows, :]
            cum = cum_ref[0, rows, :]
            tot = cum[0:1] if d else cum[c - 1:c]
            e_neg = jnp.exp(-cum)
            e_tail = jnp.exp(tot - cum)
            e_tot = jnp.exp(tot)
            k = k_ref[0, rows, :].astype(F32)
            v = v_ref[0, rows, :].astype(F32)
            kk = kk_ref[0, rows, :].astype(F32)
            b_vec = kk * a_ref[0, rows, :].astype(F32)
            ra = r_ref[0, rows, :].astype(F32) * jnp.exp(cum)
            aa = -kk * jnp.exp(cum - lw)
            bb = b_vec * e_neg
            kb = k * e_neg
            bt = b_vec * e_tail
            kt = k * e_tail
            for g in range(n_pair):
                lanes = slice(g * LANE, (g + 1) * LANE)
                chains.append(dict(
                    u=u, slot=d * n_pair + g, rows=rows, lanes=lanes, y_ref=y_ref, incl=masks2[d][0],
                    strict=masks2[d][1],
                    aa=aa[:, lanes], ra=ra[:, lanes], bb=bb[:, lanes], kb=kb[:, lanes], v=v[:, lanes],
                    bt=bt[:, lanes], kt=kt[:, lanes], e_tot=e_tot[:, lanes]))
        for ch in chains:
            ch["lhs"] = jnp.concatenate([ch["aa"], ch["ra"]], axis=0)
        gram = [_dot16(ch["lhs"], jnp.concatenate([_pair_diag(ch["bb"]), _pair_diag(ch["kb"])], axis=0), "nt")
                for ch in chains]
        gb = [g[:, :LANE] for g in gram]
        gk = [g[:, LANE:] for g in gram]
        a_ab = [jnp.where(ch["strict"], g[:c], 0.0) for g, ch in zip(gb, chains)]
        m_rb = [jnp.where(ch["incl"], g[c:], 0.0) for g, ch in zip(gb, chains)]
        akrk = [jnp.concatenate([jnp.where(ch["strict"], g[:c], 0.0), jnp.where(ch["incl"], g[c:], 0.0)], axis=0)
                for g, ch in zip(gk, chains)]
        avyv = [_dot16(m, _pair_diag(ch["v"])) for m, ch in zip(akrk, chains)]
        t_inv = _pair_inverses(a_ab, eye2, c)
        tq = [_dot16(t, jnp.concatenate([_pair_diag(ch["aa"]), _pair_diag(x[:c])], axis=1))
              for t, ch, x in zip(t_inv, chains, avyv)]
        yy = [_dot16(m, jnp.concatenate([_pair_diag(x[:, :LANE]), _pair_diag(x[:, LANE:])], axis=1))
              for m, x in zip(m_rb, tq)]
        ya = [ch["ra"] + y[:, :LANE] for ch, y in zip(chains, yy)]
        yb = [y[:, LANE:] + x[c:] for y, x in zip(yy, avyv)]
        wm = [jnp.where(same_block, _dot16(x[:, :LANE], ch["bt"], "tn"), 0.0) for x, ch in zip(tq, chains)]
        hc_full = [_dot16(jnp.concatenate([x[:, LANE:], ch["v"]], axis=0),
                          jnp.concatenate([ch["bt"], ch["kt"]], axis=0), "tn") for x, ch in zip(tq, chains)]
        hc = [jnp.where(low_half, x[:RK_HEAD], x[RK_HEAD:]) for x in hc_full]
        state = [s_ref[slot] for slot in range(2 * n_pair)]
        for u in range(RK_UNROLL):
            mine = [n for n, ch in enumerate(chains) if ch["u"] == u]
            cur = [state[chains[n]["slot"]] for n in mine]
            outs = [_dot16(ya[n], _pair_diag(s), "nt") + yb[n] for n, s in zip(mine, cur)]
            news = [s * chains[n]["e_tot"] + _dot16(s, wm[n]) + hc[n] for n, s in zip(mine, cur)]
            for n, y, s_new in zip(mine, outs, news):
                ch = chains[n]
                ch["y_ref"][0, ch["rows"], ch["lanes"]] = y.astype(ch["y_ref"].dtype)
                state[ch["slot"]] = s_new
        for slot in range(2 * n_pair):
            s_ref[slot] = state[slot]
        return carry

    lax.fori_loop(0, n_sub // RK_UNROLL, chunk, 0)


def _rk_scan(r, k, v, kk, a, lw, cum):
    b, l, w = r.shape
    t = RK_SCAN_TILE
    n = l // t
    fwd = pl.BlockSpec((1, t, w), lambda bi, ti: (bi, ti, 0))
    bwd = pl.BlockSpec((1, t, w), lambda bi, ti: (bi, n - 1 - ti, 0))
    bwd_lw = pl.BlockSpec((1, t, w), lambda bi, ti: (bi, n - 1 - ti, 1))
    return pl.pallas_call(
        _rk_scan_kernel, name="rk_scan",
        grid=(b, n),
        in_specs=[fwd] * 7 + [bwd] * 5 + [bwd_lw] * 2,
        out_specs=[fwd, bwd],
        out_shape=[jax.ShapeDtypeStruct((b, l, w), ACT)] * 2,
        scratch_shapes=[pltpu.VMEM((2 * RK_WIDTH // LANE, RK_HEAD, LANE), F32)],
        compiler_params=_cparams(("parallel", "arbitrary")),
    )(r, k, v, kk, a, lw, cum, r, k, v, kk, a, lw, cum)


def _even_mix_kernel(of_ref, ob_ref, dg_ref, dnw_ref, yf_ref, yb_ref, r_ref, k_ref, v_ref, rg_ref,
                     rk_ref, lnw_ref, lnb_ref, segm_ref, seg1_ref, o_ref):
    f32 = lambda ref: ref[0].astype(F32)
    o = f32(of_ref) + f32(ob_ref)
    gate = _silu(f32(dg_ref))
    for h in range(DN_HEADS):
        lanes = slice(h * DN_DV, (h + 1) * DN_DV)
        oh = o[:, lanes]
        ms = jnp.mean(oh * oh, axis=-1, keepdims=True)
        o_ref[0, :, lanes] = (oh * lax.rsqrt(ms + RMS_EPS) * dnw_ref[...] * gate[:, lanes]).astype(o_ref.dtype)
    wkv = f32(yf_ref) + f32(yb_ref)
    mean = _dot_exact_rhs(wkv, segm_ref[...])
    cen = wkv - mean
    var = _dot_exact_rhs(cen * cen, segm_ref[...])
    wkv = cen * lax.rsqrt(var + RK_GN_EPS) * lnw_ref[...] + lnb_ref[...]
    bonus = _dot_exact_rhs(f32(r_ref) * f32(k_ref) * rk_ref[...], seg1_ref[...]) * f32(v_ref)
    o_ref[0, :, DN_WIDTH:] = ((wkv + bonus) * _silu(f32(rg_ref))).astype(o_ref.dtype)


def _even_mix(o_f, o_b, dn_gate, dn_norm, y_f, y_b, r, k, v, rk_gate, r_k, ln_w, ln_b):
    b, l, _ = o_f.shape
    t = ROW_TILE
    w = RK_WIDTH
    blk = lambda width: pl.BlockSpec((1, t, width), lambda bi, ti: (bi, ti, 0))
    seg1 = jnp.asarray(_seg_ones(w, RK_HEAD), dtype=BF16)
    segm = jnp.asarray(_seg_ones(w, RK_HEAD) / RK_HEAD, dtype=BF16)
    return pl.pallas_call(
        _even_mix_kernel, name="even_mix",
        grid=(b, l // t),
        in_specs=[blk(DN_WIDTH), blk(DN_WIDTH), blk(DN_WIDTH), _full((1, DN_DV)),
                  blk(w), blk(w), blk(w), blk(w), blk(w), blk(w),
                  _full((1, w)), _full((1, w)), _full((1, w)), _full((w, w)), _full((w, w))],
        out_specs=blk(DN_WIDTH + w),
        out_shape=jax.ShapeDtypeStruct((b, l, DN_WIDTH + w), ACT),
        compiler_params=_cparams(("parallel", "parallel")),
    )(o_f, o_b, dn_gate, dn_norm.astype(F32).reshape(1, DN_DV), y_f, y_b, r, k, v, rk_gate,
      r_k.astype(F32).reshape(1, w), ln_w.astype(F32).reshape(1, w), ln_b.astype(F32).reshape(1, w), segm, seg1)


def _even_layer(h, w_in, dn_conv, dn_a_log, dn_dt_bias, dn_norm, rk_mu, rk_w0, rk_w2, rk_a0, rk_a2,
                rk_k_k, rk_k_a, rk_r_k, rk_ln_w, rk_ln_b):
    b, l, d = h.shape
    s0 = DN_QKV
    s1 = s0 + DN_AB
    s2 = s1 + DN_WIDTH
    s3 = s2 + RK_SHIFT
    pad = lambda m, width: jnp.pad(m, ((0, 0), (0, width - m.shape[1])))
    widths = (DN_QKV, LANE, DN_WIDTH, RK_SHIFT_PAD, RK_WIDTH)
    w16 = jnp.concatenate([w_in[:, :s0], pad(w_in[:, s0:s1], LANE), w_in[:, s1:s2],
                           pad(w_in[:, s2:s3], RK_SHIFT_PAD), w_in[:, s3:]], axis=1).astype(BF16)
    qkv, ab, dn_gate, rk, rk_gate = _project(h.reshape(b * l, d), w16, widths, (ACT, F32, ACT, ACT, ACT))
    r3 = lambda m: m.reshape(b, l, m.shape[-1])
    q, k, v, gb = _dn_prep(r3(qkv), r3(ab), dn_conv, dn_a_log, dn_dt_bias)
    o_f, o_b = _dn_scan(q, k, v, gb)
    r, kr, vr, kk, a, lw, cum = _rk_prep(r3(rk), rk_mu, rk_w0, rk_w2, rk_a0, rk_a2, rk_k_k, rk_k_a)
    y_f, y_b = _rk_scan(r, kr, vr, kk, a, lw, cum)
    return _even_mix(o_f, o_b, r3(dn_gate), dn_norm, y_f, y_b, r, kr, vr, r3(rk_gate),
                     rk_r_k, rk_ln_w, rk_ln_b)


def _dft_geometry(l):
    nf = 2 * l
    p = nf // DFT_Q
    n1 = p // 2
    k1 = p // 2 + 1
    k1p = -(-k1 // SUBLANE) * SUBLANE
    return nf, p, n1, k1, k1p


@functools.lru_cache(maxsize=None)
def _dft_tables(l):
    nf, p, n1c, k1c, k1p = _dft_geometry(l)
    q = DFT_Q
    n2 = np.arange(q)[:, None, None]
    k1 = np.arange(k1c)[None, :, None]
    n1 = np.arange(n1c)[None, None, :]
    ph = -2.0 * np.pi * (((n1 * k1) % p) / p + ((n2 * k1) % nf) / nf)
    fa = np.zeros((q, 2 * k1p, n1c))
    fa[:, :k1c] = np.cos(ph)
    fa[:, k1p:k1p + k1c] = np.sin(ph)
    wgt = np.full((k1c,), 2.0)
    wgt[0] = 1.0
    wgt[-1] = 1.0
    th = -ph.transpose(0, 2, 1)
    gd = np.zeros((q, n1c, 2 * k1p))
    gd[:, :, :k1c] = np.cos(th) * wgt / nf
    gd[:, :, k1p:k1p + k1c] = -np.sin(th) * wgt / nf
    a = np.arange(q)
    ang = -2.0 * np.pi * ((a[:, None] * a[None, :]) % q) / q
    cr, ci = np.cos(ang), np.sin(ang)
    fb = np.block([[cr, -ci], [ci, cr]])
    fc = np.block([[cr, ci], [-ci, cr]])

    return tuple(m.astype(np.float32).astype(BF16) for m in (fa, fb, fc, gd))


def _fdot(f, x):
    return jnp.dot(f, x.astype(BF16), preferred_element_type=F32)


DFT_GROUP_N2 = 8
DFT_GROUP_K1 = 24


def _k1_group(k1p, most=DFT_GROUP_K1):
    return max(g for g in range(2, most + 1, 2) if k1p % g == 0)
DFT_PITCH = DFT_Q + SUBLANE


def _stage_a(src, y_re, y_im, fa, geo):
    nf, p, n1c, k1c, k1p = geo
    g = DFT_GROUP_N2

    def body(i, carry):
        n2s = [i * g + t for t in range(g)]
        slabs = [src[pl.ds(n2, n1c, stride=DFT_PITCH), :] for n2 in n2s]
        outs = [_fdot(fa[n2], slab) for n2, slab in zip(n2s, slabs)]
        for n2, out in zip(n2s, outs):
            y_re[pl.ds(n2, k1p, stride=DFT_PITCH), :] = out[:k1p]
            y_im[pl.ds(n2, k1p, stride=DFT_PITCH), :] = out[k1p:]
        return carry

    lax.fori_loop(0, DFT_Q // g, body, 0)


def _fdot_pairs(f, xs):
    outs = []
    for a, b in zip(xs[0::2], xs[1::2]):
        z = _fdot(f, jnp.concatenate([a, b], axis=1))
        outs += [z[:, :a.shape[1]], z[:, a.shape[1]:]]
    return outs


def _stage_b(y_re, y_im, i, fb, g):
    k1s = [i * g + t for t in range(g)]
    rows = [pl.ds(pl.multiple_of(k1 * DFT_PITCH, SUBLANE), DFT_Q) for k1 in k1s]
    ws = [jnp.concatenate([y_re[r, :], y_im[r, :]], axis=0) for r in rows]
    return k1s, rows, _fdot_pairs(fb[...], ws)


def _hy_conv_kernel(u_ref, m_ref, skip_ref, hr_ref, hi_ref, fa, fb, fc, gd, o_ref, pad, y_re, y_im, *, geo):
    nf, p, n1c, k1c, k1p = geo
    group = _k1_group(k1p)
    for n1 in range(n1c):
        pad[n1 * DFT_PITCH:n1 * DFT_PITCH + DFT_Q, :] = u_ref[0, n1 * DFT_Q:(n1 + 1) * DFT_Q, :].astype(F32)
    _stage_a(pad, y_re, y_im, fa, geo)

    def mid(i, carry):
        k1s, rows, zs = _stage_b(y_re, y_im, i, fb, group)
        prods = []
        for k1, z in zip(k1s, zs):
            zr, zi = z[:DFT_Q], z[DFT_Q:]
            hrows = pl.ds(pl.multiple_of(k1 * DFT_Q, DFT_Q), DFT_Q)
            hr = hr_ref[hrows, :]
            hi = hi_ref[hrows, :]
            prods.append(jnp.concatenate([zr * hr - zi * hi, zr * hi + zi * hr], axis=0))
        outs = _fdot_pairs(fc[...], prods)
        for r, a in zip(rows, outs):
            y_re[r, :] = a[:DFT_Q]
            y_im[r, :] = a[DFT_Q:]
        return carry

    lax.fori_loop(0, k1p // group, mid, 0)

    def last(i, carry):
        n2s = [i * DFT_GROUP_N2 + t for t in range(DFT_GROUP_N2)]
        ins = [jnp.concatenate([y_re[pl.ds(n2, k1p, stride=DFT_PITCH), :], y_im[pl.ds(n2, k1p, stride=DFT_PITCH), :]],
                               axis=0) for n2 in n2s]
        outs = [_fdot(gd[n2], a) for n2, a in zip(n2s, ins)]
        for n2, out in zip(n2s, outs):
            pad[pl.ds(n2, n1c, stride=DFT_PITCH), :] = out
        return carry

    lax.fori_loop(0, DFT_Q // DFT_GROUP_N2, last, 0)

    skip = skip_ref[...]
    for n1 in range(n1c):
        rows = slice(n1 * DFT_Q, (n1 + 1) * DFT_Q)
        conv = pad[n1 * DFT_PITCH:n1 * DFT_PITCH + DFT_Q, :]
        o_ref[0, rows, :] = (m_ref[0, rows, :].astype(F32)
                             * (conv + skip * u_ref[0, rows, :].astype(F32))).astype(o_ref.dtype)


def _single(shape, index_map):
    return pl.BlockSpec(shape, index_map, pipeline_mode=pl.Buffered(1))


def _hy_conv(u, mult, skip, h_re, h_im, order, tables):
    b, l, ch = u.shape
    geo = _dft_geometry(l)
    nf, p, n1c, k1c, k1p = geo
    ct = HY_CT
    consts = [jnp.asarray(t) for t in tables]
    seq = pl.BlockSpec((1, l, ct), lambda ci, bi: (bi, 0, ci))
    spec = _single((None, k1p * DFT_Q, ct), lambda ci, bi: (order, 0, ci))
    cspecs = [_single(c.shape, (lambda ci, bi, nd=c.ndim: (0,) * nd)) for c in consts]
    ysc = pltpu.VMEM((k1p * DFT_PITCH, ct), F32)
    return pl.pallas_call(
        functools.partial(_hy_conv_kernel, geo=geo), name="hy_conv",
        grid=(ch // ct, b),
        in_specs=[seq, seq, pl.BlockSpec((1, ct), lambda ci, bi: (0, ci)), spec, spec] + cspecs,
        out_specs=seq,
        out_shape=jax.ShapeDtypeStruct((b, l, ch), ACT),
        scratch_shapes=[pltpu.VMEM((n1c * DFT_PITCH, ct), F32), ysc, ysc],
        compiler_params=_cparams(("parallel", "parallel")),
    )(u, mult, skip, h_re, h_im, *consts)


def _hy_mlp_kernel(f_ref, w1_ref, b1_ref, w2_ref, b2_ref, w3_ref, b3_ref, fr_ref, o_ref):
    fr = fr_ref[...]
    hdn = jnp.sin(fr * (_dot32(f_ref[...], w1_ref[...]) + b1_ref[...]))
    hdn = jnp.sin(fr * (_dot32(hdn, w2_ref[...]) + b2_ref[...]))
    o_ref[...] = jnp.sin(fr * (_dot32(hdn, w3_ref[...]) + b3_ref[...]))


def _hy_mlp(feats, w1, b1, w2, b2, w3, b3, freq):
    l = feats.shape[0]
    t = min(l, 1024)
    fw = HY_FW
    row = lambda a: a.astype(F32).reshape(1, fw)
    w1p = jnp.zeros((LANE, fw), F32).at[:HY_EMB].set(w1.astype(F32))
    return pl.pallas_call(
        _hy_mlp_kernel, name="hy_mlp",
        grid=(l // t,),
        in_specs=[pl.BlockSpec((t, LANE), lambda i: (i, 0)), _full((LANE, fw)), _full((1, fw)), _full((fw, fw)),
                  _full((1, fw)), _full((fw, fw)), _full((1, fw)), _full((1, fw))],
        out_specs=pl.BlockSpec((t, fw), lambda i: (i, 0)),
        out_shape=jax.ShapeDtypeStruct((l, fw), F32),
        compiler_params=_cparams(("parallel",)),
    )(feats, w1p, row(b1), w2.astype(F32), row(b2), w3.astype(F32), row(b3), row(freq))


def _hy_filter_kernel(hdn_ref, wf_ref, wb_ref, df_ref, db_ref, fa, fb,
                      hr_ref, hi_ref, filt, yr_f, yi_f, yr_b, yi_b, *, geo):
    nf, p, n1c, k1c, k1p = geo
    group = _k1_group(k1p, SUBLANE)

    def build(w_ref, d_ref, drop_first):
        grp = 4

        def body(i, acc):
            n1s = [i * grp + t for t in range(grp)]
            hxs = [hdn_ref[pl.ds(pl.multiple_of(n1 * DFT_Q, DFT_Q), DFT_Q), :] for n1 in n1s]
            raw = [_dot32(hx, w_ref[0]) for hx in hxs]
            for n1, hx, hv in zip(n1s, hxs, raw):
                hv = hv * jnp.exp(-hx[:, HY_FW:HY_FW + 1] * jnp.abs(d_ref[0]))
                if drop_first:
                    pos = lax.broadcasted_iota(jnp.int32, hv.shape, 0) + n1 * DFT_Q
                    hv = jnp.where(pos == 0, 0.0, hv)
                filt[pl.ds(pl.multiple_of(n1 * DFT_PITCH, SUBLANE), DFT_Q), :] = hv
                acc = acc + jnp.sum(jnp.abs(hv), axis=0, keepdims=True)
            return acc

        return lax.fori_loop(0, n1c // grp, body, jnp.zeros((1, filt.shape[1]), F32))

    l1 = build(wf_ref, df_ref, False)
    _stage_a(filt, yr_f, yi_f, fa, geo)
    l1 = l1 + build(wb_ref, db_ref, True)
    _stage_a(filt, yr_b, yi_b, fa, geo)
    inv = 1.0 / (l1 + RMS_EPS)

    def mid(i, carry):
        k1s, _, zf = _stage_b(yr_f, yi_f, i, fb, group)
        _, _, zb = _stage_b(yr_b, yi_b, i, fb, group)
        for k1, f, b in zip(k1s, zf, zb):
            r = pl.ds(pl.multiple_of(k1 * DFT_Q, DFT_Q), DFT_Q)
            hr_ref[0, r, :] = (f[:DFT_Q] + b[:DFT_Q]) * inv
            hi_ref[0, r, :] = (f[DFT_Q:] - b[DFT_Q:]) * inv
        return carry

    lax.fori_loop(0, k1p // group, mid, 0)


def _hy_filters(hdn, tcol, w_out, deltas, tables, l):
    geo = _dft_geometry(l)
    nf, p, n1c, k1c, k1p = geo
    ch = w_out.shape[1] // (2 * HY_ORDER)
    ct = HY_CT
    nct = ch // ct
    hdn_x = jnp.concatenate([hdn, tcol, jnp.zeros((l, LANE - HY_FW - 1), F32)], axis=1)
    w4 = w_out.astype(F32).reshape(HY_FW, 2 * HY_ORDER, ch).transpose(1, 0, 2)
    w4 = jnp.pad(w4, ((0, 0), (0, LANE - HY_FW), (0, 0)))
    d4 = deltas.astype(F32).reshape(2 * HY_ORDER, 1, ch)
    consts = [jnp.asarray(t) for t in tables[:2]]
    cspecs = [_single(c.shape, (lambda o, ci, nd=c.ndim: (0,) * nd)) for c in consts]
    wspec = lambda d: pl.BlockSpec((1, LANE, ct), lambda o, ci: (2 * o + d, 0, ci))
    dspec = lambda d: pl.BlockSpec((1, 1, ct), lambda o, ci: (2 * o + d, 0, ci))
    ospec = pl.BlockSpec((1, k1p * DFT_Q, ct), lambda o, ci: (o, 0, ci))
    ysc = pltpu.VMEM((k1p * DFT_PITCH, ct), F32)
    return pl.pallas_call(
        functools.partial(_hy_filter_kernel, geo=geo), name="hy_filters",
        grid=(HY_ORDER, nct),
        in_specs=[_single((l, LANE), lambda o, ci: (0, 0)), wspec(0), wspec(1), dspec(0), dspec(1)] + cspecs,
        out_specs=[ospec, ospec],
        out_shape=[jax.ShapeDtypeStruct((HY_ORDER, k1p * DFT_Q, ch), F32)] * 2,
        scratch_shapes=[pltpu.VMEM((n1c * DFT_PITCH, ct), F32), ysc, ysc, ysc, ysc],
        compiler_params=_cparams(("parallel", "parallel")),
    )(hdn_x, w4, w4, d4, d4, *consts)


def _position_features(l):
    bands = (HY_EMB - 1) // 2
    t = jnp.linspace(0.0, 1.0, l, dtype=F32)[:, None]
    f = jnp.linspace(1e-4, bands - 1, bands, dtype=F32)[None, :]
    ang = (2.0 * math.pi / l) * jnp.arange(l, dtype=F32)[:, None] * f
    feats = jnp.concatenate([t, jnp.cos(ang), -jnp.sin(ang)], axis=-1)
    return jnp.pad(feats, ((0, 0), (0, LANE - HY_EMB))), t


def _hy_prep_kernel(x_ref, xp_ref, xn_ref, g_ref, cw_ref, cb_ref, x1_ref, m2_ref, v_ref):
    cur, shift = _tile_and_shifts(x_ref, xp_ref, xn_ref)
    y = shift(-1) * cw_ref[0:1, :] + cur * cw_ref[1:2, :] + shift(1) * cw_ref[2:3, :] + cb_ref[...]
    c = g_ref.shape[2]
    x1_ref[0] = y[:, :c].astype(x1_ref.dtype)
    m2_ref[0] = (y[:, c:2 * c] * _silu(g_ref[0].astype(F32))).astype(m2_ref.dtype)
    v_ref[0] = y[:, 2 * c:].astype(v_ref.dtype)


def _hy_prep(xv, gate, conv_w, conv_b):
    b, l, w3 = xv.shape
    c = gate.shape[2]
    t = ROW_TILE
    prev, nxt = _halo_specs(t, w3, 0, l)
    blk = lambda width: pl.BlockSpec((1, t, width), lambda bi, ti: (bi, ti, 0))
    return pl.pallas_call(
        _hy_prep_kernel, name="hy_prep",
        grid=(b, l // t),
        in_specs=[blk(w3), prev, nxt, blk(c), _full((HY_SHORT, w3)), _full((1, w3))],
        out_specs=[blk(c)] * 3,
        out_shape=[jax.ShapeDtypeStruct((b, l, c), ACT)] * 3,
        compiler_params=_cparams(("parallel", "parallel")),
    )(xv, xv, xv, gate, conv_w.astype(F32), conv_b.astype(F32).reshape(1, w3))


def _odd_layer(h, w_in, conv_w, conv_b, f_w1, f_b1, f_w2, f_b2, f_w3, f_b3, f_freq, f_out, deltas, skip):
    b, l, d = h.shape
    c = skip.shape[1]
    xv, gate = _project(h.reshape(b * l, d), w_in.astype(BF16), (3 * c, c), (ACT, ACT))
    x1, m2, v = _hy_prep(xv.reshape(b, l, 3 * c), gate.reshape(b, l, c), conv_w, conv_b)
    tables = _dft_tables(l)
    feats, tcol = _position_features(l)
    hdn = _hy_mlp(feats, f_w1, f_b1, f_w2, f_b2, f_w3, f_b3, f_freq)
    h_re, h_im = _hy_filters(hdn, tcol, f_out, deltas, tables, l)
    skip = skip.astype(F32)
    z = _hy_conv(v, x1, skip[0:1], h_re, h_im, 0, tables)
    return _hy_conv(z, m2, skip[1:2], h_re, h_im, 1, tables)


def kernel(x, p, even_w_in, dn_conv, dn_a_log, dn_dt_bias, dn_norm, rk_mu, rk_w0, rk_w2, rk_a0, rk_a2, rk_k_k, rk_k_a, rk_r_k, rk_ln_w, rk_ln_b, odd_w_in, hy_conv_w, hy_conv_b, hy_ffn_w1, hy_ffn_b1, hy_ffn_w2, hy_ffn_b2, hy_ffn_w3, hy_ffn_b3, hy_ffn_freq, hy_ffn_out, hy_deltas, hy_skip, w_out, ln_g, ln_b, ple_w, ple_norm, ple_gate):
    b, l, d = x.shape
    depth = p.shape[0]
    alpha = (2.0 * depth) ** 0.25
    h = x
    for i in range(depth):
        j = i // 2
        if i % 2 == 0:
            mix = _even_layer(h, even_w_in[j], dn_conv[j], dn_a_log[j], dn_dt_bias[j], dn_norm[j], rk_mu[j],
                              rk_w0[j], rk_w2[j], rk_a0[j], rk_a2[j], rk_k_k[j].reshape(-1), rk_k_a[j].reshape(-1),
                              rk_r_k[j].reshape(-1), rk_ln_w[j], rk_ln_b[j])
        else:
            mix = _odd_layer(h, odd_w_in[j], hy_conv_w[j], hy_conv_b[j], hy_ffn_w1[j], hy_ffn_b1[j],
                             hy_ffn_w2[j], hy_ffn_b2[j], hy_ffn_w3[j], hy_ffn_b3[j], hy_ffn_freq[j],
                             hy_ffn_out[j], hy_deltas[j], hy_skip[j])
        h2 = _post_layer(h.reshape(b * l, d), mix.reshape(b * l, mix.shape[-1]), p.reshape(depth, b * l, p.shape[-1]),
                         i, w_out[i], ple_w[i], ple_gate[i], ln_g[i], ln_b[i], ple_norm[i], alpha)
        h = h2.reshape(b, l, d)
    return h
```

```python
import functools
import math

import numpy as np
import jax
import jax.numpy as jnp
from jax import lax
from jax.experimental import pallas as pl
from jax.experimental.pallas import tpu as pltpu

F32 = jnp.float32
BF16 = jnp.bfloat16

LN_EPS = 1e-5
RMS_EPS = 1e-6

DN_HEADS = 4
DN_DK = 128
DN_DV = 128
DN_WIDTH = DN_HEADS * DN_DV
DN_QKV = 2 * DN_HEADS * DN_DK + DN_WIDTH
DN_AB = 4 * DN_HEADS
DN_CONV = 5
DN_CHUNK = 64
DN_UNROLL = 4

RK_HEADS = 8
RK_HEAD = 64
RK_WIDTH = RK_HEADS * RK_HEAD
RK_LORA = 64
RK_SHIFT = 3 * RK_WIDTH + 3 * RK_LORA
RK_SHIFT_PAD = 1792
RK_GN_EPS = 64e-5
RK_CHUNK = 64
RK_UNROLL = 2

HY_ORDER = 2
HY_SHORT = 3
HY_EMB = 33
HY_FW = 64

LANE = 128
SUBLANE = 8
DFT_Q = 128
VMEM_LIMIT = 56 * 1024 * 1024

ROW_TILE = 1024
SCAN_TILE = 1024
RK_SCAN_TILE = 512
HY_CT = 128

HI = lax.Precision.HIGHEST
ACT = BF16


def _cparams(sem):
    return pltpu.CompilerParams(dimension_semantics=sem, vmem_limit_bytes=VMEM_LIMIT)


_DIMS = {
    "nn": (((1,), (0,)), ((), ())),
    "nt": (((1,), (1,)), ((), ())),
    "tn": (((0,), (0,)), ((), ())),
}


def _dot16(a, b, dims="nn"):
    return lax.dot_general(a.astype(BF16), b.astype(BF16), _DIMS[dims], preferred_element_type=F32)


def _dot32(a, b, dims="nn"):
    return lax.dot_general(a.astype(F32), b.astype(F32), _DIMS[dims], precision=HI,
                           preferred_element_type=F32)


def _split2(x):
    hi = x.astype(BF16)
    lo = (x - hi.astype(F32)).astype(BF16)
    return hi, lo


def _dot_exact_rhs(x, m16):
    hi, lo = _split2(x)
    return (jnp.dot(hi, m16, preferred_element_type=F32) + jnp.dot(lo, m16, preferred_element_type=F32))


def _sigmoid(x):
    return 1.0 / (1.0 + jnp.exp(-x))


def _silu(x):
    return x * _sigmoid(x)


def _softplus(x):
    return jnp.maximum(x, 0.0) + jnp.log1p(jnp.exp(-jnp.abs(x)))


HALO = 16


SHIFT_ROWS = 128


def _tile_and_shifts(x_ref, xp_ref, xn_ref):
    assert x_ref.dtype == BF16
    t_idx = pl.program_id(1)
    t = x_ref.shape[1]
    cur16 = x_ref[0]
    zero = jnp.zeros_like(xp_ref[0])
    ext = jnp.concatenate([jnp.where(t_idx > 0, xp_ref[0], zero), cur16,
                           jnp.where(t_idx < pl.num_programs(1) - 1, xn_ref[0], zero)], axis=0)
    k = SHIFT_ROWS + 2 * HALO
    row = lax.broadcasted_iota(jnp.int32, (SHIFT_ROWS, k), 0)
    col = lax.broadcasted_iota(jnp.int32, (SHIFT_ROWS, k), 1)

    def shift(d):
        sel = jnp.where(col == row + (HALO + d), 1.0, 0.0).astype(BF16)
        return jnp.concatenate([jnp.dot(sel, ext[r0:r0 + k], preferred_element_type=F32)
                                for r0 in range(0, t, SHIFT_ROWS)], axis=0)

    return cur16.astype(F32), shift


def _halo_specs(t_rows, width, col, l_total):
    nb = t_rows // HALO
    last = l_total // HALO - 1
    prev = pl.BlockSpec((1, HALO, width), lambda b, t: (b, jnp.maximum(t * nb - 1, 0), col))
    nxt = pl.BlockSpec((1, HALO, width), lambda b, t: (b, jnp.minimum((t + 1) * nb, last), col))
    return prev, nxt


def _full(shape):
    nd = len(shape)
    return pl.BlockSpec(shape, lambda *_: (0,) * nd)


def _proj_kernel(a_ref, w_ref, *o_refs, offs):
    a = a_ref[...].astype(BF16)
    for o_ref, (lo, hi) in zip(o_refs, offs):
        o_ref[...] = jnp.dot(a, w_ref[:, lo:hi], preferred_element_type=F32).astype(o_ref.dtype)


def _project(a, w16, widths, dtypes):
    m, k = a.shape
    offs, o = [], 0
    for w in widths:
        offs.append((o, o + w))
        o += w
    n = o
    tm = ROW_TILE
    return pl.pallas_call(
        functools.partial(_proj_kernel, offs=tuple(offs)), name="project",
        grid=(m // tm,),
        in_specs=[pl.BlockSpec((tm, k), lambda i: (i, 0)), _full((k, n))],
        out_specs=[pl.BlockSpec((tm, w), lambda i: (i, 0)) for w in widths],
        out_shape=[jax.ShapeDtypeStruct((m, w), dt) for w, dt in zip(widths, dtypes)],
        compiler_params=_cparams(("parallel",)),
    )(a, w16)


def _post_kernel(h_ref, mix_ref, p_ref, wo_ref, pw_ref, pg_ref, lng_ref, lnb_ref, pn_ref, o_ref, *, alpha):
    t = alpha * h_ref[...] + jnp.dot(mix_ref[...], wo_ref[...], preferred_element_type=F32)
    mu = jnp.mean(t, axis=-1, keepdims=True)
    tc = t - mu
    var = jnp.mean(tc * tc, axis=-1, keepdims=True)
    y = tc * lax.rsqrt(var + LN_EPS) * lng_ref[...] + lnb_ref[...]
    e = jnp.dot(p_ref[...].astype(BF16), pw_ref[...], preferred_element_type=F32)
    e = e * lax.rsqrt(jnp.mean(e * e, axis=-1, keepdims=True) + RMS_EPS) * pn_ref[...]
    gate = _sigmoid(jnp.dot(y.astype(BF16), pg_ref[...], preferred_element_type=F32))
    o_ref[...] = y + gate * e


def _post_layer(h, mix, p_all, layer, w_out, ple_w, ple_gate, ln_g, ln_b, ple_norm, alpha):
    m, d = h.shape
    pd = p_all.shape[2]
    tm = ROW_TILE
    row = lambda w: pl.BlockSpec((tm, w), lambda i: (i, 0))
    p_spec = pl.BlockSpec((None, tm, pd), lambda i: (layer, i, 0))
    return pl.pallas_call(
        functools.partial(_post_kernel, alpha=alpha), name="post_layer",
        grid=(m // tm,),
        in_specs=[row(d), row(mix.shape[1]), p_spec, _full(w_out.shape), _full(ple_w.shape),
                  _full(ple_gate.shape), _full((1, d)), _full((1, d)), _full((1, d))],
        out_specs=row(d),
        out_shape=jax.ShapeDtypeStruct((m, d), F32),
        compiler_params=_cparams(("parallel",)),
    )(h, mix, p_all, w_out.astype(BF16), ple_w.astype(BF16), ple_gate.astype(BF16),
      ln_g.reshape(1, d), ln_b.reshape(1, d), ple_norm.reshape(1, d))


def _split3(x):
    t1 = x.astype(BF16)
    rem = x - t1.astype(F32)
    t2 = rem.astype(BF16)
    return t1, t2, (rem - t2.astype(F32)).astype(BF16)


def _dn_prep_kernel(x_ref, xp_ref, xn_ref, ab_ref, cw_ref, ga_ref, gbias_ref, trif_ref, trib_ref,
                    q_ref, k_ref, v_ref, gb_ref):
    cur, shift = _tile_and_shifts(x_ref, xp_ref, xn_ref)
    pad = DN_CONV // 2
    acc = cur * cw_ref[pad:pad + 1, :]
    for j in range(DN_CONV):
        if j != pad:
            acc = acc + shift(j - pad) * cw_ref[j:j + 1, :]
    y = _silu(acc)
    nqk = DN_HEADS * DN_DK
    for h in range(DN_HEADS):
        qh = y[:, h * DN_DK:(h + 1) * DN_DK]
        kh = y[:, nqk + h * DN_DK:nqk + (h + 1) * DN_DK]
        qn = lax.rsqrt(jnp.sum(qh * qh, axis=-1, keepdims=True) + RMS_EPS) * (DN_DK ** -0.5)
        kn = lax.rsqrt(jnp.sum(kh * kh, axis=-1, keepdims=True) + RMS_EPS)
        q_ref[0, :, h * DN_DK:(h + 1) * DN_DK] = (qh * qn).astype(q_ref.dtype)
        k_ref[0, :, h * DN_DK:(h + 1) * DN_DK] = (kh * kn).astype(k_ref.dtype)
    v_ref[0] = y[:, 2 * nqk:].astype(v_ref.dtype)
    ab = ab_ref[0]
    lane = lax.broadcasted_iota(jnp.int32, ab.shape, 1)
    g = ga_ref[...] * _softplus(ab + gbias_ref[...])
    parts = _split3(g)
    c = DN_CHUNK
    tri_f, tri_b = trif_ref[...], trib_ref[...]
    chunks = range(0, g.shape[0], c)
    cum_f = jnp.concatenate([sum(jnp.dot(tri_f, p[r:r + c], preferred_element_type=F32) for p in parts)
                             for r in chunks], axis=0)
    cum_b = jnp.concatenate([sum(jnp.dot(tri_b, p[r:r + c], preferred_element_type=F32) for p in parts)
                             for r in chunks], axis=0)
    gb_ref[0] = jnp.where(lane < DN_HEADS, cum_f, jnp.where(lane < 2 * DN_HEADS, cum_b, _sigmoid(ab)))


def _dn_prep(qkv, ab, conv_w, a_log, dt_bias):
    b, l, _ = qkv.shape
    t = SCAN_TILE
    ga = jnp.zeros((1, LANE), F32).at[0, :2 * DN_HEADS].set(-jnp.exp(a_log.astype(F32)).reshape(-1))
    gbias = jnp.zeros((1, LANE), F32).at[0, :2 * DN_HEADS].set(dt_bias.astype(F32).reshape(-1))
    prev, nxt = _halo_specs(t, DN_QKV, 0, l)
    blk = lambda w: pl.BlockSpec((1, t, w), lambda bi, ti: (bi, ti, 0))
    i = np.arange(DN_CHUNK)
    tri_f = jnp.asarray(i[None, :] <= i[:, None], dtype=BF16)
    tri_b = jnp.asarray(i[None, :] >= i[:, None], dtype=BF16)
    return pl.pallas_call(
        _dn_prep_kernel, name="dn_prep",
        grid=(b, l // t),
        in_specs=[blk(DN_QKV), prev, nxt, blk(LANE), _full((DN_CONV, DN_QKV)), _full((1, LANE)), _full((1, LANE)),
                  _full((DN_CHUNK, DN_CHUNK)), _full((DN_CHUNK, DN_CHUNK))],
        out_specs=[blk(DN_WIDTH), blk(DN_WIDTH), blk(DN_WIDTH), blk(LANE)],
        out_shape=[jax.ShapeDtypeStruct((b, l, DN_WIDTH), ACT)] * 3 + [jax.ShapeDtypeStruct((b, l, LANE), F32)],
        compiler_params=_cparams(("parallel", "parallel")),
    )(qkv, qkv, qkv, ab, conv_w.astype(F32), ga, gbias, tri_f, tri_b)


HALF = LANE // 2


def _pair_diag(x):
    low = lax.broadcasted_iota(jnp.int32, x.shape, 1) < HALF
    return jnp.concatenate([jnp.where(low, x, 0.0), jnp.where(low, 0.0, x)], axis=0).astype(BF16)


def _pair_inverses(xs, eye2, c):
    rs = [eye2 + x for x in xs]
    ps = [_dot16(x, _pair_diag(x)) for x in xs]
    for _ in range(int(math.log2(c)) - 2):
        zs = [_dot16(jnp.concatenate([r, p], axis=0), _pair_diag(p)) for r, p in zip(rs, ps)]
        rs = [r + z[:c] for r, z in zip(rs, zs)]
        ps = [z[c:] for z in zs]
    return [r + _dot16(r, _pair_diag(p)) for r, p in zip(rs, ps)]


def _dn_scan_kernel(qf_ref, kf_ref, vf_ref, gf_ref, qb_ref, kb_ref, vb_ref, gb_ref, of_ref, ob_ref, s_ref):
    c = DN_CHUNK
    n_sub = qf_ref.shape[1] // c

    @pl.when(pl.program_id(1) == 0)
    def _():
        s_ref[...] = jnp.zeros_like(s_ref)

    row = lax.broadcasted_iota(jnp.int32, (c, LANE), 0)
    col = lax.broadcasted_iota(jnp.int32, (c, LANE), 1) % c
    low = lax.broadcasted_iota(jnp.int32, (c, LANE), 1) < c
    eye2_b = row == col
    eye2 = eye2_b.astype(F32)
    masks2 = ((col <= row, col < row, row <= col), (col >= row, col > row, row >= col))
    neg = jnp.float32(-1e30)
    sides =((qf_ref, kf_ref, vf_ref, gf_ref, of_ref), (qb_ref, kb_ref, vb_ref, gb_ref, ob_ref))

    def diag2(a, b):
        a16, b16 = a.astype(BF16), b.astype(BF16)
        za = jnp.zeros((a16.shape[0], b16.shape[1]), BF16)
        zb = jnp.zeros((b16.shape[0], a16.shape[1]), BF16)
        return jnp.concatenate([jnp.concatenate([a16, za], axis=1), jnp.concatenate([zb, b16], axis=1)], axis=0)

    def chunk(i, carry):
        groups = []
        for u, d in [(u, d) for u in range(DN_UNROLL) for d in range(2)]:
            j = i * DN_UNROLL + u
            jj = (n_sub - 1 - j) if d else j
            rows = pl.ds(pl.multiple_of(jj * c, c), c)
            groups.append(dict(u=u, d=d, rows=rows, gb=sides[d][3][0, rows, :]))
        chains, pairs = [], []
        for grp in groups:
            d, gb = grp["d"], grp["gb"]
            for h0 in range(0, DN_HEADS, 2):
                cums = [gb[:, d * DN_HEADS + h:d * DN_HEADS + h + 1] for h in (h0, h0 + 1)]
                pairs.append(dict(a=len(chains), b=len(chains) + 1, strict=masks2[d][1], incl=masks2[d][0],
                                  cum=jnp.where(low, cums[0], cums[1])))
                for t, h in enumerate((h0, h0 + 1)):
                    last = cums[t][0:1] if d else cums[t][c - 1:c]
                    chains.append(dict(u=grp["u"], slot=d * DN_HEADS + h, rows=grp["rows"], o_ref=sides[d][4],
                                       lanes=slice(h * DN_DK, (h + 1) * DN_DK), refs=sides[d], cum=cums[t], g_tot=last,
                                       beta=gb[:, (2 + d) * DN_HEADS + h:(2 + d) * DN_HEADS + h + 1]))
        for p in pairs:
            p["cum_row"] = jnp.sum(jnp.where(eye2_b, p["cum"], 0.0), axis=0, keepdims=True)
        for p in pairs:
            p["decay"] = jnp.exp(jnp.where(p["incl"], p["cum"] - p["cum_row"], neg))
        for ch in chains:
            ch["eg"] = jnp.exp(ch["cum"])
            ch["e_tot"] = jnp.exp(ch["g_tot"])
            ch["e_tail"] = jnp.exp(ch["g_tot"] - ch["cum"])
        for ch in chains:
            q_ref, k_ref, v_ref = ch["refs"][:3]
            q = q_ref[0, ch["rows"], ch["lanes"]].astype(F32)
            k = k_ref[0, ch["rows"], ch["lanes"]].astype(F32)
            v = v_ref[0, ch["rows"], ch["lanes"]].astype(F32)
            k_beta = k * ch["beta"]
            ch.update(q_dec=q * ch["eg"], k=k, lhs=jnp.concatenate([k_beta, q], axis=0),
                      rhs=jnp.concatenate([v * ch["beta"], k_beta * ch["eg"]], axis=1), k_tail=k * ch["e_tail"])
        gram = [_dot16(jnp.concatenate([chains[p["a"]]["lhs"], chains[p["b"]]["lhs"]], axis=1),
                       diag2(chains[p["a"]]["k"], chains[p["b"]]["k"]), "nt") for p in pairs]
        kks = [g[:c] * p["decay"] for g, p in zip(gram, pairs)]
        qks = [g[c:] * p["decay"] for g, p in zip(gram, pairs)]
        t_inv = _pair_inverses([jnp.where(p["strict"], -kk, 0.0) for kk, p in zip(kks, pairs)], eye2, c)
        uw_p = [_dot16(t, diag2(chains[p["a"]]["rhs"], chains[p["b"]]["rhs"])) for t, p in zip(t_inv, pairs)]
        split = lambda xs: [part for x in xs for part in (x[:, :2 * DN_DV], x[:, 2 * DN_DV:])]
        uw = split(uw_p)
        quw = split([_dot16(qk, diag2(x[:, :2 * DN_DV], x[:, 2 * DN_DV:])) for qk, x in zip(qks, uw_p)])
        kuw = [_dot16(ch["k_tail"], x, "tn") for ch, x in zip(chains, uw)]
        o_a = [ch["q_dec"] - x[:, DN_DV:] for ch, x in zip(chains, quw)]
        state = [s_ref[slot] for slot in range(2 * DN_HEADS)]
        for u in range(DN_UNROLL):
            mine = [n for n, ch in enumerate(chains) if ch["u"] == u]
            cur = [state[chains[n]["slot"]] for n in mine]
            outs = [_dot16(o_a[n], s) + quw[n][:, :DN_DV] for n, s in zip(mine, cur)]
            news = [s * chains[n]["e_tot"] - _dot16(kuw[n][:, DN_DV:], s) + kuw[n][:, :DN_DV] for n, s in zip(mine, cur)]
            for n, o, s_new in zip(mine, outs, news):
                ch = chains[n]
                ch["o_ref"][0, ch["rows"], ch["lanes"]] = o.astype(ch["o_ref"].dtype)
                state[ch["slot"]] = s_new
        for slot in range(2 * DN_HEADS):
            s_ref[slot] = state[slot]
        return carry

    lax.fori_loop(0, n_sub // DN_UNROLL, chunk, 0)


def _dn_scan(q, k, v, gb):
    b, l, _ = q.shape
    t = SCAN_TILE
    n = l // t
    fwd = lambda w: pl.BlockSpec((1, t, w), lambda bi, ti: (bi, ti, 0))
    bwd = lambda w: pl.BlockSpec((1, t, w), lambda bi, ti: (bi, n - 1 - ti, 0))
    w = DN_WIDTH
    return pl.pallas_call(
        _dn_scan_kernel, name="dn_scan",
        grid=(b, n),
        in_specs=[fwd(w), fwd(w), fwd(w), fwd(LANE), bwd(w), bwd(w), bwd(w), bwd(LANE)],
        out_specs=[fwd(w), bwd(w)],
        out_shape=[jax.ShapeDtypeStruct((b, l, w), ACT)] * 2,
        scratch_shapes=[pltpu.VMEM((2 * DN_HEADS, DN_DK, DN_DV), F32)],
        compiler_params=_cparams(("parallel", "arbitrary")),
    )(q, k, v, gb, q, k, v, gb)


def _rk_prep_kernel(x_ref, xp_ref, xn_ref, mu_ref, w2_ref, w0_ref, a2_ref, a0_ref, kk_w_ref, ka_ref, seg_ref,
                    trif_ref, trib_ref, r_ref, k_ref, v_ref, kk_ref, a_ref, lw_ref, cum_ref):
    cur, shift = _tile_and_shifts(x_ref, xp_ref, xn_ref)
    s = cur + mu_ref[0:1, :] * (shift(-1) - cur) + mu_ref[1:2, :] * (shift(1) - cur)
    w = RK_WIDTH
    r = s[:, 0:w]
    k = s[:, w:2 * w]
    v = s[:, 2 * w:3 * w]
    wd = s[:, 3 * w:3 * w + 2 * RK_LORA]
    ad = s[:, 3 * w + 2 * RK_LORA:3 * w + 4 * RK_LORA]
    lora_w = _dot16(jnp.tanh(wd), w2_ref[...])
    lw = -math.exp(-0.5) * _sigmoid(w0_ref[...] + lora_w)
    lw_ref[0] = lw
    c = RK_CHUNK
    parts = _split3(lw)
    halves = ((trif_ref[...], slice(0, w)), (trib_ref[...], slice(w, 2 * w)))
    cum_ref[0] = jnp.concatenate(
        [jnp.concatenate([sum(jnp.dot(tri, p[r0:r0 + c, cols], preferred_element_type=F32) for p in parts)
                          for tri, cols in halves], axis=1) for r0 in range(0, lw.shape[0], c)], axis=0)
    a = _sigmoid(a0_ref[...] + _dot16(ad, a2_ref[...]))
    kk_raw = k * kk_w_ref[...]
    ssq = _dot_exact_rhs(kk_raw * kk_raw, seg_ref[...])
    kk_ref[0] = (kk_raw * lax.rsqrt(ssq + RMS_EPS)).astype(kk_ref.dtype)
    r_ref[0] = r.astype(r_ref.dtype)
    k_ref[0] = (k * (1.0 + (a - 1.0) * ka_ref[...])).astype(k_ref.dtype)
    v_ref[0] = v.astype(v_ref.dtype)
    a_ref[0] = a.astype(a_ref.dtype)


def _seg_ones(width, group):
    i = np.arange(width) // group
    return (i[:, None] == i[None, :]).astype(np.float32)


def _rk_prep(rk, mu, w0, w2, a0, a2, k_k, k_a):
    b, l, wp = rk.shape
    t = ROW_TILE
    w = RK_WIDTH
    mu_p = jnp.zeros((2, wp), F32).at[:, :RK_SHIFT].set(mu.astype(F32))
    w2cat = jnp.zeros((2 * RK_LORA, 2 * w), F32)
    w2cat = w2cat.at[:RK_LORA, :w].set(w2[0]).at[RK_LORA:, w:].set(w2[1]).astype(BF16)
    w0cat = w0.astype(F32).reshape(1, 2 * w)
    a2p = jnp.zeros((2 * RK_LORA, w), F32).at[:RK_LORA].set(a2).astype(BF16)
    idx = np.arange(RK_CHUNK)
    prev, nxt = _halo_specs(t, wp, 0, l)
    blk = lambda width: pl.BlockSpec((1, t, width), lambda bi, ti: (bi, ti, 0))
    return pl.pallas_call(
        _rk_prep_kernel, name="rk_prep",
        grid=(b, l // t),
        in_specs=[blk(wp), prev, nxt, _full((2, wp)), _full((2 * RK_LORA, 2 * w)), _full((1, 2 * w)),
                  _full((2 * RK_LORA, w)), _full((1, w)), _full((1, w)), _full((1, w)), _full((w, w)),
                  _full((RK_CHUNK, RK_CHUNK)), _full((RK_CHUNK, RK_CHUNK))],
        out_specs=[blk(w)] * 5 + [blk(2 * w)] * 2,
        out_shape=[jax.ShapeDtypeStruct((b, l, w), ACT)] * 5 + [jax.ShapeDtypeStruct((b, l, 2 * w), F32)] * 2,
        compiler_params=_cparams(("parallel", "parallel")),
    )(rk, rk, rk, mu_p, w2cat, w0cat, a2p, a0.astype(F32).reshape(1, w), k_k.astype(F32).reshape(1, w),
      k_a.astype(F32).reshape(1, w), jnp.asarray(_seg_ones(w, RK_HEAD), dtype=BF16),
      jnp.asarray(idx[None, :] <= idx[:, None], dtype=BF16), jnp.asarray(idx[None, :] >= idx[:, None], dtype=BF16))


def _rk_scan_kernel(rf_ref, kf_ref, vf_ref, kkf_ref, af_ref, lwf_ref, cumf_ref, rb_ref, kb_ref, vb_ref, kkb_ref,
                    ab_ref, lwb_ref, cumb_ref, yf_ref, yb_ref, s_ref):
    c = RK_CHUNK
    n_sub = rf_ref.shape[1] // c
    n_pair = RK_WIDTH // LANE

    @pl.when(pl.program_id(1) == 0)
    def _():
        s_ref[...] = jnp.zeros_like(s_ref)

    row = lax.broadcasted_iota(jnp.int32, (c, LANE), 0)
    col = lax.broadcasted_iota(jnp.int32, (c, LANE), 1) % RK_HEAD
    eye2 = (row == col).astype(F32)
    masks2 = ((col <= row, col < row), (col >= row, col > row))
    low_half = lax.broadcasted_iota(jnp.int32, (RK_HEAD, LANE), 1) < RK_HEAD
    same_block = (lax.broadcasted_iota(jnp.int32, (LANE, LANE), 0) // RK_HEAD) == (
        lax.broadcasted_iota(jnp.int32, (LANE, LANE), 1) // RK_HEAD)
    sides = ((rf_ref, kf_ref, vf_ref, kkf_ref, af_ref, lwf_ref, cumf_ref, yf_ref),
             (rb_ref, kb_ref, vb_ref, kkb_ref, ab_ref, lwb_ref, cumb_ref, yb_ref))

    def chunk(i, carry):
        chains = []
        for u, d in [(u, d) for u in range(RK_UNROLL) for d in range(2)]:
            r_ref, k_ref, v_ref, kk_ref, a_ref, lw_ref, cum_ref, y_ref = sides[d]
            j = i * RK_UNROLL + u
            jj = (n_sub - 1 - j) if d else j
            rows = pl.ds(pl.multiple_of(jj * c, c), c)
            lw = lw_ref[0, rows, :]
            cum = cum_ref[0, rows, :]
            tot = cum[0:1] if d else cum[c - 1:c]
            e_neg = jnp.exp(-cum)
            e_tail = jnp.exp(tot - cum)
            e_tot = jnp.exp(tot)
            k = k_ref[0, rows, :].astype(F32)
            v = v_ref[0, rows, :].astype(F32)
            kk = kk_ref[0, rows, :].astype(F32)
            b_vec = kk * a_ref[0, rows, :].astype(F32)
            ra = r_ref[0, rows, :].astype(F32) * jnp.exp(cum)
            aa = -kk * jnp.exp(cum - lw)
            bb = b_vec * e_neg
            kb = k * e_neg
            bt = b_vec * e_tail
            kt = k * e_tail
            for g in range(n_pair):
                lanes = slice(g * LANE, (g + 1) * LANE)
                chains.append(dict(
                    u=u, slot=d * n_pair + g, rows=rows, lanes=lanes, y_ref=y_ref, incl=masks2[d][0],
                    strict=masks2[d][1],
                    aa=aa[:, lanes], ra=ra[:, lanes], bb=bb[:, lanes], kb=kb[:, lanes], v=v[:, lanes],
                    bt=bt[:, lanes], kt=kt[:, lanes], e_tot=e_tot[:, lanes]))
        for ch in chains:
            ch["lhs"] = jnp.concatenate([ch["aa"], ch["ra"]], axis=0)
        gram = [_dot16(ch["lhs"], jnp.concatenate([_pair_diag(ch["bb"]), _pair_diag(ch["kb"])], axis=0), "nt")
                for ch in chains]
        gb = [g[:, :LANE] for g in gram]
        gk = [g[:, LANE:] for g in gram]
        a_ab = [jnp.where(ch["strict"], g[:c], 0.0) for g, ch in zip(gb, chains)]
        m_rb = [jnp.where(ch["incl"], g[c:], 0.0) for g, ch in zip(gb, chains)]
        akrk = [jnp.concatenate([jnp.where(ch["strict"], g[:c], 0.0), jnp.where(ch["incl"], g[c:], 0.0)], axis=0)
                for g, ch in zip(gk, chains)]
        avyv = [_dot16(m, _pair_diag(ch["v"])) for m, ch in zip(akrk, chains)]
        t_inv = _pair_inverses(a_ab, eye2, c)
        tq = [_dot16(t, jnp.concatenate([_pair_diag(ch["aa"]), _pair_diag(x[:c])], axis=1))
              for t, ch, x in zip(t_inv, chains, avyv)]
        yy = [_dot16(m, jnp.concatenate([_pair_diag(x[:, :LANE]), _pair_diag(x[:, LANE:])], axis=1))
              for m, x in zip(m_rb, tq)]
        ya = [ch["ra"] + y[:, :LANE] for ch, y in zip(chains, yy)]
        yb = [y[:, LANE:] + x[c:] for y, x in zip(yy, avyv)]
        wm = [jnp.where(same_block, _dot16(x[:, :LANE], ch["bt"], "tn"), 0.0) for x, ch in zip(tq, chains)]
        hc_full = [_dot16(jnp.concatenate([x[:, LANE:], ch["v"]], axis=0),
                          jnp.concatenate([ch["bt"], ch["kt"]], axis=0), "tn") for x, ch in zip(tq, chains)]
        hc = [jnp.where(low_half, x[:RK_HEAD], x[RK_HEAD:]) for x in hc_full]
        state = [s_ref[slot] for slot in range(2 * n_pair)]
        for u in range(RK_UNROLL):
            mine = [n for n, ch in enumerate(chains) if ch["u"] == u]
            cur = [state[chains[n]["slot"]] for n in mine]
            outs = [_dot16(ya[n], _pair_diag(s), "nt") + yb[n] for n, s in zip(mine, cur)]
            news = [s * chains[n]["e_tot"] + _dot16(s, wm[n]) + hc[n] for n, s in zip(mine, cur)]
            for n, y, s_new in zip(mine, outs, news):
                ch = chains[n]
                ch["y_ref"][0, ch["rows"], ch["lanes"]] = y.astype(ch["y_ref"].dtype)
                state[ch["slot"]] = s_new
        for slot in range(2 * n_pair):
            s_ref[slot] = state[slot]
        return carry

    lax.fori_loop(0, n_sub // RK_UNROLL, chunk, 0)


def _rk_scan(r, k, v, kk, a, lw, cum):
    b, l, w = r.shape
    t = RK_SCAN_TILE
    n = l // t
    fwd = pl.BlockSpec((1, t, w), lambda bi, ti: (bi, ti, 0))
    bwd = pl.BlockSpec((1, t, w), lambda bi, ti: (bi, n - 1 - ti, 0))
    bwd_lw = pl.BlockSpec((1, t, w), lambda bi, ti: (bi, n - 1 - ti, 1))
    return pl.pallas_call(
        _rk_scan_kernel, name="rk_scan",
        grid=(b, n),
        in_specs=[fwd] * 7 + [bwd] * 5 + [bwd_lw] * 2,
        out_specs=[fwd, bwd],
        out_shape=[jax.ShapeDtypeStruct((b, l, w), ACT)] * 2,
        scratch_shapes=[pltpu.VMEM((2 * RK_WIDTH // LANE, RK_HEAD, LANE), F32)],
        compiler_params=_cparams(("parallel", "arbitrary")),
    )(r, k, v, kk, a, lw, cum, r, k, v, kk, a, lw, cum)


def _even_mix_kernel(of_ref, ob_ref, dg_ref, dnw_ref, yf_ref, yb_ref, r_ref, k_ref, v_ref, rg_ref,
                     rk_ref, lnw_ref, lnb_ref, segm_ref, seg1_ref, o_ref):
    f32 = lambda ref: ref[0].astype(F32)
    o = f32(of_ref) + f32(ob_ref)
    gate = _silu(f32(dg_ref))
    for h in range(DN_HEADS):
        lanes = slice(h * DN_DV, (h + 1) * DN_DV)
        oh = o[:, lanes]
        ms = jnp.mean(oh * oh, axis=-1, keepdims=True)
        o_ref[0, :, lanes] = (oh * lax.rsqrt(ms + RMS_EPS) * dnw_ref[...] * gate[:, lanes]).astype(o_ref.dtype)
    wkv = f32(yf_ref) + f32(yb_ref)
    mean = _dot_exact_rhs(wkv, segm_ref[...])
    cen = wkv - mean
    var = _dot_exact_rhs(cen * cen, segm_ref[...])
    wkv = cen * lax.rsqrt(var + RK_GN_EPS) * lnw_ref[...] + lnb_ref[...]
    bonus = _dot_exact_rhs(f32(r_ref) * f32(k_ref) * rk_ref[...], seg1_ref[...]) * f32(v_ref)
    o_ref[0, :, DN_WIDTH:] = ((wkv + bonus) * _silu(f32(rg_ref))).astype(o_ref.dtype)


def _even_mix(o_f, o_b, dn_gate, dn_norm, y_f, y_b, r, k, v, rk_gate, r_k, ln_w, ln_b):
    b, l, _ = o_f.shape
    t = ROW_TILE
    w = RK_WIDTH
    blk = lambda width: pl.BlockSpec((1, t, width), lambda bi, ti: (bi, ti, 0))
    seg1 = jnp.asarray(_seg_ones(w, RK_HEAD), dtype=BF16)
    segm = jnp.asarray(_seg_ones(w, RK_HEAD) / RK_HEAD, dtype=BF16)
    return pl.pallas_call(
        _even_mix_kernel, name="even_mix",
        grid=(b, l // t),
        in_specs=[blk(DN_WIDTH), blk(DN_WIDTH), blk(DN_WIDTH), _full((1, DN_DV)),
                  blk(w), blk(w), blk(w), blk(w), blk(w), blk(w),
                  _full((1, w)), _full((1, w)), _full((1, w)), _full((w, w)), _full((w, w))],
        out_specs=blk(DN_WIDTH + w),
        out_shape=jax.ShapeDtypeStruct((b, l, DN_WIDTH + w), ACT),
        compiler_params=_cparams(("parallel", "parallel")),
    )(o_f, o_b, dn_gate, dn_norm.astype(F32).reshape(1, DN_DV), y_f, y_b, r, k, v, rk_gate,
      r_k.astype(F32).reshape(1, w), ln_w.astype(F32).reshape(1, w), ln_b.astype(F32).reshape(1, w), segm, seg1)


def _even_layer(h, w_in, dn_conv, dn_a_log, dn_dt_bias, dn_norm, rk_mu, rk_w0, rk_w2, rk_a0, rk_a2,
                rk_k_k, rk_k_a, rk_r_k, rk_ln_w, rk_ln_b):
    b, l, d = h.shape
    s0 = DN_QKV
    s1 = s0 + DN_AB
    s2 = s1 + DN_WIDTH
    s3 = s2 + RK_SHIFT
    pad = lambda m, width: jnp.pad(m, ((0, 0), (0, width - m.shape[1])))
    widths = (DN_QKV, LANE, DN_WIDTH, RK_SHIFT_PAD, RK_WIDTH)
    w16 = jnp.concatenate([w_in[:, :s0], pad(w_in[:, s0:s1], LANE), w_in[:, s1:s2],
                           pad(w_in[:, s2:s3], RK_SHIFT_PAD), w_in[:, s3:]], axis=1).astype(BF16)
    qkv, ab, dn_gate, rk, rk_gate = _project(h.reshape(b * l, d), w16, widths, (ACT, F32, ACT, ACT, ACT))
    r3 = lambda m: m.reshape(b, l, m.shape[-1])
    q, k, v, gb = _dn_prep(r3(qkv), r3(ab), dn_conv, dn_a_log, dn_dt_bias)
    o_f, o_b = _dn_scan(q, k, v, gb)
    r, kr, vr, kk, a, lw, cum = _rk_prep(r3(rk), rk_mu, rk_w0, rk_w2, rk_a0, rk_a2, rk_k_k, rk_k_a)
    y_f, y_b = _rk_scan(r, kr, vr, kk, a, lw, cum)
    return _even_mix(o_f, o_b, r3(dn_gate), dn_norm, y_f, y_b, r, kr, vr, r3(rk_gate),
                     rk_r_k, rk_ln_w, rk_ln_b)


def _dft_geometry(l):
    nf = 2 * l
    p = nf // DFT_Q
    n1 = p // 2
    k1 = p // 2 + 1
    k1p = -(-k1 // SUBLANE) * SUBLANE
    return nf, p, n1, k1, k1p


@functools.lru_cache(maxsize=None)
def _dft_tables(l):
    nf, p, n1c, k1c, k1p = _dft_geometry(l)
    q = DFT_Q
    n2 = np.arange(q)[:, None, None]
    k1 = np.arange(k1c)[None, :, None]
    n1 = np.arange(n1c)[None, None, :]
    ph = -2.0 * np.pi * (((n1 * k1) % p) / p + ((n2 * k1) % nf) / nf)
    fa = np.zeros((q, 2 * k1p, n1c))
    fa[:, :k1c] = np.cos(ph)
    fa[:, k1p:k1p + k1c] = np.sin(ph)
    wgt = np.full((k1c,), 2.0)
    wgt[0] = 1.0
    wgt[-1] = 1.0
    th = -ph.transpose(0, 2, 1)
    gd = np.zeros((q, n1c, 2 * k1p))
    gd[:, :, :k1c] = np.cos(th) * wgt / nf
    gd[:, :, k1p:k1p + k1c] = -np.sin(th) * wgt / nf
    a = np.arange(q)
    ang = -2.0 * np.pi * ((a[:, None] * a[None, :]) % q) / q
    cr, ci = np.cos(ang), np.sin(ang)
    fb = np.block([[cr, -ci], [ci, cr]])
    fc = np.block([[cr, ci], [-ci, cr]])

    return tuple(m.astype(np.float32).astype(BF16) for m in (fa, fb, fc, gd))


def _fdot(f, x):
    return jnp.dot(f, x.astype(BF16), preferred_element_type=F32)


DFT_GROUP_N2 = 32
DFT_GROUP_K1 = 24


def _k1_group(k1p, most=DFT_GROUP_K1):
    return max(g for g in range(2, most + 1, 2) if k1p % g == 0)
DFT_PITCH = DFT_Q + SUBLANE


def _stage_a(src, y_re, y_im, fa, geo):
    nf, p, n1c, k1c, k1p = geo
    g = DFT_GROUP_N2

    def body(i, carry):
        n2s = [i * g + t for t in range(g)]
        slabs = [src[pl.ds(n2, n1c, stride=DFT_PITCH), :] for n2 in n2s]
        outs = [_fdot(fa[n2], slab) for n2, slab in zip(n2s, slabs)]
        for n2, out in zip(n2s, outs):
            y_re[pl.ds(n2, k1p, stride=DFT_PITCH), :] = out[:k1p]
            y_im[pl.ds(n2, k1p, stride=DFT_PITCH), :] = out[k1p:]
        return carry

    lax.fori_loop(0, DFT_Q // g, body, 0)


def _fdot_pairs(f, xs):
    outs = []
    for a, b in zip(xs[0::2], xs[1::2]):
        z = _fdot(f, jnp.concatenate([a, b], axis=1))
        outs += [z[:, :a.shape[1]], z[:, a.shape[1]:]]
    return outs


def _stage_b(y_re, y_im, i, fb, g):
    k1s = [i * g + t for t in range(g)]
    rows = [pl.ds(pl.multiple_of(k1 * DFT_PITCH, SUBLANE), DFT_Q) for k1 in k1s]
    ws = [jnp.concatenate([y_re[r, :], y_im[r, :]], axis=0) for r in rows]
    return k1s, rows, _fdot_pairs(fb[...], ws)


def _hy_conv_kernel(u_ref, m_ref, skip_ref, hr_ref, hi_ref, fa, fb, fc, gd, o_ref, pad, y_re, y_im, *, geo):
    nf, p, n1c, k1c, k1p = geo
    group = _k1_group(k1p)
    for n1 in range(n1c):
        pad[n1 * DFT_PITCH:n1 * DFT_PITCH + DFT_Q, :] = u_ref[0, n1 * DFT_Q:(n1 + 1) * DFT_Q, :].astype(F32)
    _stage_a(pad, y_re, y_im, fa, geo)

    def mid(i, carry):
        k1s, rows, zs = _stage_b(y_re, y_im, i, fb, group)
        prods = []
        for k1, z in zip(k1s, zs):
            zr, zi = z[:DFT_Q], z[DFT_Q:]
            hrows = pl.ds(pl.multiple_of(k1 * DFT_Q, DFT_Q), DFT_Q)
            hr = hr_ref[hrows, :]
            hi = hi_ref[hrows, :]
            prods.append(jnp.concatenate([zr * hr - zi * hi, zr * hi + zi * hr], axis=0))
        outs = _fdot_pairs(fc[...], prods)
        for r, a in zip(rows, outs):
            y_re[r, :] = a[:DFT_Q]
            y_im[r, :] = a[DFT_Q:]
        return carry

    lax.fori_loop(0, k1p // group, mid, 0)

    def last(i, carry):
        n2s = [i * DFT_GROUP_N2 + t for t in range(DFT_GROUP_N2)]
        ins = [jnp.concatenate([y_re[pl.ds(n2, k1p, stride=DFT_PITCH), :], y_im[pl.ds(n2, k1p, stride=DFT_PITCH), :]],
                               axis=0) for n2 in n2s]
        outs = [_fdot(gd[n2], a) for n2, a in zip(n2s, ins)]
        for n2, out in zip(n2s, outs):
            pad[pl.ds(n2, n1c, stride=DFT_PITCH), :] = out
        return carry

    lax.fori_loop(0, DFT_Q // DFT_GROUP_N2, last, 0)

    skip = skip_ref[...]
    for n1 in range(n1c):
        rows = slice(n1 * DFT_Q, (n1 + 1) * DFT_Q)
        conv = pad[n1 * DFT_PITCH:n1 * DFT_PITCH + DFT_Q, :]
        o_ref[0, rows, :] = (m_ref[0, rows, :].astype(F32)
                             * (conv + skip * u_ref[0, rows, :].astype(F32))).astype(o_ref.dtype)


def _single(shape, index_map):
    return pl.BlockSpec(shape, index_map, pipeline_mode=pl.Buffered(1))


def _hy_conv(u, mult, skip, h_re, h_im, order, tables):
    b, l, ch = u.shape
    geo = _dft_geometry(l)
    nf, p, n1c, k1c, k1p = geo
    ct = HY_CT
    consts = [jnp.asarray(t) for t in tables]
    seq = pl.BlockSpec((1, l, ct), lambda ci, bi: (bi, 0, ci))
    spec = _single((None, k1p * DFT_Q, ct), lambda ci, bi: (order, 0, ci))
    cspecs = [_single(c.shape, (lambda ci, bi, nd=c.ndim: (0,) * nd)) for c in consts]
    ysc = pltpu.VMEM((k1p * DFT_PITCH, ct), F32)
    return pl.pallas_call(
        functools.partial(_hy_conv_kernel, geo=geo), name="hy_conv",
        grid=(ch // ct, b),
        in_specs=[seq, seq, pl.BlockSpec((1, ct), lambda ci, bi: (0, ci)), spec, spec] + cspecs,
        out_specs=seq,
        out_shape=jax.ShapeDtypeStruct((b, l, ch), ACT),
        scratch_shapes=[pltpu.VMEM((n1c * DFT_PITCH, ct), F32), ysc, ysc],
        compiler_params=_cparams(("parallel", "parallel")),
    )(u, mult, skip, h_re, h_im, *consts)


def _hy_mlp_kernel(f_ref, w1_ref, b1_ref, w2_ref, b2_ref, w3_ref, b3_ref, fr_ref, o_ref):
    fr = fr_ref[...]
    hdn = jnp.sin(fr * (_dot32(f_ref[...], w1_ref[...]) + b1_ref[...]))
    hdn = jnp.sin(fr * (_dot32(hdn, w2_ref[...]) + b2_ref[...]))
    o_ref[...] = jnp.sin(fr * (_dot32(hdn, w3_ref[...]) + b3_ref[...]))


def _hy_mlp(feats, w1, b1, w2, b2, w3, b3, freq):
    l = feats.shape[0]
    t = min(l, 1024)
    fw = HY_FW
    row = lambda a: a.astype(F32).reshape(1, fw)
    w1p = jnp.zeros((LANE, fw), F32).at[:HY_EMB].set(w1.astype(F32))
    return pl.pallas_call(
        _hy_mlp_kernel, name="hy_mlp",
        grid=(l // t,),
        in_specs=[pl.BlockSpec((t, LANE), lambda i: (i, 0)), _full((LANE, fw)), _full((1, fw)), _full((fw, fw)),
                  _full((1, fw)), _full((fw, fw)), _full((1, fw)), _full((1, fw))],
        out_specs=pl.BlockSpec((t, fw), lambda i: (i, 0)),
        out_shape=jax.ShapeDtypeStruct((l, fw), F32),
        compiler_params=_cparams(("parallel",)),
    )(feats, w1p, row(b1), w2.astype(F32), row(b2), w3.astype(F32), row(b3), row(freq))


def _hy_filter_kernel(hdn_ref, wf_ref, wb_ref, df_ref, db_ref, fa, fb,
                      hr_ref, hi_ref, filt, yr_f, yi_f, yr_b, yi_b, *, geo):
    nf, p, n1c, k1c, k1p = geo
    group = _k1_group(k1p, SUBLANE)

    def build(w_ref, d_ref, drop_first):
        grp = 4
        w_hi, w_lo = _split2(w_ref[0])
        w_both = jnp.concatenate([w_hi, w_lo], axis=1)
        ct = w_hi.shape[1]

        def body(i, acc):
            n1s = [i * grp + t for t in range(grp)]
            hxs = [hdn_ref[pl.ds(pl.multiple_of(n1 * DFT_Q, DFT_Q), DFT_Q), :] for n1 in n1s]
            splits = [_split2(hx) for hx in hxs]
            first = [jnp.dot(hi, w_both, preferred_element_type=F32) for hi, _ in splits]
            second = [jnp.dot(lo, w_hi, preferred_element_type=F32) for _, lo in splits]
            raw = [a[:, :ct] + a[:, ct:] + b for a, b in zip(first, second)]
            for n1, hx, hv in zip(n1s, hxs, raw):
                hv = hv * jnp.exp(-hx[:, HY_FW:HY_FW + 1] * jnp.abs(d_ref[0]))
                if drop_first:
                    pos = lax.broadcasted_iota(jnp.int32, hv.shape, 0) + n1 * DFT_Q
                    hv = jnp.where(pos == 0, 0.0, hv)
                filt[pl.ds(pl.multiple_of(n1 * DFT_PITCH, SUBLANE), DFT_Q), :] = hv
                acc = acc + jnp.sum(jnp.abs(hv), axis=0, keepdims=True)
            return acc

        return lax.fori_loop(0, n1c // grp, body, jnp.zeros((1, filt.shape[1]), F32))

    l1 = build(wf_ref, df_ref, False)
    _stage_a(filt, yr_f, yi_f, fa, geo)
    l1 = l1 + build(wb_ref, db_ref, True)
    _stage_a(filt, yr_b, yi_b, fa, geo)
    inv = 1.0 / (l1 + RMS_EPS)

    def mid(i, carry):
        k1s, _, zf = _stage_b(yr_f, yi_f, i, fb, group)
        _, _, zb = _stage_b(yr_b, yi_b, i, fb, group)
        for k1, f, b in zip(k1s, zf, zb):
            r = pl.ds(pl.multiple_of(k1 * DFT_Q, DFT_Q), DFT_Q)
            hr_ref[0, r, :] = (f[:DFT_Q] + b[:DFT_Q]) * inv
            hi_ref[0, r, :] = (f[DFT_Q:] - b[DFT_Q:]) * inv
        return carry

    lax.fori_loop(0, k1p // group, mid, 0)


def _hy_filters(hdn, tcol, w_out, deltas, tables, l):
    geo = _dft_geometry(l)
    nf, p, n1c, k1c, k1p = geo
    ch = w_out.shape[1] // (2 * HY_ORDER)
    ct = HY_CT
    nct = ch // ct
    hdn_x = jnp.concatenate([hdn, tcol, jnp.zeros((l, LANE - HY_FW - 1), F32)], axis=1)
    w4 = w_out.astype(F32).reshape(HY_FW, 2 * HY_ORDER, ch).transpose(1, 0, 2)
    w4 = jnp.pad(w4, ((0, 0), (0, LANE - HY_FW), (0, 0)))
    d4 = deltas.astype(F32).reshape(2 * HY_ORDER, 1, ch)
    consts = [jnp.asarray(t) for t in tables[:2]]
    cspecs = [_single(c.shape, (lambda o, ci, nd=c.ndim: (0,) * nd)) for c in consts]
    wspec = lambda d: pl.BlockSpec((1, LANE, ct), lambda o, ci: (2 * o + d, 0, ci))
    dspec = lambda d: pl.BlockSpec((1, 1, ct), lambda o, ci: (2 * o + d, 0, ci))
    ospec = pl.BlockSpec((1, k1p * DFT_Q, ct), lambda o, ci: (o, 0, ci))
    ysc = pltpu.VMEM((k1p * DFT_PITCH, ct), F32)
    return pl.pallas_call(
        functools.partial(_hy_filter_kernel, geo=geo), name="hy_filters",
        grid=(HY_ORDER, nct),
        in_specs=[_single((l, LANE), lambda o, ci: (0, 0)), wspec(0), wspec(1), dspec(0), dspec(1)] + cspecs,
        out_specs=[ospec, ospec],
        out_shape=[jax.ShapeDtypeStruct((HY_ORDER, k1p * DFT_Q, ch), F32)] * 2,
        scratch_shapes=[pltpu.VMEM((n1c * DFT_PITCH, ct), F32), ysc, ysc, ysc, ysc],
        compiler_params=_cparams(("parallel", "parallel")),
    )(hdn_x, w4, w4, d4, d4, *consts)


def _position_features(l):
    bands = (HY_EMB - 1) // 2
    t = jnp.linspace(0.0, 1.0, l, dtype=F32)[:, None]
    f = jnp.linspace(1e-4, bands - 1, bands, dtype=F32)[None, :]
    ang = (2.0 * math.pi / l) * jnp.arange(l, dtype=F32)[:, None] * f
    feats = jnp.concatenate([t, jnp.cos(ang), -jnp.sin(ang)], axis=-1)
    return jnp.pad(feats, ((0, 0), (0, LANE - HY_EMB))), t


def _hy_prep_kernel(x_ref, xp_ref, xn_ref, g_ref, cw_ref, cb_ref, x1_ref, m2_ref, v_ref):
    cur, shift = _tile_and_shifts(x_ref, xp_ref, xn_ref)
    y = shift(-1) * cw_ref[0:1, :] + cur * cw_ref[1:2, :] + shift(1) * cw_ref[2:3, :] + cb_ref[...]
    c = g_ref.shape[2]
    x1_ref[0] = y[:, :c].astype(x1_ref.dtype)
    m2_ref[0] = (y[:, c:2 * c] * _silu(g_ref[0].astype(F32))).astype(m2_ref.dtype)
    v_ref[0] = y[:, 2 * c:].astype(v_ref.dtype)


def _hy_prep(xv, gate, conv_w, conv_b):
    b, l, w3 = xv.shape
    c = gate.shape[2]
    t = ROW_TILE
    prev, nxt = _halo_specs(t, w3, 0, l)
    blk = lambda width: pl.BlockSpec((1, t, width), lambda bi, ti: (bi, ti, 0))
    return pl.pallas_call(
        _hy_prep_kernel, name="hy_prep",
        grid=(b, l // t),
        in_specs=[blk(w3), prev, nxt, blk(c), _full((HY_SHORT, w3)), _full((1, w3))],
        out_specs=[blk(c)] * 3,
        out_shape=[jax.ShapeDtypeStruct((b, l, c), ACT)] * 3,
        compiler_params=_cparams(("parallel", "parallel")),
    )(xv, xv, xv, gate, conv_w.astype(F32), conv_b.astype(F32).reshape(1, w3))


def _odd_layer(h, w_in, conv_w, conv_b, f_w1, f_b1, f_w2, f_b2, f_w3, f_b3, f_freq, f_out, deltas, skip):
    b, l, d = h.shape
    c = skip.shape[1]
    xv, gate = _project(h.reshape(b * l, d), w_in.astype(BF16), (3 * c, c), (ACT, ACT))
    x1, m2, v = _hy_prep(xv.reshape(b, l, 3 * c), gate.reshape(b, l, c), conv_w, conv_b)
    tables = _dft_tables(l)
    feats, tcol = _position_features(l)
    hdn = _hy_mlp(feats, f_w1, f_b1, f_w2, f_b2, f_w3, f_b3, f_freq)
    h_re, h_im = _hy_filters(hdn, tcol, f_out, deltas, tables, l)
    skip = skip.astype(F32)
    z = _hy_conv(v, x1, skip[0:1], h_re, h_im, 0, tables)
    return _hy_conv(z, m2, skip[1:2], h_re, h_im, 1, tables)


def kernel(x, p, even_w_in, dn_conv, dn_a_log, dn_dt_bias, dn_norm, rk_mu, rk_w0, rk_w2, rk_a0, rk_a2, rk_k_k, rk_k_a, rk_r_k, rk_ln_w, rk_ln_b, odd_w_in, hy_conv_w, hy_conv_b, hy_ffn_w1, hy_ffn_b1, hy_ffn_w2, hy_ffn_b2, hy_ffn_w3, hy_ffn_b3, hy_ffn_freq, hy_ffn_out, hy_deltas, hy_skip, w_out, ln_g, ln_b, ple_w, ple_norm, ple_gate):
    b, l, d = x.shape
    depth = p.shape[0]
    alpha = (2.0 * depth) ** 0.25
    h = x
    for i in range(depth):
        j = i // 2
        if i % 2 == 0:
            mix = _even_layer(h, even_w_in[j], dn_conv[j], dn_a_log[j], dn_dt_bias[j], dn_norm[j], rk_mu[j],
                              rk_w0[j], rk_w2[j], rk_a0[j], rk_a2[j], rk_k_k[j].reshape(-1), rk_k_a[j].reshape(-1),
                              rk_r_k[j].reshape(-1), rk_ln_w[j], rk_ln_b[j])
        else:
            mix = _odd_layer(h, odd_w_in[j], hy_conv_w[j], hy_conv_b[j], hy_ffn_w1[j], hy_ffn_b1[j],
                             hy_ffn_w2[j], hy_ffn_b2[j], hy_ffn_w3[j], hy_ffn_b3[j], hy_ffn_freq[j],
                             hy_ffn_out[j], hy_deltas[j], hy_skip[j])
        h2 = _post_layer(h.reshape(b * l, d), mix.reshape(b * l, mix.shape[-1]), p.reshape(depth, b * l, p.shape[-1]),
                         i, w_out[i], ple_w[i], ple_gate[i], ln_g[i], ln_b[i], ple_norm[i], alpha)
        h = h2.reshape(b, l, d)
    return h
```

```python
import functools
import math

import numpy as np
import jax
import jax.numpy as jnp
from jax import lax
from jax.experimental import pallas as pl
from jax.experimental.pallas import tpu as pltpu

F32 = jnp.float32
BF16 = jnp.bfloat16

LN_EPS = 1e-5
RMS_EPS = 1e-6

DN_HEADS = 4
DN_DK = 128
DN_DV = 128
DN_WIDTH = DN_HEADS * DN_DV
DN_QKV = 2 * DN_HEADS * DN_DK + DN_WIDTH
DN_AB = 4 * DN_HEADS
DN_CONV = 5
DN_CHUNK = 64
DN_UNROLL = 4

RK_HEADS = 8
RK_HEAD = 64
RK_WIDTH = RK_HEADS * RK_HEAD
RK_LORA = 64
RK_SHIFT = 3 * RK_WIDTH + 3 * RK_LORA
RK_SHIFT_PAD = 1792
RK_GN_EPS = 64e-5
RK_CHUNK = 64
RK_UNROLL = 2

HY_ORDER = 2
HY_SHORT = 3
HY_EMB = 33
HY_FW = 64

LANE = 128
SUBLANE = 8
DFT_Q = 128
VMEM_LIMIT = 56 * 1024 * 1024

ROW_TILE = 1024
SCAN_TILE = 1024
RK_SCAN_TILE = 1024
HY_CT = 128

HI = lax.Precision.HIGHEST
ACT = BF16


def _cparams(sem):
    return pltpu.CompilerParams(dimension_semantics=sem, vmem_limit_bytes=VMEM_LIMIT)


_DIMS = {
    "nn": (((1,), (0,)), ((), ())),
    "nt": (((1,), (1,)), ((), ())),
    "tn": (((0,), (0,)), ((), ())),
}


def _dot16(a, b, dims="nn"):
    return lax.dot_general(a.astype(BF16), b.astype(BF16), _DIMS[dims], preferred_element_type=F32)


def _dot32(a, b, dims="nn"):
    return lax.dot_general(a.astype(F32), b.astype(F32), _DIMS[dims], precision=HI,
                           preferred_element_type=F32)


def _split2(x):
    hi = x.astype(BF16)
    lo = (x - hi.astype(F32)).astype(BF16)
    return hi, lo


def _dot_exact_rhs(x, m16):
    hi, lo = _split2(x)
    return (jnp.dot(hi, m16, preferred_element_type=F32) + jnp.dot(lo, m16, preferred_element_type=F32))


def _sigmoid(x):
    return 1.0 / (1.0 + jnp.exp(-x))


def _silu(x):
    return x * _sigmoid(x)


def _softplus(x):
    return jnp.maximum(x, 0.0) + jnp.log1p(jnp.exp(-jnp.abs(x)))


HALO = 16


SHIFT_ROWS = 128


def _tile_and_shifts(x_ref, xp_ref, xn_ref):
    assert x_ref.dtype == BF16
    t_idx = pl.program_id(1)
    t = x_ref.shape[1]
    cur16 = x_ref[0]
    zero = jnp.zeros_like(xp_ref[0])
    ext = jnp.concatenate([jnp.where(t_idx > 0, xp_ref[0], zero), cur16,
                           jnp.where(t_idx < pl.num_programs(1) - 1, xn_ref[0], zero)], axis=0)
    k = SHIFT_ROWS + 2 * HALO
    row = lax.broadcasted_iota(jnp.int32, (SHIFT_ROWS, k), 0)
    col = lax.broadcasted_iota(jnp.int32, (SHIFT_ROWS, k), 1)

    def shift(d):
        sel = jnp.where(col == row + (HALO + d), 1.0, 0.0).astype(BF16)
        return jnp.concatenate([jnp.dot(sel, ext[r0:r0 + k], preferred_element_type=F32)
                                for r0 in range(0, t, SHIFT_ROWS)], axis=0)

    return cur16.astype(F32), shift


def _halo_specs(t_rows, width, col, l_total):
    nb = t_rows // HALO
    last = l_total // HALO - 1
    prev = pl.BlockSpec((1, HALO, width), lambda b, t: (b, jnp.maximum(t * nb - 1, 0), col))
    nxt = pl.BlockSpec((1, HALO, width), lambda b, t: (b, jnp.minimum((t + 1) * nb, last), col))
    return prev, nxt


def _full(shape):
    nd = len(shape)
    return pl.BlockSpec(shape, lambda *_: (0,) * nd)


def _proj_kernel(a_ref, w_ref, *o_refs, offs):
    a = a_ref[...].astype(BF16)
    for o_ref, (lo, hi) in zip(o_refs, offs):
        o_ref[...] = jnp.dot(a, w_ref[:, lo:hi], preferred_element_type=F32).astype(o_ref.dtype)


def _project(a, w16, widths, dtypes):
    m, k = a.shape
    offs, o = [], 0
    for w in widths:
        offs.append((o, o + w))
        o += w
    n = o
    tm = ROW_TILE
    return pl.pallas_call(
        functools.partial(_proj_kernel, offs=tuple(offs)), name="project",
        grid=(m // tm,),
        in_specs=[pl.BlockSpec((tm, k), lambda i: (i, 0)), _full((k, n))],
        out_specs=[pl.BlockSpec((tm, w), lambda i: (i, 0)) for w in widths],
        out_shape=[jax.ShapeDtypeStruct((m, w), dt) for w, dt in zip(widths, dtypes)],
        compiler_params=_cparams(("parallel",)),
    )(a, w16)


def _post_kernel(h_ref, mix_ref, p_ref, wo_ref, pw_ref, pg_ref, lng_ref, lnb_ref, pn_ref, o_ref, *, alpha):
    t = alpha * h_ref[...] + jnp.dot(mix_ref[...], wo_ref[...], preferred_element_type=F32)
    mu = jnp.mean(t, axis=-1, keepdims=True)
    tc = t - mu
    var = jnp.mean(tc * tc, axis=-1, keepdims=True)
    y = tc * lax.rsqrt(var + LN_EPS) * lng_ref[...] + lnb_ref[...]
    e = jnp.dot(p_ref[...].astype(BF16), pw_ref[...], preferred_element_type=F32)
    e = e * lax.rsqrt(jnp.mean(e * e, axis=-1, keepdims=True) + RMS_EPS) * pn_ref[...]
    gate = _sigmoid(jnp.dot(y.astype(BF16), pg_ref[...], preferred_element_type=F32))
    o_ref[...] = y + gate * e


def _post_layer(h, mix, p_all, layer, w_out, ple_w, ple_gate, ln_g, ln_b, ple_norm, alpha):
    m, d = h.shape
    pd = p_all.shape[2]
    tm = ROW_TILE
    row = lambda w: pl.BlockSpec((tm, w), lambda i: (i, 0))
    p_spec = pl.BlockSpec((None, tm, pd), lambda i: (layer, i, 0))
    return pl.pallas_call(
        functools.partial(_post_kernel, alpha=alpha), name="post_layer",
        grid=(m // tm,),
        in_specs=[row(d), row(mix.shape[1]), p_spec, _full(w_out.shape), _full(ple_w.shape),
                  _full(ple_gate.shape), _full((1, d)), _full((1, d)), _full((1, d))],
        out_specs=row(d),
        out_shape=jax.ShapeDtypeStruct((m, d), F32),
        compiler_params=_cparams(("parallel",)),
    )(h, mix, p_all, w_out.astype(BF16), ple_w.astype(BF16), ple_gate.astype(BF16),
      ln_g.reshape(1, d), ln_b.reshape(1, d), ple_norm.reshape(1, d))


def _split3(x):
    t1 = x.astype(BF16)
    rem = x - t1.astype(F32)
    t2 = rem.astype(BF16)
    return t1, t2, (rem - t2.astype(F32)).astype(BF16)


def _dn_prep_kernel(x_ref, xp_ref, xn_ref, ab_ref, cw_ref, ga_ref, gbias_ref, trif_ref, trib_ref,
                    q_ref, k_ref, v_ref, gb_ref):
    cur, shift = _tile_and_shifts(x_ref, xp_ref, xn_ref)
    pad = DN_CONV // 2
    acc = cur * cw_ref[pad:pad + 1, :]
    for j in range(DN_CONV):
        if j != pad:
            acc = acc + shift(j - pad) * cw_ref[j:j + 1, :]
    y = _silu(acc)
    nqk = DN_HEADS * DN_DK
    for h in range(DN_HEADS):
        qh = y[:, h * DN_DK:(h + 1) * DN_DK]
        kh = y[:, nqk + h * DN_DK:nqk + (h + 1) * DN_DK]
        qn = lax.rsqrt(jnp.sum(qh * qh, axis=-1, keepdims=True) + RMS_EPS) * (DN_DK ** -0.5)
        kn = lax.rsqrt(jnp.sum(kh * kh, axis=-1, keepdims=True) + RMS_EPS)
        q_ref[0, :, h * DN_DK:(h + 1) * DN_DK] = (qh * qn).astype(q_ref.dtype)
        k_ref[0, :, h * DN_DK:(h + 1) * DN_DK] = (kh * kn).astype(k_ref.dtype)
    v_ref[0] = y[:, 2 * nqk:].astype(v_ref.dtype)
    ab = ab_ref[0]
    lane = lax.broadcasted_iota(jnp.int32, ab.shape, 1)
    g = ga_ref[...] * _softplus(ab + gbias_ref[...])
    parts = _split3(g)
    c = DN_CHUNK
    tri_f, tri_b = trif_ref[...], trib_ref[...]
    chunks = range(0, g.shape[0], c)
    cum_f = jnp.concatenate([sum(jnp.dot(tri_f, p[r:r + c], preferred_element_type=F32) for p in parts)
                             for r in chunks], axis=0)
    cum_b = jnp.concatenate([sum(jnp.dot(tri_b, p[r:r + c], preferred_element_type=F32) for p in parts)
                             for r in chunks], axis=0)
    gb_ref[0] = jnp.where(lane < DN_HEADS, cum_f, jnp.where(lane < 2 * DN_HEADS, cum_b, _sigmoid(ab)))


def _dn_prep(qkv, ab, conv_w, a_log, dt_bias):
    b, l, _ = qkv.shape
    t = SCAN_TILE
    ga = jnp.zeros((1, LANE), F32).at[0, :2 * DN_HEADS].set(-jnp.exp(a_log.astype(F32)).reshape(-1))
    gbias = jnp.zeros((1, LANE), F32).at[0, :2 * DN_HEADS].set(dt_bias.astype(F32).reshape(-1))
    prev, nxt = _halo_specs(t, DN_QKV, 0, l)
    blk = lambda w: pl.BlockSpec((1, t, w), lambda bi, ti: (bi, ti, 0))
    i = np.arange(DN_CHUNK)
    tri_f = jnp.asarray(i[None, :] <= i[:, None], dtype=BF16)
    tri_b = jnp.asarray(i[None, :] >= i[:, None], dtype=BF16)
    return pl.pallas_call(
        _dn_prep_kernel, name="dn_prep",
        grid=(b, l // t),
        in_specs=[blk(DN_QKV), prev, nxt, blk(LANE), _full((DN_CONV, DN_QKV)), _full((1, LANE)), _full((1, LANE)),
                  _full((DN_CHUNK, DN_CHUNK)), _full((DN_CHUNK, DN_CHUNK))],
        out_specs=[blk(DN_WIDTH), blk(DN_WIDTH), blk(DN_WIDTH), blk(LANE)],
        out_shape=[jax.ShapeDtypeStruct((b, l, DN_WIDTH), ACT)] * 3 + [jax.ShapeDtypeStruct((b, l, LANE), F32)],
        compiler_params=_cparams(("parallel", "parallel")),
    )(qkv, qkv, qkv, ab, conv_w.astype(F32), ga, gbias, tri_f, tri_b)


HALF = LANE // 2


def _pair_diag(x):
    low = lax.broadcasted_iota(jnp.int32, x.shape, 1) < HALF
    return jnp.concatenate([jnp.where(low, x, 0.0), jnp.where(low, 0.0, x)], axis=0).astype(BF16)


def _pair_inverses(xs, eye2, c):
    rs = [eye2 + x for x in xs]
    ps = [_dot16(x, _pair_diag(x)) for x in xs]
    for _ in range(int(math.log2(c)) - 2):
        zs = [_dot16(jnp.concatenate([r, p], axis=0), _pair_diag(p)) for r, p in zip(rs, ps)]
        rs = [r + z[:c] for r, z in zip(rs, zs)]
        ps = [z[c:] for z in zs]
    return [r + _dot16(r, _pair_diag(p)) for r, p in zip(rs, ps)]


def _dn_scan_kernel(qf_ref, kf_ref, vf_ref, gf_ref, qb_ref, kb_ref, vb_ref, gb_ref, of_ref, ob_ref, s_ref):
    c = DN_CHUNK
    n_sub = qf_ref.shape[1] // c

    @pl.when(pl.program_id(1) == 0)
    def _():
        s_ref[...] = jnp.zeros_like(s_ref)

    row = lax.broadcasted_iota(jnp.int32, (c, LANE), 0)
    col = lax.broadcasted_iota(jnp.int32, (c, LANE), 1) % c
    low = lax.broadcasted_iota(jnp.int32, (c, LANE), 1) < c
    eye2_b = row == col
    eye2 = eye2_b.astype(F32)
    masks2 = ((col <= row, col < row, row <= col), (col >= row, col > row, row >= col))
    neg = jnp.float32(-1e30)
    sides =((qf_ref, kf_ref, vf_ref, gf_ref, of_ref), (qb_ref, kb_ref, vb_ref, gb_ref, ob_ref))

    def diag2(a, b):
        a16, b16 = a.astype(BF16), b.astype(BF16)
        za = jnp.zeros((a16.shape[0], b16.shape[1]), BF16)
        zb = jnp.zeros((b16.shape[0], a16.shape[1]), BF16)
        return jnp.concatenate([jnp.concatenate([a16, za], axis=1), jnp.concatenate([zb, b16], axis=1)], axis=0)

    def chunk(i, carry):
        groups = []
        for u, d in [(u, d) for u in range(DN_UNROLL) for d in range(2)]:
            j = i * DN_UNROLL + u
            jj = (n_sub - 1 - j) if d else j
            rows = pl.ds(pl.multiple_of(jj * c, c), c)
            groups.append(dict(u=u, d=d, rows=rows, gb=sides[d][3][0, rows, :]))
        chains, pairs = [], []
        for grp in groups:
            d, gb = grp["d"], grp["gb"]
            for h0 in range(0, DN_HEADS, 2):
                cums = [gb[:, d * DN_HEADS + h:d * DN_HEADS + h + 1] for h in (h0, h0 + 1)]
                pairs.append(dict(a=len(chains), b=len(chains) + 1, strict=masks2[d][1], incl=masks2[d][0],
                                  cum=jnp.where(low, cums[0], cums[1])))
                for t, h in enumerate((h0, h0 + 1)):
                    last = cums[t][0:1] if d else cums[t][c - 1:c]
                    chains.append(dict(u=grp["u"], slot=d * DN_HEADS + h, rows=grp["rows"], o_ref=sides[d][4],
                                       lanes=slice(h * DN_DK, (h + 1) * DN_DK), refs=sides[d], cum=cums[t], g_tot=last,
                                       beta=gb[:, (2 + d) * DN_HEADS + h:(2 + d) * DN_HEADS + h + 1]))
        for p in pairs:
            p["cum_row"] = jnp.sum(jnp.where(eye2_b, p["cum"], 0.0), axis=0, keepdims=True)
        for p in pairs:
            p["decay"] = jnp.exp(jnp.where(p["incl"], p["cum"] - p["cum_row"], neg))
        for ch in chains:
            ch["eg"] = jnp.exp(ch["cum"])
            ch["e_tot"] = jnp.exp(ch["g_tot"])
            ch["e_tail"] = jnp.exp(ch["g_tot"] - ch["cum"])
        for ch in chains:
            q_ref, k_ref, v_ref = ch["refs"][:3]
            q = q_ref[0, ch["rows"], ch["lanes"]].astype(F32)
            k = k_ref[0, ch["rows"], ch["lanes"]].astype(F32)
            v = v_ref[0, ch["rows"], ch["lanes"]].astype(F32)
            k_beta = k * ch["beta"]
            ch.update(q_dec=q * ch["eg"], k=k, lhs=jnp.concatenate([k_beta, q], axis=0),
                      rhs=jnp.concatenate([v * ch["beta"], k_beta * ch["eg"]], axis=1), k_tail=k * ch["e_tail"])
        gram = [_dot16(jnp.concatenate([chains[p["a"]]["lhs"], chains[p["b"]]["lhs"]], axis=1),
                       diag2(chains[p["a"]]["k"], chains[p["b"]]["k"]), "nt") for p in pairs]
        kks = [g[:c] * p["decay"] for g, p in zip(gram, pairs)]
        qks = [g[c:] * p["decay"] for g, p in zip(gram, pairs)]
        t_inv = _pair_inverses([jnp.where(p["strict"], -kk, 0.0) for kk, p in zip(kks, pairs)], eye2, c)
        uw_p = [_dot16(t, diag2(chains[p["a"]]["rhs"], chains[p["b"]]["rhs"])) for t, p in zip(t_inv, pairs)]
        split = lambda xs: [part for x in xs for part in (x[:, :2 * DN_DV], x[:, 2 * DN_DV:])]
        uw = split(uw_p)
        quw = split([_dot16(qk, diag2(x[:, :2 * DN_DV], x[:, 2 * DN_DV:])) for qk, x in zip(qks, uw_p)])
        kuw = [_dot16(ch["k_tail"], x, "tn") for ch, x in zip(chains, uw)]
        o_a = [ch["q_dec"] - x[:, DN_DV:] for ch, x in zip(chains, quw)]
        state = [s_ref[slot] for slot in range(2 * DN_HEADS)]
        for u in range(DN_UNROLL):
            mine = [n for n, ch in enumerate(chains) if ch["u"] == u]
            cur = [state[chains[n]["slot"]] for n in mine]
            outs = [_dot16(o_a[n], s) + quw[n][:, :DN_DV] for n, s in zip(mine, cur)]
            news = [s * chains[n]["e_tot"] - _dot16(kuw[n][:, DN_DV:], s) + kuw[n][:, :DN_DV] for n, s in zip(mine, cur)]
            for n, o, s_new in zip(mine, outs, news):
                ch = chains[n]
                ch["o_ref"][0, ch["rows"], ch["lanes"]] = o.astype(ch["o_ref"].dtype)
                state[ch["slot"]] = s_new
        for slot in range(2 * DN_HEADS):
            s_ref[slot] = state[slot]
        return carry

    lax.fori_loop(0, n_sub // DN_UNROLL, chunk, 0)


def _dn_scan(q, k, v, gb):
    b, l, _ = q.shape
    t = SCAN_TILE
    n = l // t
    fwd = lambda w: pl.BlockSpec((1, t, w), lambda bi, ti: (bi, ti, 0))
    bwd = lambda w: pl.BlockSpec((1, t, w), lambda bi, ti: (bi, n - 1 - ti, 0))
    w = DN_WIDTH
    return pl.pallas_call(
        _dn_scan_kernel, name="dn_scan",
        grid=(b, n),
        in_specs=[fwd(w), fwd(w), fwd(w), fwd(LANE), bwd(w), bwd(w), bwd(w), bwd(LANE)],
        out_specs=[fwd(w), bwd(w)],
        out_shape=[jax.ShapeDtypeStruct((b, l, w), ACT)] * 2,
        scratch_shapes=[pltpu.VMEM((2 * DN_HEADS, DN_DK, DN_DV), F32)],
        compiler_params=_cparams(("parallel", "arbitrary")),
    )(q, k, v, gb, q, k, v, gb)


def _rk_prep_kernel(x_ref, xp_ref, xn_ref, mu_ref, w2_ref, w0_ref, a2_ref, a0_ref, kk_w_ref, ka_ref, seg_ref,
                    trif_ref, trib_ref, r_ref, k_ref, v_ref, kk_ref, a_ref, lw_ref, cum_ref):
    cur, shift = _tile_and_shifts(x_ref, xp_ref, xn_ref)
    s = cur + mu_ref[0:1, :] * (shift(-1) - cur) + mu_ref[1:2, :] * (shift(1) - cur)
    w = RK_WIDTH
    r = s[:, 0:w]
    k = s[:, w:2 * w]
    v = s[:, 2 * w:3 * w]
    wd = s[:, 3 * w:3 * w + 2 * RK_LORA]
    ad = s[:, 3 * w + 2 * RK_LORA:3 * w + 4 * RK_LORA]
    lora_w = _dot16(jnp.tanh(wd), w2_ref[...])
    lw = -math.exp(-0.5) * _sigmoid(w0_ref[...] + lora_w)
    lw_ref[0] = lw
    c = RK_CHUNK
    parts = _split3(lw)
    halves = ((trif_ref[...], slice(0, w)), (trib_ref[...], slice(w, 2 * w)))
    cum_ref[0] = jnp.concatenate(
        [jnp.concatenate([sum(jnp.dot(tri, p[r0:r0 + c, cols], preferred_element_type=F32) for p in parts)
                          for tri, cols in halves], axis=1) for r0 in range(0, lw.shape[0], c)], axis=0)
    a = _sigmoid(a0_ref[...] + _dot16(ad, a2_ref[...]))
    kk_raw = k * kk_w_ref[...]
    ssq = _dot_exact_rhs(kk_raw * kk_raw, seg_ref[...])
    kk_ref[0] = (kk_raw * lax.rsqrt(ssq + RMS_EPS)).astype(kk_ref.dtype)
    r_ref[0] = r.astype(r_ref.dtype)
    k_ref[0] = (k * (1.0 + (a - 1.0) * ka_ref[...])).astype(k_ref.dtype)
    v_ref[0] = v.astype(v_ref.dtype)
    a_ref[0] = a.astype(a_ref.dtype)


def _seg_ones(width, group):
    i = np.arange(width) // group
    return (i[:, None] == i[None, :]).astype(np.float32)


def _rk_prep(rk, mu, w0, w2, a0, a2, k_k, k_a):
    b, l, wp = rk.shape
    t = ROW_TILE
    w = RK_WIDTH
    mu_p = jnp.zeros((2, wp), F32).at[:, :RK_SHIFT].set(mu.astype(F32))
    w2cat = jnp.zeros((2 * RK_LORA, 2 * w), F32)
    w2cat = w2cat.at[:RK_LORA, :w].set(w2[0]).at[RK_LORA:, w:].set(w2[1]).astype(BF16)
    w0cat = w0.astype(F32).reshape(1, 2 * w)
    a2p = jnp.zeros((2 * RK_LORA, w), F32).at[:RK_LORA].set(a2).astype(BF16)
    idx = np.arange(RK_CHUNK)
    prev, nxt = _halo_specs(t, wp, 0, l)
    blk = lambda width: pl.BlockSpec((1, t, width), lambda bi, ti: (bi, ti, 0))
    return pl.pallas_call(
        _rk_prep_kernel, name="rk_prep",
        grid=(b, l // t),
        in_specs=[blk(wp), prev, nxt, _full((2, wp)), _full((2 * RK_LORA, 2 * w)), _full((1, 2 * w)),
                  _full((2 * RK_LORA, w)), _full((1, w)), _full((1, w)), _full((1, w)), _full((w, w)),
                  _full((RK_CHUNK, RK_CHUNK)), _full((RK_CHUNK, RK_CHUNK))],
        out_specs=[blk(w)] * 5 + [blk(2 * w)] * 2,
        out_shape=[jax.ShapeDtypeStruct((b, l, w), ACT)] * 5 + [jax.ShapeDtypeStruct((b, l, 2 * w), F32)] * 2,
        compiler_params=_cparams(("parallel", "parallel")),
    )(rk, rk, rk, mu_p, w2cat, w0cat, a2p, a0.astype(F32).reshape(1, w), k_k.astype(F32).reshape(1, w),
      k_a.astype(F32).reshape(1, w), jnp.asarray(_seg_ones(w, RK_HEAD), dtype=BF16),
      jnp.asarray(idx[None, :] <= idx[:, None], dtype=BF16), jnp.asarray(idx[None, :] >= idx[:, None], dtype=BF16))


def _rk_scan_kernel(rf_ref, kf_ref, vf_ref, kkf_ref, af_ref, lwf_ref, cumf_ref, rb_ref, kb_ref, vb_ref, kkb_ref,
                    ab_ref, lwb_ref, cumb_ref, yf_ref, yb_ref, s_ref):
    c = RK_CHUNK
    n_sub = rf_ref.shape[1] // c
    n_pair = RK_WIDTH // LANE

    @pl.when(pl.program_id(1) == 0)
    def _():
        s_ref[...] = jnp.zeros_like(s_ref)

    row = lax.broadcasted_iota(jnp.int32, (c, LANE), 0)
    col = lax.broadcasted_iota(jnp.int32, (c, LANE), 1) % RK_HEAD
    eye2 = (row == col).astype(F32)
    masks2 = ((col <= row, col < row), (col >= row, col > row))
    low_half = lax.broadcasted_iota(jnp.int32, (RK_HEAD, LANE), 1) < RK_HEAD
    same_block = (lax.broadcasted_iota(jnp.int32, (LANE, LANE), 0) // RK_HEAD) == (
        lax.broadcasted_iota(jnp.int32, (LANE, LANE), 1) // RK_HEAD)
    sides = ((rf_ref, kf_ref, vf_ref, kkf_ref, af_ref, lwf_ref, cumf_ref, yf_ref),
             (rb_ref, kb_ref, vb_ref, kkb_ref, ab_ref, lwb_ref, cumb_ref, yb_ref))

    def chunk(i, carry):
        chains = []
        for u, d in [(u, d) for u in range(RK_UNROLL) for d in range(2)]:
            r_ref, k_ref, v_ref, kk_ref, a_ref, lw_ref, cum_ref, y_ref = sides[d]
            j = i * RK_UNROLL + u
            jj = (n_sub - 1 - j) if d else j
            rows = pl.ds(pl.multiple_of(jj * c, c), c)
            lw = lw_ref[0, rows, :]
            cum = cum_ref[0, rows, :]
            tot = cum[0:1] if d else cum[c - 1:c]
            e_neg = jnp.exp(-cum)
            e_tail = jnp.exp(tot - cum)
            e_tot = jnp.exp(tot)
            k = k_ref[0, rows, :].astype(F32)
            v = v_ref[0, rows, :].astype(F32)
            kk = kk_ref[0, rows, :].astype(F32)
            b_vec = kk * a_ref[0, rows, :].astype(F32)
            ra = r_ref[0, rows, :].astype(F32) * jnp.exp(cum)
            aa = -kk * jnp.exp(cum - lw)
            bb = b_vec * e_neg
            kb = k * e_neg
            bt = b_vec * e_tail
            kt = k * e_tail
            for g in range(n_pair):
                lanes = slice(g * LANE, (g + 1) * LANE)
                chains.append(dict(
                    u=u, slot=d * n_pair + g, rows=rows, lanes=lanes, y_ref=y_ref, incl=masks2[d][0],
                    strict=masks2[d][1],
                    aa=aa[:, lanes], ra=ra[:, lanes], bb=bb[:, lanes], kb=kb[:, lanes], v=v[:, lanes],
                    bt=bt[:, lanes], kt=kt[:, lanes], e_tot=e_tot[:, lanes]))
        for ch in chains:
            ch["lhs"] = jnp.concatenate([ch["aa"], ch["ra"]], axis=0)
        gram = [_dot16(ch["lhs"], jnp.concatenate([_pair_diag(ch["bb"]), _pair_diag(ch["kb"])], axis=0), "nt")
                for ch in chains]
        gb = [g[:, :LANE] for g in gram]
        gk = [g[:, LANE:] for g in gram]
        a_ab = [jnp.where(ch["strict"], g[:c], 0.0) for g, ch in zip(gb, chains)]
        m_rb = [jnp.where(ch["incl"], g[c:], 0.0) for g, ch in zip(gb, chains)]
        akrk = [jnp.concatenate([jnp.where(ch["strict"], g[:c], 0.0), jnp.where(ch["incl"], g[c:], 0.0)], axis=0)
                for g, ch in zip(gk, chains)]
        avyv = [_dot16(m, _pair_diag(ch["v"])) for m, ch in zip(akrk, chains)]
        t_inv = _pair_inverses(a_ab, eye2, c)
        tq = [_dot16(t, jnp.concatenate([_pair_diag(ch["aa"]), _pair_diag(x[:c])], axis=1))
              for t, ch, x in zip(t_inv, chains, avyv)]
        yy = [_dot16(m, jnp.concatenate([_pair_diag(x[:, :LANE]), _pair_diag(x[:, LANE:])], axis=1))
              for m, x in zip(m_rb, tq)]
        ya = [ch["ra"] + y[:, :LANE] for ch, y in zip(chains, yy)]
        yb = [y[:, LANE:] + x[c:] for y, x in zip(yy, avyv)]
        wm = [jnp.where(same_block, _dot16(x[:, :LANE], ch["bt"], "tn"), 0.0) for x, ch in zip(tq, chains)]
        hc_full = [_dot16(jnp.concatenate([x[:, LANE:], ch["v"]], axis=0),
                          jnp.concatenate([ch["bt"], ch["kt"]], axis=0), "tn") for x, ch in zip(tq, chains)]
        hc = [jnp.where(low_half, x[:RK_HEAD], x[RK_HEAD:]) for x in hc_full]
        state = [s_ref[slot] for slot in range(2 * n_pair)]
        for u in range(RK_UNROLL):
            mine = [n for n, ch in enumerate(chains) if ch["u"] == u]
            cur = [state[chains[n]["slot"]] for n in mine]
            outs = [_dot16(ya[n], _pair_diag(s), "nt") + yb[n] for n, s in zip(mine, cur)]
            news = [s * chains[n]["e_tot"] + _dot16(s, wm[n]) + hc[n] for n, s in zip(mine, cur)]
            for n, y, s_new in zip(mine, outs, news):
                ch = chains[n]
                ch["y_ref"][0, ch["rows"], ch["lanes"]] = y.astype(ch["y_ref"].dtype)
                state[ch["slot"]] = s_new
        for slot in range(2 * n_pair):
            s_ref[slot] = state[slot]
        return carry

    lax.fori_loop(0, n_sub // RK_UNROLL, chunk, 0)


def _rk_scan(r, k, v, kk, a, lw, cum):
    b, l, w = r.shape
    t = RK_SCAN_TILE
    n = l // t
    fwd = pl.BlockSpec((1, t, w), lambda bi, ti: (bi, ti, 0))
    bwd = pl.BlockSpec((1, t, w), lambda bi, ti: (bi, n - 1 - ti, 0))
    bwd_lw = pl.BlockSpec((1, t, w), lambda bi, ti: (bi, n - 1 - ti, 1))
    return pl.pallas_call(
        _rk_scan_kernel, name="rk_scan",
        grid=(b, n),
        in_specs=[fwd] * 7 + [bwd] * 5 + [bwd_lw] * 2,
        out_specs=[fwd, bwd],
        out_shape=[jax.ShapeDtypeStruct((b, l, w), ACT)] * 2,
        scratch_shapes=[pltpu.VMEM((2 * RK_WIDTH // LANE, RK_HEAD, LANE), F32)],
        compiler_params=_cparams(("parallel", "arbitrary")),
    )(r, k, v, kk, a, lw, cum, r, k, v, kk, a, lw, cum)


def _even_mix_kernel(of_ref, ob_ref, dg_ref, dnw_ref, yf_ref, yb_ref, r_ref, k_ref, v_ref, rg_ref,
                     rk_ref, lnw_ref, lnb_ref, segm_ref, seg1_ref, o_ref):
    f32 = lambda ref: ref[0].astype(F32)
    o = f32(of_ref) + f32(ob_ref)
    gate = _silu(f32(dg_ref))
    for h in range(DN_HEADS):
        lanes = slice(h * DN_DV, (h + 1) * DN_DV)
        oh = o[:, lanes]
        ms = jnp.mean(oh * oh, axis=-1, keepdims=True)
        o_ref[0, :, lanes] = (oh * lax.rsqrt(ms + RMS_EPS) * dnw_ref[...] * gate[:, lanes]).astype(o_ref.dtype)
    wkv = f32(yf_ref) + f32(yb_ref)
    mean = _dot_exact_rhs(wkv, segm_ref[...])
    cen = wkv - mean
    var = _dot_exact_rhs(cen * cen, segm_ref[...])
    wkv = cen * lax.rsqrt(var + RK_GN_EPS) * lnw_ref[...] + lnb_ref[...]
    bonus = _dot_exact_rhs(f32(r_ref) * f32(k_ref) * rk_ref[...], seg1_ref[...]) * f32(v_ref)
    o_ref[0, :, DN_WIDTH:] = ((wkv + bonus) * _silu(f32(rg_ref))).astype(o_ref.dtype)


def _even_mix(o_f, o_b, dn_gate, dn_norm, y_f, y_b, r, k, v, rk_gate, r_k, ln_w, ln_b):
    b, l, _ = o_f.shape
    t = ROW_TILE
    w = RK_WIDTH
    blk = lambda width: pl.BlockSpec((1, t, width), lambda bi, ti: (bi, ti, 0))
    seg1 = jnp.asarray(_seg_ones(w, RK_HEAD), dtype=BF16)
    segm = jnp.asarray(_seg_ones(w, RK_HEAD) / RK_HEAD, dtype=BF16)
    return pl.pallas_call(
        _even_mix_kernel, name="even_mix",
        grid=(b, l // t),
        in_specs=[blk(DN_WIDTH), blk(DN_WIDTH), blk(DN_WIDTH), _full((1, DN_DV)),
                  blk(w), blk(w), blk(w), blk(w), blk(w), blk(w),
                  _full((1, w)), _full((1, w)), _full((1, w)), _full((w, w)), _full((w, w))],
        out_specs=blk(DN_WIDTH + w),
        out_shape=jax.ShapeDtypeStruct((b, l, DN_WIDTH + w), ACT),
        compiler_params=_cparams(("parallel", "parallel")),
    )(o_f, o_b, dn_gate, dn_norm.astype(F32).reshape(1, DN_DV), y_f, y_b, r, k, v, rk_gate,
      r_k.astype(F32).reshape(1, w), ln_w.astype(F32).reshape(1, w), ln_b.astype(F32).reshape(1, w), segm, seg1)


def _even_layer(h, w_in, dn_conv, dn_a_log, dn_dt_bias, dn_norm, rk_mu, rk_w0, rk_w2, rk_a0, rk_a2,
                rk_k_k, rk_k_a, rk_r_k, rk_ln_w, rk_ln_b):
    b, l, d = h.shape
    s0 = DN_QKV
    s1 = s0 + DN_AB
    s2 = s1 + DN_WIDTH
    s3 = s2 + RK_SHIFT
    pad = lambda m, width: jnp.pad(m, ((0, 0), (0, width - m.shape[1])))
    widths = (DN_QKV, LANE, DN_WIDTH, RK_SHIFT_PAD, RK_WIDTH)
    w16 = jnp.concatenate([w_in[:, :s0], pad(w_in[:, s0:s1], LANE), w_in[:, s1:s2],
                           pad(w_in[:, s2:s3], RK_SHIFT_PAD), w_in[:, s3:]], axis=1).astype(BF16)
    qkv, ab, dn_gate, rk, rk_gate = _project(h.reshape(b * l, d), w16, widths, (ACT, F32, ACT, ACT, ACT))
    r3 = lambda m: m.reshape(b, l, m.shape[-1])
    q, k, v, gb = _dn_prep(r3(qkv), r3(ab), dn_conv, dn_a_log, dn_dt_bias)
    o_f, o_b = _dn_scan(q, k, v, gb)
    r, kr, vr, kk, a, lw, cum = _rk_prep(r3(rk), rk_mu, rk_w0, rk_w2, rk_a0, rk_a2, rk_k_k, rk_k_a)
    y_f, y_b = _rk_scan(r, kr, vr, kk, a, lw, cum)
    return _even_mix(o_f, o_b, r3(dn_gate), dn_norm, y_f, y_b, r, kr, vr, r3(rk_gate),
                     rk_r_k, rk_ln_w, rk_ln_b)


def _dft_geometry(l):
    nf = 2 * l
    p = nf // DFT_Q
    n1 = p // 2
    k1 = p // 2 + 1
    k1p = -(-k1 // SUBLANE) * SUBLANE
    return nf, p, n1, k1, k1p


@functools.lru_cache(maxsize=None)
def _dft_tables(l):
    nf, p, n1c, k1c, k1p = _dft_geometry(l)
    q = DFT_Q
    n2 = np.arange(q)[:, None, None]
    k1 = np.arange(k1c)[None, :, None]
    n1 = np.arange(n1c)[None, None, :]
    ph = -2.0 * np.pi * (((n1 * k1) % p) / p + ((n2 * k1) % nf) / nf)
    fa = np.zeros((q, 2 * k1p, n1c))
    fa[:, :k1c] = np.cos(ph)
    fa[:, k1p:k1p + k1c] = np.sin(ph)
    wgt = np.full((k1c,), 2.0)
    wgt[0] = 1.0
    wgt[-1] = 1.0
    th = -ph.transpose(0, 2, 1)
    gd = np.zeros((q, n1c, 2 * k1p))
    gd[:, :, :k1c] = np.cos(th) * wgt / nf
    gd[:, :, k1p:k1p + k1c] = -np.sin(th) * wgt / nf
    a = np.arange(q)
    ang = -2.0 * np.pi * ((a[:, None] * a[None, :]) % q) / q
    cr, ci = np.cos(ang), np.sin(ang)
    fb = np.block([[cr, -ci], [ci, cr]])
    fc = np.block([[cr, ci], [-ci, cr]])

    return tuple(m.astype(np.float32).astype(BF16) for m in (fa, fb, fc, gd))


def _fdot(f, x):
    return jnp.dot(f, x.astype(BF16), preferred_element_type=F32)


DFT_GROUP_N2 = 32
DFT_GROUP_K1 = 24


def _k1_group(k1p, most=DFT_GROUP_K1):
    return max(g for g in range(2, most + 1, 2) if k1p % g == 0)
DFT_PITCH = DFT_Q + SUBLANE


def _stage_a(src, y_re, y_im, fa, geo):
    nf, p, n1c, k1c, k1p = geo
    g = DFT_GROUP_N2

    def body(i, carry):
        n2s = [i * g + t for t in range(g)]
        slabs = [src[pl.ds(n2, n1c, stride=DFT_PITCH), :] for n2 in n2s]
        outs = [_fdot(fa[n2], slab) for n2, slab in zip(n2s, slabs)]
        for n2, out in zip(n2s, outs):
            y_re[pl.ds(n2, k1p, stride=DFT_PITCH), :] = out[:k1p]
            y_im[pl.ds(n2, k1p, stride=DFT_PITCH), :] = out[k1p:]
        return carry

    lax.fori_loop(0, DFT_Q // g, body, 0)


def _fdot_pairs(f, xs):
    outs = []
    for a, b in zip(xs[0::2], xs[1::2]):
        z = _fdot(f, jnp.concatenate([a, b], axis=1))
        outs += [z[:, :a.shape[1]], z[:, a.shape[1]:]]
    return outs


def _stage_b(y_re, y_im, i, fb, g):
    k1s = [i * g + t for t in range(g)]
    rows = [pl.ds(pl.multiple_of(k1 * DFT_PITCH, SUBLANE), DFT_Q) for k1 in k1s]
    ws = [jnp.concatenate([y_re[r, :], y_im[r, :]], axis=0) for r in rows]
    return k1s, rows, _fdot_pairs(fb[...], ws)


def _hy_conv_kernel(u_ref, m_ref, skip_ref, hr_ref, hi_ref, fa, fb, fc, gd, o_ref, pad, y_re, y_im, *, geo):
    nf, p, n1c, k1c, k1p = geo
    group = _k1_group(k1p)
    for n1 in range(n1c):
        pad[n1 * DFT_PITCH:n1 * DFT_PITCH + DFT_Q, :] = u_ref[0, n1 * DFT_Q:(n1 + 1) * DFT_Q, :].astype(F32)
    _stage_a(pad, y_re, y_im, fa, geo)

    def mid(i, carry):
        k1s, rows, zs = _stage_b(y_re, y_im, i, fb, group)
        prods = []
        for k1, z in zip(k1s, zs):
            zr, zi = z[:DFT_Q], z[DFT_Q:]
            hrows = pl.ds(pl.multiple_of(k1 * DFT_Q, DFT_Q), DFT_Q)
            hr = hr_ref[hrows, :]
            hi = hi_ref[hrows, :]
            prods.append(jnp.concatenate([zr * hr - zi * hi, zr * hi + zi * hr], axis=0))
        outs = _fdot_pairs(fc[...], prods)
        for r, a in zip(rows, outs):
            y_re[r, :] = a[:DFT_Q]
            y_im[r, :] = a[DFT_Q:]
        return carry

    lax.fori_loop(0, k1p // group, mid, 0)

    def last(i, carry):
        n2s = [i * DFT_GROUP_N2 + t for t in range(DFT_GROUP_N2)]
        ins = [jnp.concatenate([y_re[pl.ds(n2, k1p, stride=DFT_PITCH), :], y_im[pl.ds(n2, k1p, stride=DFT_PITCH), :]],
                               axis=0) for n2 in n2s]
        outs = [_fdot(gd[n2], a) for n2, a in zip(n2s, ins)]
        for n2, out in zip(n2s, outs):
            pad[pl.ds(n2, n1c, stride=DFT_PITCH), :] = out
        return carry

    lax.fori_loop(0, DFT_Q // DFT_GROUP_N2, last, 0)

    skip = skip_ref[...]
    for n1 in range(n1c):
        rows = slice(n1 * DFT_Q, (n1 + 1) * DFT_Q)
        conv = pad[n1 * DFT_PITCH:n1 * DFT_PITCH + DFT_Q, :]
        o_ref[0, rows, :] = (m_ref[0, rows, :].astype(F32)
                             * (conv + skip * u_ref[0, rows, :].astype(F32))).astype(o_ref.dtype)


def _single(shape, index_map):
    return pl.BlockSpec(shape, index_map, pipeline_mode=pl.Buffered(1))


def _hy_conv(u, mult, skip, h_re, h_im, order, tables):
    b, l, ch = u.shape
    geo = _dft_geometry(l)
    nf, p, n1c, k1c, k1p = geo
    ct = HY_CT
    consts = [jnp.asarray(t) for t in tables]
    seq = pl.BlockSpec((1, l, ct), lambda ci, bi: (bi, 0, ci))
    spec = _single((None, k1p * DFT_Q, ct), lambda ci, bi: (order, 0, ci))
    cspecs = [_single(c.shape, (lambda ci, bi, nd=c.ndim: (0,) * nd)) for c in consts]
    ysc = pltpu.VMEM((k1p * DFT_PITCH, ct), F32)
    return pl.pallas_call(
        functools.partial(_hy_conv_kernel, geo=geo), name="hy_conv",
        grid=(ch // ct, b),
        in_specs=[seq, seq, pl.BlockSpec((1, ct), lambda ci, bi: (0, ci)), spec, spec] + cspecs,
        out_specs=seq,
        out_shape=jax.ShapeDtypeStruct((b, l, ch), ACT),
        scratch_shapes=[pltpu.VMEM((n1c * DFT_PITCH, ct), F32), ysc, ysc],
        compiler_params=_cparams(("parallel", "parallel")),
    )(u, mult, skip, h_re, h_im, *consts)


def _hy_mlp_kernel(f_ref, w1_ref, b1_ref, w2_ref, b2_ref, w3_ref, b3_ref, fr_ref, o_ref):
    fr = fr_ref[...]
    hdn = jnp.sin(fr * (_dot32(f_ref[...], w1_ref[...]) + b1_ref[...]))
    hdn = jnp.sin(fr * (_dot32(hdn, w2_ref[...]) + b2_ref[...]))
    o_ref[...] = jnp.sin(fr * (_dot32(hdn, w3_ref[...]) + b3_ref[...]))


def _hy_mlp(feats, w1, b1, w2, b2, w3, b3, freq):
    l = feats.shape[0]
    t = min(l, 1024)
    fw = HY_FW
    row = lambda a: a.astype(F32).reshape(1, fw)
    w1p = jnp.zeros((LANE, fw), F32).at[:HY_EMB].set(w1.astype(F32))
    return pl.pallas_call(
        _hy_mlp_kernel, name="hy_mlp",
        grid=(l // t,),
        in_specs=[pl.BlockSpec((t, LANE), lambda i: (i, 0)), _full((LANE, fw)), _full((1, fw)), _full((fw, fw)),
                  _full((1, fw)), _full((fw, fw)), _full((1, fw)), _full((1, fw))],
        out_specs=pl.BlockSpec((t, fw), lambda i: (i, 0)),
        out_shape=jax.ShapeDtypeStruct((l, fw), F32),
        compiler_params=_cparams(("parallel",)),
    )(feats, w1p, row(b1), w2.astype(F32), row(b2), w3.astype(F32), row(b3), row(freq))


def _hy_filter_kernel(hdn_ref, wf_ref, wb_ref, df_ref, db_ref, fa, fb,
                      hr_ref, hi_ref, filt, yr_f, yi_f, yr_b, yi_b, *, geo):
    nf, p, n1c, k1c, k1p = geo
    group = _k1_group(k1p, DFT_GROUP_K1 // 2)

    def build(w_ref, d_ref, drop_first):
        grp = 4
        w_hi, w_lo = _split2(w_ref[0])
        w_both = jnp.concatenate([w_hi, w_lo], axis=1)
        ct = w_hi.shape[1]

        def body(i, acc):
            n1s = [i * grp + t for t in range(grp)]
            hxs = [hdn_ref[pl.ds(pl.multiple_of(n1 * DFT_Q, DFT_Q), DFT_Q), :] for n1 in n1s]
            splits = [_split2(hx) for hx in hxs]
            first = [jnp.dot(hi, w_both, preferred_element_type=F32) for hi, _ in splits]
            second = [jnp.dot(lo, w_hi, preferred_element_type=F32) for _, lo in splits]
            raw = [a[:, :ct] + a[:, ct:] + b for a, b in zip(first, second)]
            for n1, hx, hv in zip(n1s, hxs, raw):
                hv = hv * jnp.exp(-hx[:, HY_FW:HY_FW + 1] * jnp.abs(d_ref[0]))
                if drop_first:
                    pos = lax.broadcasted_iota(jnp.int32, hv.shape, 0) + n1 * DFT_Q
                    hv = jnp.where(pos == 0, 0.0, hv)
                filt[pl.ds(pl.multiple_of(n1 * DFT_PITCH, SUBLANE), DFT_Q), :] = hv
                acc = acc + jnp.sum(jnp.abs(hv), axis=0, keepdims=True)
            return acc

        return lax.fori_loop(0, n1c // grp, body, jnp.zeros((1, filt.shape[1]), F32))

    l1 = build(wf_ref, df_ref, False)
    _stage_a(filt, yr_f, yi_f, fa, geo)
    l1 = l1 + build(wb_ref, db_ref, True)
    _stage_a(filt, yr_b, yi_b, fa, geo)
    inv = 1.0 / (l1 + RMS_EPS)

    def mid(i, carry):
        k1s, _, zf = _stage_b(yr_f, yi_f, i, fb, group)
        _, _, zb = _stage_b(yr_b, yi_b, i, fb, group)
        for k1, f, b in zip(k1s, zf, zb):
            r = pl.ds(pl.multiple_of(k1 * DFT_Q, DFT_Q), DFT_Q)
            hr_ref[0, r, :] = (f[:DFT_Q] + b[:DFT_Q]) * inv
            hi_ref[0, r, :] = (f[DFT_Q:] - b[DFT_Q:]) * inv
        return carry

    lax.fori_loop(0, k1p // group, mid, 0)


def _hy_filters(hdn, tcol, w_out, deltas, tables, l):
    geo = _dft_geometry(l)
    nf, p, n1c, k1c, k1p = geo
    ch = w_out.shape[1] // (2 * HY_ORDER)
    ct = HY_CT
    nct = ch // ct
    hdn_x = jnp.concatenate([hdn, tcol, jnp.zeros((l, LANE - HY_FW - 1), F32)], axis=1)
    w4 = w_out.astype(F32).reshape(HY_FW, 2 * HY_ORDER, ch).transpose(1, 0, 2)
    w4 = jnp.pad(w4, ((0, 0), (0, LANE - HY_FW), (0, 0)))
    d4 = deltas.astype(F32).reshape(2 * HY_ORDER, 1, ch)
    consts = [jnp.asarray(t) for t in tables[:2]]
    cspecs = [_single(c.shape, (lambda o, ci, nd=c.ndim: (0,) * nd)) for c in consts]
    wspec = lambda d: pl.BlockSpec((1, LANE, ct), lambda o, ci: (2 * o + d, 0, ci))
    dspec = lambda d: pl.BlockSpec((1, 1, ct), lambda o, ci: (2 * o + d, 0, ci))
    ospec = pl.BlockSpec((1, k1p * DFT_Q, ct), lambda o, ci: (o, 0, ci))
    ysc = pltpu.VMEM((k1p * DFT_PITCH, ct), F32)
    return pl.pallas_call(
        functools.partial(_hy_filter_kernel, geo=geo), name="hy_filters",
        grid=(HY_ORDER, nct),
        in_specs=[_single((l, LANE), lambda o, ci: (0, 0)), wspec(0), wspec(1), dspec(0), dspec(1)] + cspecs,
        out_specs=[ospec, ospec],
        out_shape=[jax.ShapeDtypeStruct((HY_ORDER, k1p * DFT_Q, ch), F32)] * 2,
        scratch_shapes=[pltpu.VMEM((n1c * DFT_PITCH, ct), F32), ysc, ysc, ysc, ysc],
        compiler_params=_cparams(("parallel", "parallel")),
    )(hdn_x, w4, w4, d4, d4, *consts)


def _position_features(l):
    bands = (HY_EMB - 1) // 2
    t = jnp.linspace(0.0, 1.0, l, dtype=F32)[:, None]
    f = jnp.linspace(1e-4, bands - 1, bands, dtype=F32)[None, :]
    ang = (2.0 * math.pi / l) * jnp.arange(l, dtype=F32)[:, None] * f
    feats = jnp.concatenate([t, jnp.cos(ang), -jnp.sin(ang)], axis=-1)
    return jnp.pad(feats, ((0, 0), (0, LANE - HY_EMB))), t


def _hy_prep_kernel(x_ref, xp_ref, xn_ref, g_ref, cw_ref, cb_ref, x1_ref, m2_ref, v_ref):
    cur, shift = _tile_and_shifts(x_ref, xp_ref, xn_ref)
    y = shift(-1) * cw_ref[0:1, :] + cur * cw_ref[1:2, :] + shift(1) * cw_ref[2:3, :] + cb_ref[...]
    c = g_ref.shape[2]
    x1_ref[0] = y[:, :c].astype(x1_ref.dtype)
    m2_ref[0] = (y[:, c:2 * c] * _silu(g_ref[0].astype(F32))).astype(m2_ref.dtype)
    v_ref[0] = y[:, 2 * c:].astype(v_ref.dtype)


def _hy_prep(xv, gate, conv_w, conv_b):
    b, l, w3 = xv.shape
    c = gate.shape[2]
    t = ROW_TILE
    prev, nxt = _halo_specs(t, w3, 0, l)
    blk = lambda width: pl.BlockSpec((1, t, width), lambda bi, ti: (bi, ti, 0))
    return pl.pallas_call(
        _hy_prep_kernel, name="hy_prep",
        grid=(b, l // t),
        in_specs=[blk(w3), prev, nxt, blk(c), _full((HY_SHORT, w3)), _full((1, w3))],
        out_specs=[blk(c)] * 3,
        out_shape=[jax.ShapeDtypeStruct((b, l, c), ACT)] * 3,
        compiler_params=_cparams(("parallel", "parallel")),
    )(xv, xv, xv, gate, conv_w.astype(F32), conv_b.astype(F32).reshape(1, w3))


def _odd_layer(h, w_in, conv_w, conv_b, f_w1, f_b1, f_w2, f_b2, f_w3, f_b3, f_freq, f_out, deltas, skip):
    b, l, d = h.shape
    c = skip.shape[1]
    xv, gate = _project(h.reshape(b * l, d), w_in.astype(BF16), (3 * c, c), (ACT, ACT))
    x1, m2, v = _hy_prep(xv.reshape(b, l, 3 * c), gate.reshape(b, l, c), conv_w, conv_b)
    tables = _dft_tables(l)
    feats, tcol = _position_features(l)
    hdn = _hy_mlp(feats, f_w1, f_b1, f_w2, f_b2, f_w3, f_b3, f_freq)
    h_re, h_im = _hy_filters(hdn, tcol, f_out, deltas, tables, l)
    skip = skip.astype(F32)
    z = _hy_conv(v, x1, skip[0:1], h_re, h_im, 0, tables)
    return _hy_conv(z, m2, skip[1:2], h_re, h_im, 1, tables)


def kernel(x, p, even_w_in, dn_conv, dn_a_log, dn_dt_bias, dn_norm, rk_mu, rk_w0, rk_w2, rk_a0, rk_a2, rk_k_k, rk_k_a, rk_r_k, rk_ln_w, rk_ln_b, odd_w_in, hy_conv_w, hy_conv_b, hy_ffn_w1, hy_ffn_b1, hy_ffn_w2, hy_ffn_b2, hy_ffn_w3, hy_ffn_b3, hy_ffn_freq, hy_ffn_out, hy_deltas, hy_skip, w_out, ln_g, ln_b, ple_w, ple_norm, ple_gate):
    b, l, d = x.shape
    depth = p.shape[0]
    alpha = (2.0 * depth) ** 0.25
    h = x
    for i in range(depth):
        j = i // 2
        if i % 2 == 0:
            mix = _even_layer(h, even_w_in[j], dn_conv[j], dn_a_log[j], dn_dt_bias[j], dn_norm[j], rk_mu[j],
                              rk_w0[j], rk_w2[j], rk_a0[j], rk_a2[j], rk_k_k[j].reshape(-1), rk_k_a[j].reshape(-1),
                              rk_r_k[j].reshape(-1), rk_ln_w[j], rk_ln_b[j])
        else:
            mix = _odd_layer(h, odd_w_in[j], hy_conv_w[j], hy_conv_b[j], hy_ffn_w1[j], hy_ffn_b1[j],
                             hy_ffn_w2[j], hy_ffn_b2[j], hy_ffn_w3[j], hy_ffn_b3[j], hy_ffn_freq[j],
                             hy_ffn_out[j], hy_deltas[j], hy_skip[j])
        h2 = _post_layer(h.reshape(b * l, d), mix.reshape(b * l, mix.shape[-1]), p.reshape(depth, b * l, p.shape[-1]),
                         i, w_out[i], ple_w[i], ple_gate[i], ln_g[i], ln_b[i], ple_norm[i], alpha)
        h = h2.reshape(b, l, d)
    return h
```

```python
import functools
import math

import numpy as np
import jax
import jax.numpy as jnp
from jax import lax
from jax.experimental import pallas as pl
from jax.experimental.pallas import tpu as pltpu

F32 = jnp.float32
BF16 = jnp.bfloat16

LN_EPS = 1e-5
RMS_EPS = 1e-6

DN_HEADS = 4
DN_DK = 128
DN_DV = 128
DN_WIDTH = DN_HEADS * DN_DV
DN_QKV = 2 * DN_HEADS * DN_DK + DN_WIDTH
DN_AB = 4 * DN_HEADS
DN_CONV = 5
DN_CHUNK = 64
DN_UNROLL = 4

RK_HEADS = 8
RK_HEAD = 64
RK_WIDTH = RK_HEADS * RK_HEAD
RK_LORA = 64
RK_SHIFT = 3 * RK_WIDTH + 3 * RK_LORA
RK_SHIFT_PAD = 1792
RK_GN_EPS = 64e-5
RK_CHUNK = 64
RK_UNROLL = 2

HY_ORDER = 2
HY_SHORT = 3
HY_EMB = 33
HY_FW = 64

LANE = 128
SUBLANE = 8
DFT_Q = 128
VMEM_LIMIT = 56 * 1024 * 1024

ROW_TILE = 1024
SCAN_TILE = 1024
RK_SCAN_TILE = 1024
HY_CT = 128

HI = lax.Precision.HIGHEST
ACT = BF16


def _cparams(sem):
    return pltpu.CompilerParams(dimension_semantics=sem, vmem_limit_bytes=VMEM_LIMIT)


_DIMS = {
    "nn": (((1,), (0,)), ((), ())),
    "nt": (((1,), (1,)), ((), ())),
    "tn": (((0,), (0,)), ((), ())),
}


def _dot16(a, b, dims="nn"):
    return lax.dot_general(a.astype(BF16), b.astype(BF16), _DIMS[dims], preferred_element_type=F32)


def _dot32(a, b, dims="nn"):
    return lax.dot_general(a.astype(F32), b.astype(F32), _DIMS[dims], precision=HI,
                           preferred_element_type=F32)


def _split2(x):
    hi = x.astype(BF16)
    lo = (x - hi.astype(F32)).astype(BF16)
    return hi, lo


def _dot_exact_rhs(x, m16):
    hi, lo = _split2(x)
    return (jnp.dot(hi, m16, preferred_element_type=F32) + jnp.dot(lo, m16, preferred_element_type=F32))


def _sigmoid(x):
    return 1.0 / (1.0 + jnp.exp(-x))


def _silu(x):
    return x * _sigmoid(x)


def _softplus(x):
    return jnp.maximum(x, 0.0) + jnp.log1p(jnp.exp(-jnp.abs(x)))


HALO = 16


SHIFT_ROWS = 128


def _tile_and_shifts(x_ref, xp_ref, xn_ref):
    assert x_ref.dtype == BF16
    t_idx = pl.program_id(1)
    t = x_ref.shape[1]
    cur16 = x_ref[0]
    zero = jnp.zeros_like(xp_ref[0])
    ext = jnp.concatenate([jnp.where(t_idx > 0, xp_ref[0], zero), cur16,
                           jnp.where(t_idx < pl.num_programs(1) - 1, xn_ref[0], zero)], axis=0)
    k = SHIFT_ROWS + 2 * HALO
    row = lax.broadcasted_iota(jnp.int32, (SHIFT_ROWS, k), 0)
    col = lax.broadcasted_iota(jnp.int32, (SHIFT_ROWS, k), 1)

    def shift(d):
        sel = jnp.where(col == row + (HALO + d), 1.0, 0.0).astype(BF16)
        return jnp.concatenate([jnp.dot(sel, ext[r0:r0 + k], preferred_element_type=F32)
                                for r0 in range(0, t, SHIFT_ROWS)], axis=0)

    return cur16.astype(F32), shift


def _halo_specs(t_rows, width, col, l_total):
    nb = t_rows // HALO
    last = l_total // HALO - 1
    prev = pl.BlockSpec((1, HALO, width), lambda b, t: (b, jnp.maximum(t * nb - 1, 0), col))
    nxt = pl.BlockSpec((1, HALO, width), lambda b, t: (b, jnp.minimum((t + 1) * nb, last), col))
    return prev, nxt


def _full(shape):
    nd = len(shape)
    return pl.BlockSpec(shape, lambda *_: (0,) * nd)


def _proj_kernel(a_ref, w_ref, *o_refs, offs):
    a = a_ref[...].astype(BF16)
    for o_ref, (lo, hi) in zip(o_refs, offs):
        o_ref[...] = jnp.dot(a, w_ref[:, lo:hi], preferred_element_type=F32).astype(o_ref.dtype)


def _project(a, w16, widths, dtypes):
    m, k = a.shape
    offs, o = [], 0
    for w in widths:
        offs.append((o, o + w))
        o += w
    n = o
    tm = ROW_TILE
    return pl.pallas_call(
        functools.partial(_proj_kernel, offs=tuple(offs)), name="project",
        grid=(m // tm,),
        in_specs=[pl.BlockSpec((tm, k), lambda i: (i, 0)), _full((k, n))],
        out_specs=[pl.BlockSpec((tm, w), lambda i: (i, 0)) for w in widths],
        out_shape=[jax.ShapeDtypeStruct((m, w), dt) for w, dt in zip(widths, dtypes)],
        compiler_params=_cparams(("parallel",)),
    )(a, w16)


def _post_kernel(h_ref, mix_ref, p_ref, wo_ref, pw_ref, pg_ref, lng_ref, lnb_ref, pn_ref, o_ref, *, alpha):
    t = alpha * h_ref[...] + jnp.dot(mix_ref[...], wo_ref[...], preferred_element_type=F32)
    mu = jnp.mean(t, axis=-1, keepdims=True)
    tc = t - mu
    var = jnp.mean(tc * tc, axis=-1, keepdims=True)
    y = tc * lax.rsqrt(var + LN_EPS) * lng_ref[...] + lnb_ref[...]
    e = jnp.dot(p_ref[...].astype(BF16), pw_ref[...], preferred_element_type=F32)
    e = e * lax.rsqrt(jnp.mean(e * e, axis=-1, keepdims=True) + RMS_EPS) * pn_ref[...]
    gate = _sigmoid(jnp.dot(y.astype(BF16), pg_ref[...], preferred_element_type=F32))
    o_ref[...] = y + gate * e


def _post_layer(h, mix, p_all, layer, w_out, ple_w, ple_gate, ln_g, ln_b, ple_norm, alpha):
    m, d = h.shape
    pd = p_all.shape[2]
    tm = ROW_TILE
    row = lambda w: pl.BlockSpec((tm, w), lambda i: (i, 0))
    p_spec = pl.BlockSpec((None, tm, pd), lambda i: (layer, i, 0))
    return pl.pallas_call(
        functools.partial(_post_kernel, alpha=alpha), name="post_layer",
        grid=(m // tm,),
        in_specs=[row(d), row(mix.shape[1]), p_spec, _full(w_out.shape), _full(ple_w.shape),
                  _full(ple_gate.shape), _full((1, d)), _full((1, d)), _full((1, d))],
        out_specs=row(d),
        out_shape=jax.ShapeDtypeStruct((m, d), F32),
        compiler_params=_cparams(("parallel",)),
    )(h, mix, p_all, w_out.astype(BF16), ple_w.astype(BF16), ple_gate.astype(BF16),
      ln_g.reshape(1, d), ln_b.reshape(1, d), ple_norm.reshape(1, d))


def _split3(x):
    t1 = x.astype(BF16)
    rem = x - t1.astype(F32)
    t2 = rem.astype(BF16)
    return t1, t2, (rem - t2.astype(F32)).astype(BF16)


def _dn_prep_kernel(x_ref, xp_ref, xn_ref, ab_ref, cw_ref, ga_ref, gbias_ref, trif_ref, trib_ref,
                    q_ref, k_ref, v_ref, gb_ref):
    cur, shift = _tile_and_shifts(x_ref, xp_ref, xn_ref)
    pad = DN_CONV // 2
    acc = cur * cw_ref[pad:pad + 1, :]
    for j in range(DN_CONV):
        if j != pad:
            acc = acc + shift(j - pad) * cw_ref[j:j + 1, :]
    y = _silu(acc)
    nqk = DN_HEADS * DN_DK
    for h in range(DN_HEADS):
        qh = y[:, h * DN_DK:(h + 1) * DN_DK]
        kh = y[:, nqk + h * DN_DK:nqk + (h + 1) * DN_DK]
        qn = lax.rsqrt(jnp.sum(qh * qh, axis=-1, keepdims=True) + RMS_EPS) * (DN_DK ** -0.5)
        kn = lax.rsqrt(jnp.sum(kh * kh, axis=-1, keepdims=True) + RMS_EPS)
        q_ref[0, :, h * DN_DK:(h + 1) * DN_DK] = (qh * qn).astype(q_ref.dtype)
        k_ref[0, :, h * DN_DK:(h + 1) * DN_DK] = (kh * kn).astype(k_ref.dtype)
    v_ref[0] = y[:, 2 * nqk:].astype(v_ref.dtype)
    ab = ab_ref[0]
    lane = lax.broadcasted_iota(jnp.int32, ab.shape, 1)
    g = ga_ref[...] * _softplus(ab + gbias_ref[...])
    parts = _split3(g)
    c = DN_CHUNK
    tri_f, tri_b = trif_ref[...], trib_ref[...]
    chunks = range(0, g.shape[0], c)
    cum_f = jnp.concatenate([sum(jnp.dot(tri_f, p[r:r + c], preferred_element_type=F32) for p in parts)
                             for r in chunks], axis=0)
    cum_b = jnp.concatenate([sum(jnp.dot(tri_b, p[r:r + c], preferred_element_type=F32) for p in parts)
                             for r in chunks], axis=0)
    gb_ref[0] = jnp.where(lane < DN_HEADS, cum_f, jnp.where(lane < 2 * DN_HEADS, cum_b, _sigmoid(ab)))


def _dn_prep(qkv, ab, conv_w, a_log, dt_bias):
    b, l, _ = qkv.shape
    t = SCAN_TILE
    ga = jnp.zeros((1, LANE), F32).at[0, :2 * DN_HEADS].set(-jnp.exp(a_log.astype(F32)).reshape(-1))
    gbias = jnp.zeros((1, LANE), F32).at[0, :2 * DN_HEADS].set(dt_bias.astype(F32).reshape(-1))
    prev, nxt = _halo_specs(t, DN_QKV, 0, l)
    blk = lambda w: pl.BlockSpec((1, t, w), lambda bi, ti: (bi, ti, 0))
    i = np.arange(DN_CHUNK)
    tri_f = jnp.asarray(i[None, :] <= i[:, None], dtype=BF16)
    tri_b = jnp.asarray(i[None, :] >= i[:, None], dtype=BF16)
    return pl.pallas_call(
        _dn_prep_kernel, name="dn_prep",
        grid=(b, l // t),
        in_specs=[blk(DN_QKV), prev, nxt, blk(LANE), _full((DN_CONV, DN_QKV)), _full((1, LANE)), _full((1, LANE)),
                  _full((DN_CHUNK, DN_CHUNK)), _full((DN_CHUNK, DN_CHUNK))],
        out_specs=[blk(DN_WIDTH), blk(DN_WIDTH), blk(DN_WIDTH), blk(LANE)],
        out_shape=[jax.ShapeDtypeStruct((b, l, DN_WIDTH), ACT)] * 3 + [jax.ShapeDtypeStruct((b, l, LANE), F32)],
        compiler_params=_cparams(("parallel", "parallel")),
    )(qkv, qkv, qkv, ab, conv_w.astype(F32), ga, gbias, tri_f, tri_b)


HALF = LANE // 2


def _pair_diag(x):
    low = lax.broadcasted_iota(jnp.int32, x.shape, 1) < HALF
    return jnp.concatenate([jnp.where(low, x, 0.0), jnp.where(low, 0.0, x)], axis=0).astype(BF16)


def _pair_inverses(xs, eye2, c):
    rs = [eye2 + x for x in xs]
    ps = [_dot16(x, _pair_diag(x)) for x in xs]
    for _ in range(int(math.log2(c)) - 2):
        zs = [_dot16(jnp.concatenate([r, p], axis=0), _pair_diag(p)) for r, p in zip(rs, ps)]
        rs = [r + z[:c] for r, z in zip(rs, zs)]
        ps = [z[c:] for z in zs]
    return [r + _dot16(r, _pair_diag(p)) for r, p in zip(rs, ps)]


def _dn_scan_kernel(qf_ref, kf_ref, vf_ref, gf_ref, qb_ref, kb_ref, vb_ref, gb_ref, of_ref, ob_ref, s_ref):
    c = DN_CHUNK
    n_sub = qf_ref.shape[1] // c

    @pl.when(pl.program_id(1) == 0)
    def _():
        s_ref[...] = jnp.zeros_like(s_ref)

    row = lax.broadcasted_iota(jnp.int32, (c, LANE), 0)
    col = lax.broadcasted_iota(jnp.int32, (c, LANE), 1) % c
    low = lax.broadcasted_iota(jnp.int32, (c, LANE), 1) < c
    eye2_b = row == col
    eye2 = eye2_b.astype(F32)
    masks2 = ((col <= row, col < row, row <= col), (col >= row, col > row, row >= col))
    neg = jnp.float32(-1e30)
    sides =((qf_ref, kf_ref, vf_ref, gf_ref, of_ref), (qb_ref, kb_ref, vb_ref, gb_ref, ob_ref))

    def diag2(a, b):
        a16, b16 = a.astype(BF16), b.astype(BF16)
        za = jnp.zeros((a16.shape[0], b16.shape[1]), BF16)
        zb = jnp.zeros((b16.shape[0], a16.shape[1]), BF16)
        return jnp.concatenate([jnp.concatenate([a16, za], axis=1), jnp.concatenate([zb, b16], axis=1)], axis=0)

    def chunk(i, carry):
        groups = []
        for u, d in [(u, d) for u in range(DN_UNROLL) for d in range(2)]:
            j = i * DN_UNROLL + u
            jj = (n_sub - 1 - j) if d else j
            rows = pl.ds(pl.multiple_of(jj * c, c), c)
            groups.append(dict(u=u, d=d, rows=rows, gb=sides[d][3][0, rows, :]))
        chains, pairs = [], []
        for grp in groups:
            d, gb = grp["d"], grp["gb"]
            for h0 in range(0, DN_HEADS, 2):
                cums = [gb[:, d * DN_HEADS + h:d * DN_HEADS + h + 1] for h in (h0, h0 + 1)]
                pairs.append(dict(a=len(chains), b=len(chains) + 1, strict=masks2[d][1], incl=masks2[d][0],
                                  cum=jnp.where(low, cums[0], cums[1])))
                for t, h in enumerate((h0, h0 + 1)):
                    last = cums[t][0:1] if d else cums[t][c - 1:c]
                    chains.append(dict(u=grp["u"], slot=d * DN_HEADS + h, rows=grp["rows"], o_ref=sides[d][4],
                                       lanes=slice(h * DN_DK, (h + 1) * DN_DK), refs=sides[d], cum=cums[t], g_tot=last,
                                       beta=gb[:, (2 + d) * DN_HEADS + h:(2 + d) * DN_HEADS + h + 1]))
        for p in pairs:
            p["cum_row"] = jnp.sum(jnp.where(eye2_b, p["cum"], 0.0), axis=0, keepdims=True)
        for p in pairs:
            p["decay"] = jnp.exp(jnp.where(p["incl"], p["cum"] - p["cum_row"], neg))
        for ch in chains:
            ch["eg"] = jnp.exp(ch["cum"])
            ch["e_tot"] = jnp.exp(ch["g_tot"])
            ch["e_tail"] = jnp.exp(ch["g_tot"] - ch["cum"])
        for ch in chains:
            q_ref, k_ref, v_ref = ch["refs"][:3]
            q = q_ref[0, ch["rows"], ch["lanes"]].astype(F32)
            k = k_ref[0, ch["rows"], ch["lanes"]].astype(F32)
            v = v_ref[0, ch["rows"], ch["lanes"]].astype(F32)
            k_beta = k * ch["beta"]
            ch.update(q_dec=q * ch["eg"], k=k, lhs=jnp.concatenate([k_beta, q], axis=0),
                      rhs=jnp.concatenate([v * ch["beta"], k_beta * ch["eg"]], axis=1), k_tail=k * ch["e_tail"])
        gram = [_dot16(jnp.concatenate([chains[p["a"]]["lhs"], chains[p["b"]]["lhs"]], axis=1),
                       diag2(chains[p["a"]]["k"], chains[p["b"]]["k"]), "nt") for p in pairs]
        kks = [g[:c] * p["decay"] for g, p in zip(gram, pairs)]
        qks = [g[c:] * p["decay"] for g, p in zip(gram, pairs)]
        t_inv = _pair_inverses([jnp.where(p["strict"], -kk, 0.0) for kk, p in zip(kks, pairs)], eye2, c)
        uw_p = [_dot16(t, diag2(chains[p["a"]]["rhs"], chains[p["b"]]["rhs"])) for t, p in zip(t_inv, pairs)]
        split = lambda xs: [part for x in xs for part in (x[:, :2 * DN_DV], x[:, 2 * DN_DV:])]
        uw = split(uw_p)
        quw = split([_dot16(qk, diag2(x[:, :2 * DN_DV], x[:, 2 * DN_DV:])) for qk, x in zip(qks, uw_p)])
        kuw = [_dot16(ch["k_tail"], x, "tn") for ch, x in zip(chains, uw)]
        o_a = [ch["q_dec"] - x[:, DN_DV:] for ch, x in zip(chains, quw)]
        state = [s_ref[slot] for slot in range(2 * DN_HEADS)]
        for u in range(DN_UNROLL):
            mine = [n for n, ch in enumerate(chains) if ch["u"] == u]
            cur = [state[chains[n]["slot"]] for n in mine]
            outs = [_dot16(o_a[n], s) + quw[n][:, :DN_DV] for n, s in zip(mine, cur)]
            news = [s * chains[n]["e_tot"] - _dot16(kuw[n][:, DN_DV:], s) + kuw[n][:, :DN_DV] for n, s in zip(mine, cur)]
            for n, o, s_new in zip(mine, outs, news):
                ch = chains[n]
                ch["o_ref"][0, ch["rows"], ch["lanes"]] = o.astype(ch["o_ref"].dtype)
                state[ch["slot"]] = s_new
        for slot in range(2 * DN_HEADS):
            s_ref[slot] = state[slot]
        return carry

    lax.fori_loop(0, n_sub // DN_UNROLL, chunk, 0)


def _dn_scan(q, k, v, gb):
    b, l, _ = q.shape
    t = SCAN_TILE
    n = l // t
    fwd = lambda w: pl.BlockSpec((1, t, w), lambda bi, ti: (bi, ti, 0))
    bwd = lambda w: pl.BlockSpec((1, t, w), lambda bi, ti: (bi, n - 1 - ti, 0))
    w = DN_WIDTH
    return pl.pallas_call(
        _dn_scan_kernel, name="dn_scan",
        grid=(b, n),
        in_specs=[fwd(w), fwd(w), fwd(w), fwd(LANE), bwd(w), bwd(w), bwd(w), bwd(LANE)],
        out_specs=[fwd(w), bwd(w)],
        out_shape=[jax.ShapeDtypeStruct((b, l, w), ACT)] * 2,
        scratch_shapes=[pltpu.VMEM((2 * DN_HEADS, DN_DK, DN_DV), F32)],
        compiler_params=_cparams(("parallel", "arbitrary")),
    )(q, k, v, gb, q, k, v, gb)


def _rk_prep_kernel(x_ref, xp_ref, xn_ref, mu_ref, w2_ref, w0_ref, a2_ref, a0_ref, kk_w_ref, ka_ref, seg_ref,
                    trif_ref, trib_ref, r_ref, k_ref, v_ref, kk_ref, a_ref, lw_ref, cum_ref):
    cur, shift = _tile_and_shifts(x_ref, xp_ref, xn_ref)
    s = cur + mu_ref[0:1, :] * (shift(-1) - cur) + mu_ref[1:2, :] * (shift(1) - cur)
    w = RK_WIDTH
    r = s[:, 0:w]
    k = s[:, w:2 * w]
    v = s[:, 2 * w:3 * w]
    wd = s[:, 3 * w:3 * w + 2 * RK_LORA]
    ad = s[:, 3 * w + 2 * RK_LORA:3 * w + 4 * RK_LORA]
    lora_w = _dot16(jnp.tanh(wd), w2_ref[...])
    lw = -math.exp(-0.5) * _sigmoid(w0_ref[...] + lora_w)
    lw_ref[0] = lw
    c = RK_CHUNK
    parts = _split3(lw)
    halves = ((trif_ref[...], slice(0, w)), (trib_ref[...], slice(w, 2 * w)))
    cum_ref[0] = jnp.concatenate(
        [jnp.concatenate([sum(jnp.dot(tri, p[r0:r0 + c, cols], preferred_element_type=F32) for p in parts)
                          for tri, cols in halves], axis=1) for r0 in range(0, lw.shape[0], c)], axis=0)
    a = _sigmoid(a0_ref[...] + _dot16(ad, a2_ref[...]))
    kk_raw = k * kk_w_ref[...]
    ssq = _dot_exact_rhs(kk_raw * kk_raw, seg_ref[...])
    kk_ref[0] = (kk_raw * lax.rsqrt(ssq + RMS_EPS)).astype(kk_ref.dtype)
    r_ref[0] = r.astype(r_ref.dtype)
    k_ref[0] = (k * (1.0 + (a - 1.0) * ka_ref[...])).astype(k_ref.dtype)
    v_ref[0] = v.astype(v_ref.dtype)
    a_ref[0] = a.astype(a_ref.dtype)


def _seg_ones(width, group):
    i = np.arange(width) // group
    return (i[:, None] == i[None, :]).astype(np.float32)


def _rk_prep(rk, mu, w0, w2, a0, a2, k_k, k_a):
    b, l, wp = rk.shape
    t = ROW_TILE
    w = RK_WIDTH
    mu_p = jnp.zeros((2, wp), F32).at[:, :RK_SHIFT].set(mu.astype(F32))
    w2cat = jnp.zeros((2 * RK_LORA, 2 * w), F32)
    w2cat = w2cat.at[:RK_LORA, :w].set(w2[0]).at[RK_LORA:, w:].set(w2[1]).astype(BF16)
    w0cat = w0.astype(F32).reshape(1, 2 * w)
    a2p = jnp.zeros((2 * RK_LORA, w), F32).at[:RK_LORA].set(a2).astype(BF16)
    idx = np.arange(RK_CHUNK)
    prev, nxt = _halo_specs(t, wp, 0, l)
    blk = lambda width: pl.BlockSpec((1, t, width), lambda bi, ti: (bi, ti, 0))
    return pl.pallas_call(
        _rk_prep_kernel, name="rk_prep",
        grid=(b, l // t),
        in_specs=[blk(wp), prev, nxt, _full((2, wp)), _full((2 * RK_LORA, 2 * w)), _full((1, 2 * w)),
                  _full((2 * RK_LORA, w)), _full((1, w)), _full((1, w)), _full((1, w)), _full((w, w)),
                  _full((RK_CHUNK, RK_CHUNK)), _full((RK_CHUNK, RK_CHUNK))],
        out_specs=[blk(w)] * 5 + [blk(2 * w)] * 2,
        out_shape=[jax.ShapeDtypeStruct((b, l, w), ACT)] * 5 + [jax.ShapeDtypeStruct((b, l, 2 * w), F32)] * 2,
        compiler_params=_cparams(("parallel", "parallel")),
    )(rk, rk, rk, mu_p, w2cat, w0cat, a2p, a0.astype(F32).reshape(1, w), k_k.astype(F32).reshape(1, w),
      k_a.astype(F32).reshape(1, w), jnp.asarray(_seg_ones(w, RK_HEAD), dtype=BF16),
      jnp.asarray(idx[None, :] <= idx[:, None], dtype=BF16), jnp.asarray(idx[None, :] >= idx[:, None], dtype=BF16))


def _rk_scan_kernel(rf_ref, kf_ref, vf_ref, kkf_ref, af_ref, lwf_ref, cumf_ref, rb_ref, kb_ref, vb_ref, kkb_ref,
                    ab_ref, lwb_ref, cumb_ref, yf_ref, yb_ref, s_ref):
    c = RK_CHUNK
    n_sub = rf_ref.shape[1] // c
    n_pair = RK_WIDTH // LANE

    @pl.when(pl.program_id(1) == 0)
    def _():
        s_ref[...] = jnp.zeros_like(s_ref)

    row = lax.broadcasted_iota(jnp.int32, (c, LANE), 0)
    col = lax.broadcasted_iota(jnp.int32, (c, LANE), 1) % RK_HEAD
    eye2 = (row == col).astype(F32)
    masks2 = ((col <= row, col < row), (col >= row, col > row))
    low_half = lax.broadcasted_iota(jnp.int32, (RK_HEAD, LANE), 1) < RK_HEAD
    same_block = (lax.broadcasted_iota(jnp.int32, (LANE, LANE), 0) // RK_HEAD) == (
        lax.broadcasted_iota(jnp.int32, (LANE, LANE), 1) // RK_HEAD)
    sides = ((rf_ref, kf_ref, vf_ref, kkf_ref, af_ref, lwf_ref, cumf_ref, yf_ref),
             (rb_ref, kb_ref, vb_ref, kkb_ref, ab_ref, lwb_ref, cumb_ref, yb_ref))

    def chunk(i, carry):
        chains = []
        for u, d in [(u, d) for u in range(RK_UNROLL) for d in range(2)]:
            r_ref, k_ref, v_ref, kk_ref, a_ref, lw_ref, cum_ref, y_ref = sides[d]
            j = i * RK_UNROLL + u
            jj = (n_sub - 1 - j) if d else j
            rows = pl.ds(pl.multiple_of(jj * c, c), c)
            lw = lw_ref[0, rows, :]
            cum = cum_ref[0, rows, :]
            tot = cum[0:1] if d else cum[c - 1:c]
            e_neg = jnp.exp(-cum)
            e_tail = jnp.exp(tot - cum)
            e_tot = jnp.exp(tot)
            k = k_ref[0, rows, :].astype(F32)
            v = v_ref[0, rows, :].astype(F32)
            kk = kk_ref[0, rows, :].astype(F32)
            b_vec = kk * a_ref[0, rows, :].astype(F32)
            ra = r_ref[0, rows, :].astype(F32) * jnp.exp(cum)
            aa = -kk * jnp.exp(cum - lw)
            bb = b_vec * e_neg
            kb = k * e_neg
            bt = b_vec * e_tail
            kt = k * e_tail
            for g in range(n_pair):
                lanes = slice(g * LANE, (g + 1) * LANE)
                chains.append(dict(
                    u=u, slot=d * n_pair + g, rows=rows, lanes=lanes, y_ref=y_ref, incl=masks2[d][0],
                    strict=masks2[d][1],
                    aa=aa[:, lanes], ra=ra[:, lanes], bb=bb[:, lanes], kb=kb[:, lanes], v=v[:, lanes],
                    bt=bt[:, lanes], kt=kt[:, lanes], e_tot=e_tot[:, lanes]))
        for ch in chains:
            ch["lhs"] = jnp.concatenate([ch["aa"], ch["ra"]], axis=0)
        gram = [_dot16(ch["lhs"], jnp.concatenate([_pair_diag(ch["bb"]), _pair_diag(ch["kb"])], axis=0), "nt")
                for ch in chains]
        gb = [g[:, :LANE] for g in gram]
        gk = [g[:, LANE:] for g in gram]
        a_ab = [jnp.where(ch["strict"], g[:c], 0.0) for g, ch in zip(gb, chains)]
        m_rb = [jnp.where(ch["incl"], g[c:], 0.0) for g, ch in zip(gb, chains)]
        akrk = [jnp.concatenate([jnp.where(ch["strict"], g[:c], 0.0), jnp.where(ch["incl"], g[c:], 0.0)], axis=0)
                for g, ch in zip(gk, chains)]
        avyv = [_dot16(m, _pair_diag(ch["v"])) for m, ch in zip(akrk, chains)]
        t_inv = _pair_inverses(a_ab, eye2, c)
        tq = [_dot16(t, jnp.concatenate([_pair_diag(ch["aa"]), _pair_diag(x[:c])], axis=1))
              for t, ch, x in zip(t_inv, chains, avyv)]
        yy = [_dot16(m, jnp.concatenate([_pair_diag(x[:, :LANE]), _pair_diag(x[:, LANE:])], axis=1))
              for m, x in zip(m_rb, tq)]
        ya = [ch["ra"] + y[:, :LANE] for ch, y in zip(chains, yy)]
        yb = [y[:, LANE:] + x[c:] for y, x in zip(yy, avyv)]
        wm = [jnp.where(same_block, _dot16(x[:, :LANE], ch["bt"], "tn"), 0.0) for x, ch in zip(tq, chains)]
        hc_full = [_dot16(jnp.concatenate([x[:, LANE:], ch["v"]], axis=0),
                          jnp.concatenate([ch["bt"], ch["kt"]], axis=0), "tn") for x, ch in zip(tq, chains)]
        hc = [jnp.where(low_half, x[:RK_HEAD], x[RK_HEAD:]) for x in hc_full]
        state = [s_ref[slot] for slot in range(2 * n_pair)]
        for u in range(RK_UNROLL):
            mine = [n for n, ch in enumerate(chains) if ch["u"] == u]
            cur = [state[chains[n]["slot"]] for n in mine]
            outs = [_dot16(ya[n], _pair_diag(s), "nt") + yb[n] for n, s in zip(mine, cur)]
            news = [s * chains[n]["e_tot"] + _dot16(s, wm[n]) + hc[n] for n, s in zip(mine, cur)]
            for n, y, s_new in zip(mine, outs, news):
                ch = chains[n]
                ch["y_ref"][0, ch["rows"], ch["lanes"]] = y.astype(ch["y_ref"].dtype)
                state[ch["slot"]] = s_new
        for slot in range(2 * n_pair):
            s_ref[slot] = state[slot]
        return carry

    lax.fori_loop(0, n_sub // RK_UNROLL, chunk, 0)


def _rk_scan(r, k, v, kk, a, lw, cum):
    b, l, w = r.shape
    t = RK_SCAN_TILE
    n = l // t
    fwd = pl.BlockSpec((1, t, w), lambda bi, ti: (bi, ti, 0))
    bwd = pl.BlockSpec((1, t, w), lambda bi, ti: (bi, n - 1 - ti, 0))
    bwd_lw = pl.BlockSpec((1, t, w), lambda bi, ti: (bi, n - 1 - ti, 1))
    return pl.pallas_call(
        _rk_scan_kernel, name="rk_scan",
        grid=(b, n),
        in_specs=[fwd] * 7 + [bwd] * 5 + [bwd_lw] * 2,
        out_specs=[fwd, bwd],
        out_shape=[jax.ShapeDtypeStruct((b, l, w), ACT)] * 2,
        scratch_shapes=[pltpu.VMEM((2 * RK_WIDTH // LANE, RK_HEAD, LANE), F32)],
        compiler_params=_cparams(("parallel", "arbitrary")),
    )(r, k, v, kk, a, lw, cum, r, k, v, kk, a, lw, cum)


def _even_mix_kernel(of_ref, ob_ref, dg_ref, dnw_ref, yf_ref, yb_ref, r_ref, k_ref, v_ref, rg_ref,
                     rk_ref, lnw_ref, lnb_ref, segm_ref, seg1_ref, o_ref):
    f32 = lambda ref: ref[0].astype(F32)
    o = f32(of_ref) + f32(ob_ref)
    gate = _silu(f32(dg_ref))
    for h in range(DN_HEADS):
        lanes = slice(h * DN_DV, (h + 1) * DN_DV)
        oh = o[:, lanes]
        ms = jnp.mean(oh * oh, axis=-1, keepdims=True)
        o_ref[0, :, lanes] = (oh * lax.rsqrt(ms + RMS_EPS) * dnw_ref[...] * gate[:, lanes]).astype(o_ref.dtype)
    wkv = f32(yf_ref) + f32(yb_ref)
    mean = _dot_exact_rhs(wkv, segm_ref[...])
    cen = wkv - mean
    var = _dot_exact_rhs(cen * cen, segm_ref[...])
    wkv = cen * lax.rsqrt(var + RK_GN_EPS) * lnw_ref[...] + lnb_ref[...]
    bonus = _dot_exact_rhs(f32(r_ref) * f32(k_ref) * rk_ref[...], seg1_ref[...]) * f32(v_ref)
    o_ref[0, :, DN_WIDTH:] = ((wkv + bonus) * _silu(f32(rg_ref))).astype(o_ref.dtype)


def _even_mix(o_f, o_b, dn_gate, dn_norm, y_f, y_b, r, k, v, rk_gate, r_k, ln_w, ln_b):
    b, l, _ = o_f.shape
    t = ROW_TILE
    w = RK_WIDTH
    blk = lambda width: pl.BlockSpec((1, t, width), lambda bi, ti: (bi, ti, 0))
    seg1 = jnp.asarray(_seg_ones(w, RK_HEAD), dtype=BF16)
    segm = jnp.asarray(_seg_ones(w, RK_HEAD) / RK_HEAD, dtype=BF16)
    return pl.pallas_call(
        _even_mix_kernel, name="even_mix",
        grid=(b, l // t),
        in_specs=[blk(DN_WIDTH), blk(DN_WIDTH), blk(DN_WIDTH), _full((1, DN_DV)),
                  blk(w), blk(w), blk(w), blk(w), blk(w), blk(w),
                  _full((1, w)), _full((1, w)), _full((1, w)), _full((w, w)), _full((w, w))],
        out_specs=blk(DN_WIDTH + w),
        out_shape=jax.ShapeDtypeStruct((b, l, DN_WIDTH + w), ACT),
        compiler_params=_cparams(("parallel", "parallel")),
    )(o_f, o_b, dn_gate, dn_norm.astype(F32).reshape(1, DN_DV), y_f, y_b, r, k, v, rk_gate,
      r_k.astype(F32).reshape(1, w), ln_w.astype(F32).reshape(1, w), ln_b.astype(F32).reshape(1, w), segm, seg1)


def _even_layer(h, w_in, dn_conv, dn_a_log, dn_dt_bias, dn_norm, rk_mu, rk_w0, rk_w2, rk_a0, rk_a2,
                rk_k_k, rk_k_a, rk_r_k, rk_ln_w, rk_ln_b):
    b, l, d = h.shape
    s0 = DN_QKV
    s1 = s0 + DN_AB
    s2 = s1 + DN_WIDTH
    s3 = s2 + RK_SHIFT
    pad = lambda m, width: jnp.pad(m, ((0, 0), (0, width - m.shape[1])))
    widths = (DN_QKV, LANE, DN_WIDTH, RK_SHIFT_PAD, RK_WIDTH)
    w16 = jnp.concatenate([w_in[:, :s0], pad(w_in[:, s0:s1], LANE), w_in[:, s1:s2],
                           pad(w_in[:, s2:s3], RK_SHIFT_PAD), w_in[:, s3:]], axis=1).astype(BF16)
    qkv, ab, dn_gate, rk, rk_gate = _project(h.reshape(b * l, d), w16, widths, (ACT, F32, ACT, ACT, ACT))
    r3 = lambda m: m.reshape(b, l, m.shape[-1])
    q, k, v, gb = _dn_prep(r3(qkv), r3(ab), dn_conv, dn_a_log, dn_dt_bias)
    o_f, o_b = _dn_scan(q, k, v, gb)
    r, kr, vr, kk, a, lw, cum = _rk_prep(r3(rk), rk_mu, rk_w0, rk_w2, rk_a0, rk_a2, rk_k_k, rk_k_a)
    y_f, y_b = _rk_scan(r, kr, vr, kk, a, lw, cum)
    return _even_mix(o_f, o_b, r3(dn_gate), dn_norm, y_f, y_b, r, kr, vr, r3(rk_gate),
                     rk_r_k, rk_ln_w, rk_ln_b)


def _dft_geometry(l):
    nf = 2 * l
    p = nf // DFT_Q
    n1 = p // 2
    k1 = p // 2 + 1
    k1p = -(-k1 // SUBLANE) * SUBLANE
    return nf, p, n1, k1, k1p


@functools.lru_cache(maxsize=None)
def _dft_tables(l):
    nf, p, n1c, k1c, k1p = _dft_geometry(l)
    q = DFT_Q
    n2 = np.arange(q)[:, None, None]
    k1 = np.arange(k1c)[None, :, None]
    n1 = np.arange(n1c)[None, None, :]
    ph = -2.0 * np.pi * (((n1 * k1) % p) / p + ((n2 * k1) % nf) / nf)
    fa = np.zeros((q, 2 * k1p, n1c))
    fa[:, :k1c] = np.cos(ph)
    fa[:, k1p:k1p + k1c] = np.sin(ph)
    wgt = np.full((k1c,), 2.0)
    wgt[0] = 1.0
    wgt[-1] = 1.0
    th = -ph.transpose(0, 2, 1)
    gd = np.zeros((q, n1c, 2 * k1p))
    gd[:, :, :k1c] = np.cos(th) * wgt / nf
    gd[:, :, k1p:k1p + k1c] = -np.sin(th) * wgt / nf
    a = np.arange(q)
    ang = -2.0 * np.pi * ((a[:, None] * a[None, :]) % q) / q
    cr, ci = np.cos(ang), np.sin(ang)
    fb = np.block([[cr, -ci], [ci, cr]])
    fc = np.block([[cr, ci], [-ci, cr]])

    return tuple(m.astype(np.float32).astype(BF16) for m in (fa, fb, fc, gd))


def _fdot(f, x):
    return jnp.dot(f, x.astype(BF16), preferred_element_type=F32)


DFT_GROUP_N2 = 32
DFT_GROUP_K1 = 36


def _k1_group(k1p, most=DFT_GROUP_K1):
    return max(g for g in range(2, most + 1, 2) if k1p % g == 0)
DFT_PITCH = DFT_Q + SUBLANE


def _stage_a(src, y_re, y_im, fa, geo):
    nf, p, n1c, k1c, k1p = geo
    g = DFT_GROUP_N2

    def body(i, carry):
        n2s = [i * g + t for t in range(g)]
        slabs = [src[pl.ds(n2, n1c, stride=DFT_PITCH), :] for n2 in n2s]
        outs = [_fdot(fa[n2], slab) for n2, slab in zip(n2s, slabs)]
        for n2, out in zip(n2s, outs):
            y_re[pl.ds(n2, k1p, stride=DFT_PITCH), :] = out[:k1p]
            y_im[pl.ds(n2, k1p, stride=DFT_PITCH), :] = out[k1p:]
        return carry

    lax.fori_loop(0, DFT_Q // g, body, 0)


def _fdot_pairs(f, xs):
    outs = []
    for a, b in zip(xs[0::2], xs[1::2]):
        z = _fdot(f, jnp.concatenate([a, b], axis=1))
        outs += [z[:, :a.shape[1]], z[:, a.shape[1]:]]
    return outs


def _stage_b(y_re, y_im, i, fb, g):
    k1s = [i * g + t for t in range(g)]
    rows = [pl.ds(pl.multiple_of(k1 * DFT_PITCH, SUBLANE), DFT_Q) for k1 in k1s]
    ws = [jnp.concatenate([y_re[r, :], y_im[r, :]], axis=0) for r in rows]
    return k1s, rows, _fdot_pairs(fb[...], ws)


def _hy_conv_kernel(u_ref, m_ref, skip_ref, hr_ref, hi_ref, fa, fb, fc, gd, o_ref, pad, y_re, y_im, *, geo):
    nf, p, n1c, k1c, k1p = geo
    group = _k1_group(k1p)
    for n1 in range(n1c):
        pad[n1 * DFT_PITCH:n1 * DFT_PITCH + DFT_Q, :] = u_ref[0, n1 * DFT_Q:(n1 + 1) * DFT_Q, :].astype(F32)
    _stage_a(pad, y_re, y_im, fa, geo)

    def mid(i, carry):
        k1s, rows, zs = _stage_b(y_re, y_im, i, fb, group)
        prods = []
        for k1, z in zip(k1s, zs):
            zr, zi = z[:DFT_Q], z[DFT_Q:]
            hrows = pl.ds(pl.multiple_of(k1 * DFT_Q, DFT_Q), DFT_Q)
            hr = hr_ref[hrows, :]
            hi = hi_ref[hrows, :]
            prods.append(jnp.concatenate([zr * hr - zi * hi, zr * hi + zi * hr], axis=0))
        outs = _fdot_pairs(fc[...], prods)
        for r, a in zip(rows, outs):
            y_re[r, :] = a[:DFT_Q]
            y_im[r, :] = a[DFT_Q:]
        return carry

    lax.fori_loop(0, k1p // group, mid, 0)

    def last(i, carry):
        n2s = [i * DFT_GROUP_N2 + t for t in range(DFT_GROUP_N2)]
        ins = [jnp.concatenate([y_re[pl.ds(n2, k1p, stride=DFT_PITCH), :], y_im[pl.ds(n2, k1p, stride=DFT_PITCH), :]],
                               axis=0) for n2 in n2s]
        outs = [_fdot(gd[n2], a) for n2, a in zip(n2s, ins)]
        for n2, out in zip(n2s, outs):
            pad[pl.ds(n2, n1c, stride=DFT_PITCH), :] = out
        return carry

    lax.fori_loop(0, DFT_Q // DFT_GROUP_N2, last, 0)

    skip = skip_ref[...]
    for n1 in range(n1c):
        rows = slice(n1 * DFT_Q, (n1 + 1) * DFT_Q)
        conv = pad[n1 * DFT_PITCH:n1 * DFT_PITCH + DFT_Q, :]
        o_ref[0, rows, :] = (m_ref[0, rows, :].astype(F32)
                             * (conv + skip * u_ref[0, rows, :].astype(F32))).astype(o_ref.dtype)


def _single(shape, index_map):
    return pl.BlockSpec(shape, index_map, pipeline_mode=pl.Buffered(1))


def _hy_conv(u, mult, skip, h_re, h_im, order, tables):
    b, l, ch = u.shape
    geo = _dft_geometry(l)
    nf, p, n1c, k1c, k1p = geo
    ct = HY_CT
    consts = [jnp.asarray(t) for t in tables]
    seq = pl.BlockSpec((1, l, ct), lambda ci, bi: (bi, 0, ci))
    spec = _single((None, k1p * DFT_Q, ct), lambda ci, bi: (order, 0, ci))
    cspecs = [_single(c.shape, (lambda ci, bi, nd=c.ndim: (0,) * nd)) for c in consts]
    ysc = pltpu.VMEM((k1p * DFT_PITCH, ct), F32)
    return pl.pallas_call(
        functools.partial(_hy_conv_kernel, geo=geo), name="hy_conv",
        grid=(ch // ct, b),
        in_specs=[seq, seq, pl.BlockSpec((1, ct), lambda ci, bi: (0, ci)), spec, spec] + cspecs,
        out_specs=seq,
        out_shape=jax.ShapeDtypeStruct((b, l, ch), ACT),
        scratch_shapes=[pltpu.VMEM((n1c * DFT_PITCH, ct), F32), ysc, ysc],
        compiler_params=_cparams(("parallel", "parallel")),
    )(u, mult, skip, h_re, h_im, *consts)


def _hy_mlp_kernel(f_ref, w1_ref, b1_ref, w2_ref, b2_ref, w3_ref, b3_ref, fr_ref, o_ref):
    fr = fr_ref[...]
    hdn = jnp.sin(fr * (_dot32(f_ref[...], w1_ref[...]) + b1_ref[...]))
    hdn = jnp.sin(fr * (_dot32(hdn, w2_ref[...]) + b2_ref[...]))
    o_ref[...] = jnp.sin(fr * (_dot32(hdn, w3_ref[...]) + b3_ref[...]))


def _hy_mlp(feats, w1, b1, w2, b2, w3, b3, freq):
    l = feats.shape[0]
    t = min(l, 1024)
    fw = HY_FW
    row = lambda a: a.astype(F32).reshape(1, fw)
    w1p = jnp.zeros((LANE, fw), F32).at[:HY_EMB].set(w1.astype(F32))
    return pl.pallas_call(
        _hy_mlp_kernel, name="hy_mlp",
        grid=(l // t,),
        in_specs=[pl.BlockSpec((t, LANE), lambda i: (i, 0)), _full((LANE, fw)), _full((1, fw)), _full((fw, fw)),
                  _full((1, fw)), _full((fw, fw)), _full((1, fw)), _full((1, fw))],
        out_specs=pl.BlockSpec((t, fw), lambda i: (i, 0)),
        out_shape=jax.ShapeDtypeStruct((l, fw), F32),
        compiler_params=_cparams(("parallel",)),
    )(feats, w1p, row(b1), w2.astype(F32), row(b2), w3.astype(F32), row(b3), row(freq))


def _hy_filter_kernel(hdn_ref, wf_ref, wb_ref, df_ref, db_ref, fa, fb,
                      hr_ref, hi_ref, filt, yr_f, yi_f, yr_b, yi_b, *, geo):
    nf, p, n1c, k1c, k1p = geo
    group = _k1_group(k1p, DFT_GROUP_K1 // 2)

    def build(w_ref, d_ref, drop_first):
        grp = 4
        w_hi, w_lo = _split2(w_ref[0])
        w_both = jnp.concatenate([w_hi, w_lo], axis=1)
        ct = w_hi.shape[1]

        def body(i, acc):
            n1s = [i * grp + t for t in range(grp)]
            hxs = [hdn_ref[pl.ds(pl.multiple_of(n1 * DFT_Q, DFT_Q), DFT_Q), :] for n1 in n1s]
            splits = [_split2(hx) for hx in hxs]
            first = [jnp.dot(hi, w_both, preferred_element_type=F32) for hi, _ in splits]
            second = [jnp.dot(lo, w_hi, preferred_element_type=F32) for _, lo in splits]
            raw = [a[:, :ct] + a[:, ct:] + b for a, b in zip(first, second)]
            for n1, hx, hv in zip(n1s, hxs, raw):
                hv = hv * jnp.exp(-hx[:, HY_FW:HY_FW + 1] * jnp.abs(d_ref[0]))
                if drop_first:
                    pos = lax.broadcasted_iota(jnp.int32, hv.shape, 0) + n1 * DFT_Q
                    hv = jnp.where(pos == 0, 0.0, hv)
                filt[pl.ds(pl.multiple_of(n1 * DFT_PITCH, SUBLANE), DFT_Q), :] = hv
                acc = acc + jnp.sum(jnp.abs(hv), axis=0, keepdims=True)
            return acc

        return lax.fori_loop(0, n1c // grp, body, jnp.zeros((1, filt.shape[1]), F32))

    l1 = build(wf_ref, df_ref, False)
    _stage_a(filt, yr_f, yi_f, fa, geo)
    l1 = l1 + build(wb_ref, db_ref, True)
    _stage_a(filt, yr_b, yi_b, fa, geo)
    inv = 1.0 / (l1 + RMS_EPS)

    def mid(i, carry):
        k1s, _, zf = _stage_b(yr_f, yi_f, i, fb, group)
        _, _, zb = _stage_b(yr_b, yi_b, i, fb, group)
        for k1, f, b in zip(k1s, zf, zb):
            r = pl.ds(pl.multiple_of(k1 * DFT_Q, DFT_Q), DFT_Q)
            hr_ref[0, r, :] = (f[:DFT_Q] + b[:DFT_Q]) * inv
            hi_ref[0, r, :] = (f[DFT_Q:] - b[DFT_Q:]) * inv
        return carry

    lax.fori_loop(0, k1p // group, mid, 0)


def _hy_filters(hdn, tcol, w_out, deltas, tables, l):
    geo = _dft_geometry(l)
    nf, p, n1c, k1c, k1p = geo
    ch = w_out.shape[1] // (2 * HY_ORDER)
    ct = HY_CT
    nct = ch // ct
    hdn_x = jnp.concatenate([hdn, tcol, jnp.zeros((l, LANE - HY_FW - 1), F32)], axis=1)
    w4 = w_out.astype(F32).reshape(HY_FW, 2 * HY_ORDER, ch).transpose(1, 0, 2)
    w4 = jnp.pad(w4, ((0, 0), (0, LANE - HY_FW), (0, 0)))
    d4 = deltas.astype(F32).reshape(2 * HY_ORDER, 1, ch)
    consts = [jnp.asarray(t) for t in tables[:2]]
    cspecs = [_single(c.shape, (lambda o, ci, nd=c.ndim: (0,) * nd)) for c in consts]
    wspec = lambda d: pl.BlockSpec((1, LANE, ct), lambda o, ci: (2 * o + d, 0, ci))
    dspec = lambda d: pl.BlockSpec((1, 1, ct), lambda o, ci: (2 * o + d, 0, ci))
    ospec = pl.BlockSpec((1, k1p * DFT_Q, ct), lambda o, ci: (o, 0, ci))
    ysc = pltpu.VMEM((k1p * DFT_PITCH, ct), F32)
    return pl.pallas_call(
        functools.partial(_hy_filter_kernel, geo=geo), name="hy_filters",
        grid=(HY_ORDER, nct),
        in_specs=[_single((l, LANE), lambda o, ci: (0, 0)), wspec(0), wspec(1), dspec(0), dspec(1)] + cspecs,
        out_specs=[ospec, ospec],
        out_shape=[jax.ShapeDtypeStruct((HY_ORDER, k1p * DFT_Q, ch), F32)] * 2,
        scratch_shapes=[pltpu.VMEM((n1c * DFT_PITCH, ct), F32), ysc, ysc, ysc, ysc],
        compiler_params=_cparams(("parallel", "parallel")),
    )(hdn_x, w4, w4, d4, d4, *consts)


def _position_features(l):
    bands = (HY_EMB - 1) // 2
    t = jnp.linspace(0.0, 1.0, l, dtype=F32)[:, None]
    f = jnp.linspace(1e-4, bands - 1, bands, dtype=F32)[None, :]
    ang = (2.0 * math.pi / l) * jnp.arange(l, dtype=F32)[:, None] * f
    feats = jnp.concatenate([t, jnp.cos(ang), -jnp.sin(ang)], axis=-1)
    return jnp.pad(feats, ((0, 0), (0, LANE - HY_EMB))), t


def _hy_prep_kernel(x_ref, xp_ref, xn_ref, g_ref, cw_ref, cb_ref, x1_ref, m2_ref, v_ref):
    cur, shift = _tile_and_shifts(x_ref, xp_ref, xn_ref)
    y = shift(-1) * cw_ref[0:1, :] + cur * cw_ref[1:2, :] + shift(1) * cw_ref[2:3, :] + cb_ref[...]
    c = g_ref.shape[2]
    x1_ref[0] = y[:, :c].astype(x1_ref.dtype)
    m2_ref[0] = (y[:, c:2 * c] * _silu(g_ref[0].astype(F32))).astype(m2_ref.dtype)
    v_ref[0] = y[:, 2 * c:].astype(v_ref.dtype)


def _hy_prep(xv, gate, conv_w, conv_b):
    b, l, w3 = xv.shape
    c = gate.shape[2]
    t = ROW_TILE
    prev, nxt = _halo_specs(t, w3, 0, l)
    blk = lambda width: pl.BlockSpec((1, t, width), lambda bi, ti: (bi, ti, 0))
    return pl.pallas_call(
        _hy_prep_kernel, name="hy_prep",
        grid=(b, l // t),
        in_specs=[blk(w3), prev, nxt, blk(c), _full((HY_SHORT, w3)), _full((1, w3))],
        out_specs=[blk(c)] * 3,
        out_shape=[jax.ShapeDtypeStruct((b, l, c), ACT)] * 3,
        compiler_params=_cparams(("parallel", "parallel")),
    )(xv, xv, xv, gate, conv_w.astype(F32), conv_b.astype(F32).reshape(1, w3))


def _odd_layer(h, w_in, conv_w, conv_b, f_w1, f_b1, f_w2, f_b2, f_w3, f_b3, f_freq, f_out, deltas, skip):
    b, l, d = h.shape
    c = skip.shape[1]
    xv, gate = _project(h.reshape(b * l, d), w_in.astype(BF16), (3 * c, c), (ACT, ACT))
    x1, m2, v = _hy_prep(xv.reshape(b, l, 3 * c), gate.reshape(b, l, c), conv_w, conv_b)
    tables = _dft_tables(l)
    feats, tcol = _position_features(l)
    hdn = _hy_mlp(feats, f_w1, f_b1, f_w2, f_b2, f_w3, f_b3, f_freq)
    h_re, h_im = _hy_filters(hdn, tcol, f_out, deltas, tables, l)
    skip = skip.astype(F32)
    z = _hy_conv(v, x1, skip[0:1], h_re, h_im, 0, tables)
    return _hy_conv(z, m2, skip[1:2], h_re, h_im, 1, tables)


def kernel(x, p, even_w_in, dn_conv, dn_a_log, dn_dt_bias, dn_norm, rk_mu, rk_w0, rk_w2, rk_a0, rk_a2, rk_k_k, rk_k_a, rk_r_k, rk_ln_w, rk_ln_b, odd_w_in, hy_conv_w, hy_conv_b, hy_ffn_w1, hy_ffn_b1, hy_ffn_w2, hy_ffn_b2, hy_ffn_w3, hy_ffn_b3, hy_ffn_freq, hy_ffn_out, hy_deltas, hy_skip, w_out, ln_g, ln_b, ple_w, ple_norm, ple_gate):
    b, l, d = x.shape
    depth = p.shape[0]
    alpha = (2.0 * depth) ** 0.25
    h = x
    for i in range(depth):
        j = i // 2
        if i % 2 == 0:
            mix = _even_layer(h, even_w_in[j], dn_conv[j], dn_a_log[j], dn_dt_bias[j], dn_norm[j], rk_mu[j],
                              rk_w0[j], rk_w2[j], rk_a0[j], rk_a2[j], rk_k_k[j].reshape(-1), rk_k_a[j].reshape(-1),
                              rk_r_k[j].reshape(-1), rk_ln_w[j], rk_ln_b[j])
        else:
            mix = _odd_layer(h, odd_w_in[j], hy_conv_w[j], hy_conv_b[j], hy_ffn_w1[j], hy_ffn_b1[j],
                             hy_ffn_w2[j], hy_ffn_b2[j], hy_ffn_w3[j], hy_ffn_b3[j], hy_ffn_freq[j],
                             hy_ffn_out[j], hy_deltas[j], hy_skip[j])
        h2 = _post_layer(h.reshape(b * l, d), mix.reshape(b * l, mix.shape[-1]), p.reshape(depth, b * l, p.shape[-1]),
                         i, w_out[i], ple_w[i], ple_gate[i], ln_g[i], ln_b[i], ple_norm[i], alpha)
        h = h2.reshape(b, l, d)
    return h
```
